```python
import math
import jax, jax.numpy as jnp
from jax import lax
import numpy as np

D_MODEL = 2048
BATCH = 2
SEQ = 16384
DEPTH = 4
DEC_BATCH = 32
DEC_SEQ = 16
PAST_LEN = 2048

CHUNK = 64
N_BRANCH = 3
BRANCH_DIM = 1024
SSD_INNER = BRANCH_DIM
SSD_HEAD_DIM = 64
SSD_HEADS = SSD_INNER // SSD_HEAD_DIM
SSD_GROUPS = 2
SSD_STATE = 128
CONV_WIDTH = 4
CONV_DIM = SSD_INNER + 2 * SSD_GROUPS * SSD_STATE
ATTN_HEADS = 16
KV_HEADS = 4
HEAD_DIM = BRANCH_DIM // ATTN_HEADS
Q_PER_KV = ATTN_HEADS // KV_HEADS
IDX_HEADS = 8
IDX_DIM = 64
TOPK_MAX = 256
Q_BLOCK = 128
N_BUCKETS = 32
MAX_DISTANCE = 128
POOL_WINDOWS = (2, 4, 8, 16)
POOL_GROUPS = 4
POOL_GROUP_DIM = BRANCH_DIM // POOL_GROUPS
POOL_STATE = 15

EPS = 1e-6
NEG = -1e30

SPLIT_SIZES = (
    SSD_INNER,
    CONV_DIM,
    SSD_HEADS,
    ATTN_HEADS * HEAD_DIM,
    KV_HEADS * HEAD_DIM,
    KV_HEADS * HEAD_DIM,
    BRANCH_DIM,
    IDX_HEADS * IDX_DIM,
    IDX_DIM,
    IDX_HEADS,
    BRANCH_DIM,
    BRANCH_DIM,
    N_BRANCH * D_MODEL,
)
IN_DIM = sum(SPLIT_SIZES)

kernel_name = 'hybrid_ssd_dsa_pool_stream_step'


def rms_norm(x, w):
    xf = x.astype(jnp.float32)
    y = xf * lax.rsqrt(jnp.mean(xf * xf, axis=-1, keepdims=True) + EPS)
    return (y * w.astype(jnp.float32)).astype(x.dtype)


def split_columns(proj):
    outs = []
    off = 0
    for size in SPLIT_SIZES:
        outs.append(proj[..., off:off + size])
        off += size
    return outs


def t5_bucket(rel):
    half = N_BUCKETS // 2
    max_exact = half // 2
    ret = jnp.where(rel > 0, half, 0)
    n = jnp.abs(rel)
    nf = jnp.maximum(n, max_exact).astype(jnp.float32)
    large = max_exact + (jnp.log(nf / max_exact) / math.log(MAX_DISTANCE / max_exact)
                         * (half - max_exact)).astype(jnp.int32)
    large = jnp.minimum(large, half - 1)
    return ret + jnp.where(n < max_exact, n, large)


def causal_dwconv(x_ext, w, b):
    ch = x_ext.shape[-1]
    y = lax.conv_general_dilated(x_ext, w[:, None, :].astype(x_ext.dtype), window_strides=(1,),
                                 padding='VALID', dimension_numbers=('NWC', 'WIO', 'NWC'),
                                 feature_group_count=ch)
    return y + b


def ssd_scan(x, dt, a, b_mat, c_mat, h0):
    bsz, seq, heads, pdim = x.shape
    groups, nstate = b_mat.shape[2], b_mat.shape[3]
    rep = heads // groups
    blk = min(CHUNK, seq)
    nc = seq // blk
    f32 = jnp.float32
    xr = x.astype(f32).reshape(bsz, nc, blk, groups, rep, pdim)
    dtr = dt.reshape(bsz, nc, blk, groups, rep)
    br = b_mat.astype(f32).reshape(bsz, nc, blk, groups, nstate)
    cr = c_mat.astype(f32).reshape(bsz, nc, blk, groups, nstate)
    acum = jnp.cumsum(dtr * a.reshape(groups, rep), axis=2)
    tri = jnp.tril(jnp.ones((blk, blk), bool))[None, None, :, :, None, None]
    seg = acum[:, :, :, None] - acum[:, :, None, :]
    decay = jnp.where(tri, jnp.exp(jnp.where(tri, seg, 0.0)), 0.0)
    cb = jnp.einsum('bcqgn,bckgn->bcqkg', cr, br)
    wmat = cb[..., None] * decay * dtr[:, :, None]
    y_diag = jnp.einsum('bcqkgr,bckgrp->bcqgrp', wmat, xr)
    xw = xr * (jnp.exp(acum[:, :, -1:] - acum) * dtr)[..., None]
    chunk_states = jnp.einsum('bcqgn,bcqgrp->bcgrpn', br, xw)
    chunk_decay = jnp.exp(acum[:, :, -1])

    def step(h, inp):
        s_c, d_c = inp
        return h * d_c[..., None, None] + s_c, h

    h_init = h0.astype(f32).reshape(bsz, groups, rep, pdim, nstate)
    h_last, h_prev = lax.scan(step, h_init, (jnp.moveaxis(chunk_states, 1, 0),
                                             jnp.moveaxis(chunk_decay, 1, 0)))
    h_prev = jnp.moveaxis(h_prev, 0, 1)
    y_off = jnp.einsum('bcqgn,bcgrpn->bcqgrp', cr, h_prev) * jnp.exp(acum)[..., None]
    y = (y_diag + y_off).reshape(bsz, seq, heads, pdim)
    return y, h_last.reshape(bsz, heads, pdim, nstate)


def dsa_attention(q, iq, iw, k_all, v_all, ik_all, rel_bias, start):
    bsz, seq = q.shape[0], q.shape[1]
    n_keys = k_all.shape[1]
    topk = min(TOPK_MAX, n_keys // 4)
    blk = min(Q_BLOCK, seq)
    nb = seq // blk
    f32 = jnp.float32
    k_chunk = jnp.arange(n_keys, dtype=jnp.int32) // CHUNK
    kf = k_all.astype(f32)
    vf = v_all.astype(f32)
    ikf = ik_all.astype(f32)
    bias_tab = rel_bias.astype(f32)

    def attend_block(args):
        qb, iqb, iwb, pb = args
        q_chunk = pb // CHUNK
        visible = k_chunk[None, :] <= q_chunk[:, None]
        s = jnp.einsum('bqhd,bsd->bqhs', iqb.astype(f32), ikf) * IDX_DIM ** -0.5
        index = jnp.einsum('bqh,bqhs->bqs', iwb.astype(f32), jax.nn.relu(s))
        index = jnp.where(visible[None], index, NEG)
        _, sel = lax.top_k(index, topk)
        sel_valid = (sel // CHUNK) <= q_chunk[None, :, None]
        k_sel = jax.vmap(lambda kk, ii: kk[ii])(kf, sel)
        v_sel = jax.vmap(lambda vv, ii: vv[ii])(vf, sel)
        bias = bias_tab[t5_bucket(sel - pb[None, :, None])]
        bias = jnp.transpose(bias.reshape(bsz, blk, topk, KV_HEADS, Q_PER_KV), (0, 1, 3, 4, 2))
        qg = qb.astype(f32).reshape(bsz, blk, KV_HEADS, Q_PER_KV, HEAD_DIM)
        logits = jnp.einsum('bqgrd,bqkgd->bqgrk', qg, k_sel) * HEAD_DIM ** -0.5 + bias
        logits = jnp.where(sel_valid[:, :, None, None, :], logits, NEG)
        probs = jax.nn.softmax(logits, axis=-1)
        o = jnp.einsum('bqgrk,bqkgd->bqgrd', probs, v_sel)
        return o.reshape(bsz, blk, ATTN_HEADS * HEAD_DIM)

    def to_blocks(t):
        return jnp.swapaxes(t.reshape((bsz, nb, blk) + t.shape[2:]), 0, 1)

    q_pos = (start + jnp.arange(seq, dtype=jnp.int32)).reshape(nb, blk)
    out = lax.map(attend_block, (to_blocks(q), to_blocks(iq), to_blocks(iw), q_pos))
    return jnp.swapaxes(out, 0, 1).reshape(bsz, seq, ATTN_HEADS * HEAD_DIM).astype(q.dtype)


def pool_mix(u_ext, start, pool_w, pool_b, pool_scale):
    bsz = u_ext.shape[0]
    seq = u_ext.shape[1] - POOL_STATE
    uf = u_ext.astype(jnp.float32)
    cs = jnp.concatenate([jnp.zeros((bsz, 1, BRANCH_DIM), jnp.float32), jnp.cumsum(uf, axis=1)], axis=1)
    pos = start + jnp.arange(seq, dtype=jnp.int32)
    ends = cs[:, POOL_STATE + 1:POOL_STATE + 1 + seq]
    cur = uf[:, POOL_STATE:]
    outs = []
    for gi, w in enumerate(POOL_WINDOWS):
        lo, hi = gi * POOL_GROUP_DIM, (gi + 1) * POOL_GROUP_DIM
        win_sum = ends[..., lo:hi] - cs[:, POOL_STATE + 1 - w:POOL_STATE + 1 - w + seq, lo:hi]
        cnt = jnp.minimum(w, pos + 1).astype(jnp.float32)[None, :, None]
        outs.append(win_sum / cnt - cur[..., lo:hi])
    pooled = jnp.stack(outs, axis=2)
    mixed = jnp.einsum('blgc,gcd->blgd', pooled, pool_w.astype(jnp.float32)) + pool_b.astype(jnp.float32)
    return mixed.reshape(bsz, seq, BRANCH_DIM) * pool_scale.astype(jnp.float32)


def trunk_layer(x, c, k_past, v_past, ik_past, h0, conv_hist, pool_hist, rel_bias,
                w_ada, b_ada, norm_w, w_in, conv_w, conv_b, dt_bias, a_log, d_skip,
                ssd_norm_w, q_norm_w, k_norm_w, pool_w, pool_b, pool_scale, w_branch, w_out):
    bsz, seq, _ = x.shape
    start = k_past.shape[1]
    f32 = jnp.float32
    mod = jax.nn.silu(c) @ w_ada + b_ada
    shift, scale, gate = jnp.split(mod[:, None, :], 3, axis=-1)
    h = rms_norm(x, norm_w) * (1.0 + scale) + shift
    (z, xbc, dt_raw, q, k, v, attn_gate, iq, ik, iw, u, pool_gate, merge_gate) = split_columns(h @ w_in)

    xbc_ext = jnp.concatenate([conv_hist.astype(xbc.dtype), xbc], axis=1)
    xbc_c = jax.nn.silu(causal_dwconv(xbc_ext, conv_w, conv_b))
    nbc = SSD_GROUPS * SSD_STATE
    xs = xbc_c[..., :SSD_INNER].reshape(bsz, seq, SSD_HEADS, SSD_HEAD_DIM)
    bm = xbc_c[..., SSD_INNER:SSD_INNER + nbc].reshape(bsz, seq, SSD_GROUPS, SSD_STATE)
    cm = xbc_c[..., SSD_INNER + nbc:].reshape(bsz, seq, SSD_GROUPS, SSD_STATE)
    dt = jax.nn.softplus(dt_raw.astype(f32) + dt_bias.astype(f32))
    a = -jnp.exp(a_log.astype(f32))
    y_ssd, h_last = ssd_scan(xs, dt, a, bm, cm, h0)
    y_ssd = y_ssd + d_skip.astype(f32)[:, None] * xs.astype(f32)
    y_ssd = rms_norm(y_ssd.reshape(bsz, seq, SSD_INNER) * jax.nn.silu(z.astype(f32)), ssd_norm_w).astype(x.dtype)

    q = rms_norm(q.reshape(bsz, seq, ATTN_HEADS, HEAD_DIM), q_norm_w)
    k = rms_norm(k.reshape(bsz, seq, KV_HEADS, HEAD_DIM), k_norm_w)
    v = v.reshape(bsz, seq, KV_HEADS, HEAD_DIM)
    iq = iq.reshape(bsz, seq, IDX_HEADS, IDX_DIM)
    iw = iw * IDX_HEADS ** -0.5
    k_all = jnp.concatenate([k_past.astype(k.dtype), k], axis=1)
    v_all = jnp.concatenate([v_past.astype(v.dtype), v], axis=1)
    ik_all = jnp.concatenate([ik_past.astype(ik.dtype), ik], axis=1)
    y_att = dsa_attention(q, iq, iw, k_all, v_all, ik_all, rel_bias, start) * jax.nn.silu(attn_gate)

    u_ext = jnp.concatenate([pool_hist.astype(u.dtype), u], axis=1)
    y_pool = (pool_mix(u_ext, start, pool_w, pool_b, pool_scale) * jax.nn.silu(pool_gate.astype(f32))).astype(x.dtype)

    gates = jax.nn.sigmoid(merge_gate.reshape(bsz, seq, N_BRANCH, D_MODEL))
    merged = gates[:, :, 0] * (y_ssd @ w_branch[0])
    merged = merged + gates[:, :, 1] * (y_att @ w_branch[1])
    merged = merged + gates[:, :, 2] * (y_pool @ w_branch[2])
    x_new = x + gate * (merged @ w_out)
    return (x_new, k, v, ik, h_last.astype(x.dtype),
            xbc_ext[:, -(CONV_WIDTH - 1):], u_ext[:, -POOL_STATE:])


def setup_inputs(seed: int = 0) -> dict:
    key = jax.random.key(seed)
    ks = jax.random.split(key, 32)
    f32 = jnp.float32

    def nrm(k, shape, s):
        return jax.random.normal(k, shape, f32) * s

    dt0 = jnp.exp(jax.random.uniform(ks[20], (DEPTH, SSD_HEADS), f32, math.log(1e-3), math.log(1e-1)))
    return {
        'x_prompt': nrm(ks[0], (BATCH, SEQ, D_MODEL), 1.0),
        'x_sample': nrm(ks[1], (DEC_BATCH, DEC_SEQ, D_MODEL), 1.0),
        'cache_k': nrm(ks[2], (DEPTH, DEC_BATCH, PAST_LEN, KV_HEADS, HEAD_DIM), 1.0),
        'cache_v': nrm(ks[3], (DEPTH, DEC_BATCH, PAST_LEN, KV_HEADS, HEAD_DIM), 1.0),
        'cache_idx_k': nrm(ks[4], (DEPTH, DEC_BATCH, PAST_LEN, IDX_DIM), 1.0),
        'state_ssm': nrm(ks[5], (DEPTH, DEC_BATCH, SSD_HEADS, SSD_HEAD_DIM, SSD_STATE), 0.1),
        'state_conv': nrm(ks[6], (DEPTH, DEC_BATCH, CONV_WIDTH - 1, CONV_DIM), 1.0),
        'state_pool': nrm(ks[7], (DEPTH, DEC_BATCH, POOL_STATE, BRANCH_DIM), 1.0),
        'c_prompt': nrm(ks[8], (BATCH, D_MODEL), 1.0),
        'c_sample': nrm(ks[9], (DEC_BATCH, D_MODEL), 1.0),
        'rel_bias': nrm(ks[10], (N_BUCKETS, ATTN_HEADS), 0.1),
        'w_ada': nrm(ks[11], (DEPTH, D_MODEL, 3 * D_MODEL), 0.5 * D_MODEL ** -0.5),
        'b_ada': nrm(ks[12], (DEPTH, 3 * D_MODEL), 0.1),
        'norm_w': 1.0 + nrm(ks[13], (DEPTH, D_MODEL), 0.02),
        'w_in': nrm(ks[14], (DEPTH, D_MODEL, IN_DIM), D_MODEL ** -0.5),
        'conv_w': nrm(ks[15], (DEPTH, CONV_WIDTH, CONV_DIM), CONV_WIDTH ** -0.5),
        'conv_b': nrm(ks[16], (DEPTH, CONV_DIM), 0.02),
        'dt_bias': jnp.log(jnp.expm1(dt0)),
        'a_log': jnp.log(jax.random.uniform(ks[17], (DEPTH, SSD_HEADS), f32, 1.0, 16.0)),
        'd_skip': 1.0 + nrm(ks[18], (DEPTH, SSD_HEADS), 0.02),
        'ssd_norm_w': 1.0 + nrm(ks[19], (DEPTH, SSD_INNER), 0.02),
        'q_norm_w': 1.0 + nrm(ks[21], (DEPTH, HEAD_DIM), 0.02),
        'k_norm_w': 1.0 + nrm(ks[22], (DEPTH, HEAD_DIM), 0.02),
        'pool_w': nrm(ks[23], (DEPTH, POOL_GROUPS, POOL_GROUP_DIM, POOL_GROUP_DIM), POOL_GROUP_DIM ** -0.5),
        'pool_b': nrm(ks[24], (DEPTH, POOL_GROUPS, POOL_GROUP_DIM), 0.02),
        'pool_scale': 1.0 + nrm(ks[25], (DEPTH, BRANCH_DIM), 0.02),
        'w_branch': nrm(ks[26], (DEPTH, N_BRANCH, BRANCH_DIM, D_MODEL), BRANCH_DIM ** -0.5),
        'w_out': nrm(ks[27], (DEPTH, D_MODEL, D_MODEL), D_MODEL ** -0.5),
    }


def reference(x_prompt, x_sample, cache_k, cache_v, cache_idx_k, state_ssm, state_conv, state_pool,
              c_prompt, c_sample, rel_bias, w_ada, b_ada, norm_w, w_in, conv_w, conv_b, dt_bias,
              a_log, d_skip, ssd_norm_w, q_norm_w, k_norm_w, pool_w, pool_b, pool_scale,
              w_branch, w_out):
    bp, dtp = x_prompt.shape[0], x_prompt.dtype
    empty_kv = jnp.zeros((bp, 0, KV_HEADS, HEAD_DIM), dtp)
    empty_ik = jnp.zeros((bp, 0, IDX_DIM), dtp)
    zero_ssm = jnp.zeros((bp, SSD_HEADS, SSD_HEAD_DIM, SSD_STATE), dtp)
    zero_conv = jnp.zeros((bp, CONV_WIDTH - 1, CONV_DIM), dtp)
    zero_pool = jnp.zeros((bp, POOL_STATE, BRANCH_DIM), dtp)

    xp, xs = x_prompt, x_sample
    kp, vp, ikp, sp, cvp, plp = [], [], [], [], [], []
    ksm, vsm, iks, ss, cvs, pls = [], [], [], [], [], []
    for l in range(DEPTH):
        lw = (rel_bias, w_ada[l], b_ada[l], norm_w[l], w_in[l], conv_w[l], conv_b[l], dt_bias[l],
              a_log[l], d_skip[l], ssd_norm_w[l], q_norm_w[l], k_norm_w[l], pool_w[l], pool_b[l],
              pool_scale[l], w_branch[l], w_out[l])
        xp, k1, v1, ik1, s1, cv1, pl1 = trunk_layer(xp, c_prompt, empty_kv, empty_kv, empty_ik,
                                                    zero_ssm, zero_conv, zero_pool, *lw)
        xs, k2, v2, ik2, s2, cv2, pl2 = trunk_layer(xs, c_sample, cache_k[l], cache_v[l], cache_idx_k[l],
                                                    state_ssm[l], state_conv[l], state_pool[l], *lw)
        kp.append(k1); vp.append(v1); ikp.append(ik1); sp.append(s1); cvp.append(cv1); plp.append(pl1)
        ksm.append(k2); vsm.append(v2); iks.append(ik2); ss.append(s2); cvs.append(cv2); pls.append(pl2)

    return (xp, xs,
            jnp.stack(kp), jnp.stack(vp), jnp.stack(ikp), jnp.stack(sp), jnp.stack(cvp), jnp.stack(plp),
            jnp.stack(ksm), jnp.stack(vsm), jnp.stack(iks), jnp.stack(ss), jnp.stack(cvs), jnp.stack(pls))
```

```python
import functools
import math

import numpy as np
import jax
import jax.numpy as jnp
from jax import lax
from jax.experimental import pallas as pl
from jax.experimental.pallas import tpu as pltpu

F32 = jnp.float32
BF16 = jnp.bfloat16
I32 = jnp.int32

D_MODEL = 2048
DEPTH = 4
CHUNK = 64
N_BRANCH = 3
BRANCH_DIM = 1024
SSD_INNER = BRANCH_DIM
SSD_HEAD_DIM = 64
SSD_HEADS = SSD_INNER // SSD_HEAD_DIM
SSD_GROUPS = 2
SSD_STATE = 128
CONV_WIDTH = 4
CONV_DIM = SSD_INNER + 2 * SSD_GROUPS * SSD_STATE
ATTN_HEADS = 16
KV_HEADS = 4
HEAD_DIM = BRANCH_DIM // ATTN_HEADS
Q_PER_KV = ATTN_HEADS // KV_HEADS
IDX_HEADS = 8
IDX_DIM = 64
TOPK_MAX = 256
N_BUCKETS = 32
MAX_DISTANCE = 128
POOL_WINDOWS = (2, 4, 8, 16)
POOL_GROUPS = 4
POOL_GROUP_DIM = BRANCH_DIM // POOL_GROUPS
POOL_STATE = 15
EPS = 1e-6
NEG = -1e30

LANES = 128
VMEM_LIMIT = 56 * 1024 * 1024

_ORIG = {}
_off = 0
for _name, _size in (("z", SSD_INNER), ("xbc", CONV_DIM), ("dt", SSD_HEADS), ("q", BRANCH_DIM),
                     ("k", KV_HEADS * HEAD_DIM), ("v", KV_HEADS * HEAD_DIM), ("ag", BRANCH_DIM),
                     ("iq", IDX_HEADS * IDX_DIM), ("ik", IDX_DIM), ("iw", IDX_HEADS),
                     ("u", BRANCH_DIM), ("pg", BRANCH_DIM), ("mg", N_BRANCH * D_MODEL)):
    _ORIG[_name] = (_off, _size)
    _off += _size
IN_DIM = _off

_NEW_ORDER = ("mg", "z", "q", "ag", "u", "pg", "xbc", "iq", "k", "v", "ik", "dt", "iw")
COL = {}
_off = 0
for _name in _NEW_ORDER:
    COL[_name] = _off
    _off += -(-_ORIG[_name][1] // LANES) * LANES
PROJ_DIM = -(-_off // 2048) * 2048
COL["xs"] = COL["xbc"]
COL["bm"] = COL["xbc"] + SSD_INNER
COL["cm"] = COL["bm"] + SSD_GROUPS * SSD_STATE


def _sortable_const(v):
    i = int(np.float32(v).view(np.int32))
    return i ^ ((i >> 31) & 0x7FFFFFFF)


NEG_KEY = _sortable_const(NEG)
INT_MIN = -2 ** 31


def _cparams(sem):
    return pltpu.CompilerParams(dimension_semantics=sem, vmem_limit_bytes=VMEM_LIMIT)


def _silu(x):
    return x * jax.nn.sigmoid(x)


def _ada_kernel(c_ref, w_ref, b_ref, o_ref):
    c = c_ref[...]
    o_ref[...] = jnp.dot(_silu(c).astype(BF16), w_ref[...].astype(BF16),
                         preferred_element_type=F32) + b_ref[...]


def ada_mod(c_all, w_ada, b_ada):
    nb, d = c_all.shape
    n = w_ada.shape[-1]
    tn = 512
    return pl.pallas_call(
        _ada_kernel,
        grid=(DEPTH, n // tn),
        in_specs=[pl.BlockSpec((nb, d), lambda l, j: (0, 0)),
                  pl.BlockSpec((None, d, tn), lambda l, j: (l, 0, j)),
                  pl.BlockSpec((None, 1, tn), lambda l, j: (l, 0, j))],
        out_specs=pl.BlockSpec((None, nb, tn), lambda l, j: (l, 0, j)),
        out_shape=jax.ShapeDtypeStruct((DEPTH, nb, n), F32),
        compiler_params=_cparams(("arbitrary", "arbitrary")),
        name="ada",
    )(c_all, w_ada, b_ada.reshape(DEPTH, 1, n))


def _inproj_kernel(x_ref, sc_ref, sh_ref, nw_ref, w_ref, o_ref, h_ref):
    @pl.when(pl.program_id(1) == 0)
    def _():
        x = x_ref[...]
        ms = jnp.mean(x * x, axis=-1, keepdims=True)
        y = x * lax.rsqrt(ms + EPS) * nw_ref[...]
        h_ref[...] = (y * (1.0 + sc_ref[...]) + sh_ref[...]).astype(BF16)

    o_ref[...] = jnp.dot(h_ref[...], w_ref[...], preferred_element_type=F32)


def _mod_spec(mod, tm, tn, rows_per_mod, col_of_j):
    r = mod.shape[1]
    if r == 1:
        tiles = rows_per_mod // tm
        return pl.BlockSpec((None, 1, tn), lambda i, j: (i // tiles, 0, col_of_j(j)))
    return pl.BlockSpec((None, r, tn), lambda i, j: (i, 0, col_of_j(j)))


def inproj(x, scale, shift, norm_w, w, tm, rows_per_mod):
    t, d = x.shape
    n = w.shape[1]
    tn = 1024
    zero = lambda j: 0
    return pl.pallas_call(
        _inproj_kernel,
        grid=(t // tm, n // tn),
        in_specs=[pl.BlockSpec((tm, d), lambda i, j: (i, 0)),
                  _mod_spec(scale, tm, d, rows_per_mod, zero),
                  _mod_spec(shift, tm, d, rows_per_mod, zero),
                  pl.BlockSpec((1, d), lambda i, j: (0, 0)),
                  pl.BlockSpec((d, tn), lambda i, j: (0, j))],
        out_specs=pl.BlockSpec((tm, tn), lambda i, j: (i, j)),
        out_shape=jax.ShapeDtypeStruct((t, n), F32),
        scratch_shapes=[pltpu.VMEM((tm, d), BF16)],
        compiler_params=_cparams(("arbitrary", "arbitrary")),
        name="inproj",
    )(x, scale, shift, norm_w.reshape(1, d), w)


def _prep_kernel(q_ref, k_ref, iq_ref, qw_ref, kw_ref, qh_ref, ko_ref, iqh_ref):
    def head_norm(xs, w):
        ms = jnp.mean(xs * xs, axis=-1, keepdims=True)
        return xs * lax.rsqrt(ms + EPS) * w

    q = q_ref[...]
    for h in range(ATTN_HEADS):
        qn = head_norm(q[:, h * HEAD_DIM:(h + 1) * HEAD_DIM], qw_ref[...])
        qh_ref[h] = (qn * HEAD_DIM ** -0.5).astype(BF16)
    k = k_ref[...]
    for g in range(KV_HEADS):
        ko_ref[:, g * HEAD_DIM:(g + 1) * HEAD_DIM] = head_norm(k[:, g * HEAD_DIM:(g + 1) * HEAD_DIM], kw_ref[...])
    iq = iq_ref[...]
    for h in range(IDX_HEADS):
        iqh_ref[h] = (iq[:, h * IDX_DIM:(h + 1) * IDX_DIM] * IDX_DIM ** -0.5).astype(BF16)


def prep(proj, q_norm_w, k_norm_w, tm):
    t = proj.shape[0]
    kw = KV_HEADS * HEAD_DIM
    iqw = IDX_HEADS * IDX_DIM
    return pl.pallas_call(
        _prep_kernel,
        grid=(t // tm,),
        in_specs=[pl.BlockSpec((tm, BRANCH_DIM), lambda i: (i, COL["q"] // BRANCH_DIM)),
                  pl.BlockSpec((tm, kw), lambda i: (i, COL["k"] // kw)),
                  pl.BlockSpec((tm, iqw), lambda i: (i, COL["iq"] // iqw)),
                  pl.BlockSpec((1, HEAD_DIM), lambda i: (0, 0)),
                  pl.BlockSpec((1, HEAD_DIM), lambda i: (0, 0))],
        out_specs=[pl.BlockSpec((ATTN_HEADS, tm, HEAD_DIM), lambda i: (0, i, 0)),
                   pl.BlockSpec((tm, kw), lambda i: (i, 0)),
                   pl.BlockSpec((IDX_HEADS, tm, IDX_DIM), lambda i: (0, i, 0))],
        out_shape=[jax.ShapeDtypeStruct((ATTN_HEADS, t, HEAD_DIM), BF16),
                   jax.ShapeDtypeStruct((t, kw), F32),
                   jax.ShapeDtypeStruct((IDX_HEADS, t, IDX_DIM), BF16)],
        compiler_params=_cparams(("arbitrary",)),
        name="prep",
    )(proj, proj, proj, q_norm_w.reshape(1, HEAD_DIM), k_norm_w.reshape(1, HEAD_DIM))


TKS = 256
NEAR_W = 2 * TKS
FAR_BUCKET = N_BUCKETS // 2 - 1
J_ALL = 2 ** 30


def _dsa_kernel(tab_ref, qh_ref, iqh_ref, iw_ref, ikT_ref, kT_ref, v_ref, bk_ref, ag_ref,
                o_ref, keys_ref, nb_ref, acc_ref, m_ref, l_ref, j_ref, *,
                tq, ta, tk, start, n_keys, topk):
    b = pl.program_id(0)
    i = pl.program_id(1)
    kk = pl.program_id(2)
    nk = pl.num_programs(2)
    q0 = start + i * tq
    vis_end = ((q0 + tq - 1) // CHUNK + 1) * CHUNK
    n_cols = jnp.minimum(vis_end, n_keys)
    n_a = (n_cols + ta - 1) // ta
    n_sub = (n_cols + TKS - 1) // TKS
    extra = jnp.maximum(n_keys - n_a * ta, 0).astype(F32)
    sub_per_tile = tk // TKS

    def vis_mask(t, width):
        kpos = t * width + lax.broadcasted_iota(I32, (tq, width), 1)
        qpos = q0 + lax.broadcasted_iota(I32, (tq, width), 0)
        vis = (kpos // CHUNK) <= (qpos // CHUNK)
        return vis, kpos

    @pl.when((b == 0) & (i == 0) & (kk == 0))
    def _():
        bk = bk_ref[...]

        def per_head(h, _):
            def per_bucket(n, val):
                return jnp.where(bk == n, tab_ref[n, h], val)
            val = lax.fori_loop(0, N_BUCKETS, per_bucket, jnp.zeros((tq, NEAR_W), F32))
            nb_ref[h] = val - tab_ref[FAR_BUCKET, h]
            return 0
        lax.fori_loop(0, ATTN_HEADS, per_head, 0)

    @pl.when(kk == 0)
    def _():
        iw = iw_ref[...] * IDX_HEADS ** -0.5

        def score_tile(t, _):
            c0 = pl.multiple_of(t * ta, ta)
            ikt = ikT_ref[:, pl.ds(c0, ta)]
            acc = jnp.zeros((tq, ta), F32)
            for h in range(IDX_HEADS):
                s = jnp.dot(iqh_ref[h], ikt, preferred_element_type=F32)
                acc = acc + iw[:, h:h + 1] * jnp.maximum(s, 0.0)
            vis, kpos = vis_mask(t, ta)
            acc = jnp.where(vis, acc, NEG)
            bits = pltpu.bitcast(acc, I32)
            key = bits ^ ((bits >> 31) & 0x7FFFFFFF)
            key = jnp.where(kpos < n_keys, key, INT_MIN)
            keys_ref[:, pl.ds(c0, ta)] = key
            return 0
        lax.fori_loop(0, n_a, score_tile, 0)

        def count(pred):
            def body(t, cnt):
                c0 = pl.multiple_of(t * ta, ta)
                blk = keys_ref[:, pl.ds(c0, ta)]
                c = jnp.where(pred(blk, t), 1.0, 0.0)
                for jj in range(ta // LANES):
                    cnt = cnt + c[:, jj * LANES:(jj + 1) * LANES]
                return cnt
            cnt = lax.fori_loop(0, n_a, body, jnp.zeros((tq, LANES), F32))
            return jnp.sum(cnt, axis=1, keepdims=True)

        def bit_step(it, prefix_u):
            bit = jnp.left_shift(jnp.int32(1), 31 - it)
            cand_u = prefix_u | bit
            cand_s = cand_u ^ INT_MIN
            cnt = count(lambda blk, t: blk >= cand_s) + jnp.where(NEG_KEY >= cand_s, extra, 0.0)
            return jnp.where(cnt >= topk, cand_u, prefix_u)
        prefix_u = lax.fori_loop(0, 32, bit_step, jnp.zeros((tq, 1), I32))
        thr = prefix_u ^ INT_MIN

        cnt_gt = count(lambda blk, t: blk > thr) + jnp.where(NEG_KEY > thr, extra, 0.0)
        cnt_eq = count(lambda blk, t: blk == thr)
        need = topk - cnt_gt
        j_ref[...] = jnp.full((tq, LANES), J_ALL, I32)

        @pl.when(jnp.max(cnt_eq - need) > 0.0)
        def _():
            def idx_step(it, jmax):
                cand = jmax | jnp.left_shift(jnp.int32(1), 14 - it)

                def pred(blk, t):
                    _, kpos = vis_mask(t, ta)
                    return (blk == thr) & (kpos < cand)
                g = count(pred)
                return jnp.where(g < need, cand, jmax)
            jmax = lax.fori_loop(0, 15, idx_step, jnp.zeros((tq, 1), I32))
            j_ref[...] = jnp.broadcast_to(jmax, (tq, LANES))

        jmax = j_ref[:, 0:1]

        def mask_tile(t, _):
            c0 = pl.multiple_of(t * ta, ta)
            blk = keys_ref[:, pl.ds(c0, ta)]
            vis, kpos = vis_mask(t, ta)
            sel = (blk > thr) | ((blk == thr) & (kpos <= jmax))
            ok = sel & vis & (kpos < n_keys)
            madd = jnp.where(ok, 0.0, NEG).astype(F32)
            keys_ref[:, pl.ds(c0, ta)] = pltpu.bitcast(madd, I32)
            return 0
        lax.fori_loop(0, n_a, mask_tile, 0)

        m_ref[...] = jnp.full(m_ref.shape, -jnp.inf, F32)
        l_ref[...] = jnp.zeros(l_ref.shape, F32)
        acc_ref[...] = jnp.zeros(acc_ref.shape, F32)

    def attend(jl, boff):
        lo = pl.multiple_of(jl * TKS, TKS)
        c0 = pl.multiple_of(kk * tk + lo, TKS)
        madd = pltpu.bitcast(keys_ref[:, pl.ds(c0, TKS)], F32)

        def group(g, _):
            kt = kT_ref[g, :, pl.ds(lo, TKS)]
            vv = v_ref[g, pl.ds(lo, TKS), :]
            for r in range(Q_PER_KV):
                h = g * Q_PER_KV + r
                s = jnp.dot(qh_ref[h], kt, preferred_element_type=F32) + madd
                if boff is not None:
                    s = s + nb_ref[h, :, boff:boff + TKS]
                m_prev = m_ref[h]
                m_new = jnp.maximum(m_prev, jnp.max(s, axis=1, keepdims=True))
                alpha = jnp.exp(m_prev - m_new)
                p = jnp.exp(s - m_new[:, 0:1])
                l_ref[h] = alpha * l_ref[h] + jnp.sum(p, axis=1, keepdims=True)
                acc_ref[h] = alpha[:, 0:HEAD_DIM] * acc_ref[h] + jnp.dot(
                    p.astype(BF16), vv, preferred_element_type=F32)
                m_ref[h] = m_new
            return 0
        lax.fori_loop(0, KV_HEADS, group, 0)

    n_here = jnp.clip(n_sub - kk * sub_per_tile, 0, sub_per_tile)

    def sub_body(jl, _):
        u = kk * sub_per_tile + jl

        @pl.when(u < n_sub - 2)
        def _():
            attend(jl, None)

        @pl.when(u == n_sub - 2)
        def _():
            attend(jl, 0)

        @pl.when(u == n_sub - 1)
        def _():
            attend(jl, TKS)
        return 0
    lax.fori_loop(0, n_here, sub_body, 0)

    @pl.when(kk == nk - 1)
    def _():
        ag = ag_ref[...]
        for h in range(ATTN_HEADS):
            o = acc_ref[h] / l_ref[h][:, 0:HEAD_DIM]
            sl = slice(h * HEAD_DIM, (h + 1) * HEAD_DIM)
            o_ref[:, sl] = (o * _silu(ag[:, sl])).astype(o_ref.dtype)


def dsa(qh, iqh, proj, ikT, kT, v4, bk, rel_bias, *, nbatch, tq, ta, tk, start, n_keys, topk):
    t = qh.shape[1]
    s_pad = ikT.shape[-1]
    nq = t // (nbatch * tq)
    nk = s_pad // tk

    def kt_idx(b, i, kk):
        q0 = start + i * tq
        n_cols = jnp.minimum(((q0 + tq - 1) // CHUNK + 1) * CHUNK, n_keys)
        return jnp.minimum(kk, (n_cols + tk - 1) // tk - 1)

    kern = functools.partial(_dsa_kernel, tq=tq, ta=ta, tk=tk, start=start, n_keys=n_keys, topk=topk)
    return pl.pallas_call(
        kern,
        grid=(nbatch, nq, nk),
        in_specs=[pl.BlockSpec(memory_space=pltpu.SMEM),
                  pl.BlockSpec((ATTN_HEADS, tq, HEAD_DIM), lambda b, i, kk: (0, b * nq + i, 0)),
                  pl.BlockSpec((IDX_HEADS, tq, IDX_DIM), lambda b, i, kk: (0, b * nq + i, 0)),
                  pl.BlockSpec((tq, LANES), lambda b, i, kk: (b * nq + i, COL["iw"] // LANES)),
                  pl.BlockSpec((None, IDX_DIM, s_pad), lambda b, i, kk: (b, 0, 0)),
                  pl.BlockSpec((None, KV_HEADS, HEAD_DIM, tk), lambda b, i, kk: (b, 0, 0, kt_idx(b, i, kk))),
                  pl.BlockSpec((None, KV_HEADS, tk, HEAD_DIM), lambda b, i, kk: (b, 0, kt_idx(b, i, kk), 0)),
                  pl.BlockSpec((tq, NEAR_W), lambda b, i, kk: (0, 0)),
                  pl.BlockSpec((tq, BRANCH_DIM), lambda b, i, kk: (b * nq + i, COL["ag"] // BRANCH_DIM))],
        out_specs=pl.BlockSpec((tq, BRANCH_DIM), lambda b, i, kk: (b * nq + i, 0)),
        out_shape=jax.ShapeDtypeStruct((t, BRANCH_DIM), BF16),
        scratch_shapes=[pltpu.VMEM((tq, s_pad), I32),
                        pltpu.VMEM((ATTN_HEADS, tq, NEAR_W), F32),
                        pltpu.VMEM((ATTN_HEADS, tq, HEAD_DIM), F32),
                        pltpu.VMEM((ATTN_HEADS, tq, LANES), F32),
                        pltpu.VMEM((ATTN_HEADS, tq, LANES), F32),
                        pltpu.VMEM((tq, LANES), I32)],
        compiler_params=_cparams(("arbitrary", "arbitrary", "arbitrary")),
        name="dsa",
    )(rel_bias, qh, iqh, proj, ikT, kT, v4, bk, proj)


def t5_bucket(rel):
    half = N_BUCKETS // 2
    max_exact = half // 2
    ret = jnp.where(rel > 0, half, 0)
    n = jnp.abs(rel)
    nf = jnp.maximum(n, max_exact).astype(jnp.float32)
    large = max_exact + (jnp.log(nf / max_exact) / math.log(MAX_DISTANCE / max_exact)
                         * (half - max_exact)).astype(jnp.int32)
    large = jnp.minimum(large, half - 1)
    return ret + jnp.where(n < max_exact, n, large)


def near_buckets(tq, q0, n_keys):
    vis_end = ((q0 + tq - 1) // CHUNK + 1) * CHUNK
    end = -(-min(vis_end, n_keys) // TKS) * TKS
    assert end - NEAR_W <= q0 - MAX_DISTANCE + 1, "near window must cover every non-saturated offset"
    kpos = end - NEAR_W + jnp.arange(NEAR_W, dtype=jnp.int32)[None, :]
    qpos = q0 + jnp.arange(tq, dtype=jnp.int32)[:, None]
    return t5_bucket(kpos - qpos)


def _ssd_kernel(xs_ref, bm_ref, cm_ref, dt_ref, z_ref, hist_ref, st0_ref, cw_ref, cb_ref,
                dtb_ref, alog_ref, dsk_ref, nw_ref, exp_ref, y_ref, st_ref,
                ext_ref, state_ref, *, q, valid):
    c = pl.program_id(1)
    nc = pl.num_programs(1)
    hp = SSD_INNER
    gw = hp // SSD_GROUPS
    hi = lax.Precision.HIGHEST

    @pl.when(c == 0)
    def _():
        ext_ref[0:8, :] = hist_ref[...]
        state_ref[...] = st0_ref[...]

    ext_ref[8:8 + q, 0:hp] = xs_ref[...]
    ext_ref[8:8 + q, hp:hp + 256] = bm_ref[...]
    ext_ref[8:8 + q, hp + 256:hp + 512] = cm_ref[...]
    conv = jnp.zeros((q, CONV_DIM), F32) + cb_ref[...]
    for j in range(CONV_WIDTH):
        conv = conv + ext_ref[5 + j:5 + j + q, :] * cw_ref[j:j + 1, :]
    ext_ref[0:8, :] = ext_ref[q:q + 8, :]
    conv = _silu(conv)
    xs = conv[:, 0:hp]
    bmat = conv[:, hp:hp + 256]
    cmat = conv[:, hp + 256:hp + 512]

    xdt = dt_ref[...] + dtb_ref[...]
    dt = jnp.maximum(xdt, 0.0) + jnp.log1p(jnp.exp(-jnp.abs(xdt)))
    if valid < q:
        rows = lax.broadcasted_iota(I32, (q, LANES), 0)
        dt = jnp.where(rows < valid, dt, 0.0)
    adt = dt * (-jnp.exp(alog_ref[...]))
    rr = lax.broadcasted_iota(I32, (q, q), 0)
    cc = lax.broadcasted_iota(I32, (q, q), 1)
    tri = rr >= cc
    acum = jnp.dot(tri.astype(F32), adt, precision=hi, preferred_element_type=F32)
    acum_t = acum.T
    dt_t = dt.T
    alast = acum[q - 1:q, :]

    expand = exp_ref[...]
    e_acum = jnp.dot(jnp.exp(acum), expand, precision=hi, preferred_element_type=F32)
    e_tail = jnp.dot(jnp.exp(alast - acum) * dt, expand, precision=hi, preferred_element_type=F32)
    e_last = e_acum[q - 1:q, :]

    xw = (xs * e_tail).astype(BF16)
    xb = xs.astype(BF16)
    y_parts = []
    new_state = []
    for g in range(SSD_GROUPS):
        bg = bmat[:, g * SSD_STATE:(g + 1) * SSD_STATE]
        cg = cmat[:, g * SSD_STATE:(g + 1) * SSD_STATE].astype(BF16)
        bg_t = bg.T.astype(BF16)
        cb = jnp.dot(cg, bg_t, preferred_element_type=F32)
        st_g = state_ref[:, g * gw:(g + 1) * gw]
        y_off = jnp.dot(cg, st_g.astype(BF16), preferred_element_type=F32)
        new_state.append(jnp.dot(bg_t, xw[:, g * gw:(g + 1) * gw], preferred_element_type=F32))
        heads = []
        for r in range(SSD_HEADS // SSD_GROUPS):
            h = g * (SSD_HEADS // SSD_GROUPS) + r
            seg = acum[:, h:h + 1] - acum_t[h:h + 1, :]
            decay = jnp.where(tri, jnp.exp(jnp.where(tri, seg, 0.0)), 0.0)
            wmat = (cb * decay * dt_t[h:h + 1, :]).astype(BF16)
            heads.append(jnp.dot(wmat, xb[:, h * SSD_HEAD_DIM:(h + 1) * SSD_HEAD_DIM],
                                 preferred_element_type=F32))
        y_parts.append(jnp.concatenate(heads, axis=1) + y_off * e_acum[:, g * gw:(g + 1) * gw])
    y = jnp.concatenate(y_parts, axis=1)
    state_ref[...] = state_ref[...] * e_last + jnp.concatenate(new_state, axis=1)

    y = (y + dsk_ref[...] * xs) * _silu(z_ref[...])
    ms = jnp.mean(y * y, axis=-1, keepdims=True)
    y_ref[...] = (y * lax.rsqrt(ms + EPS) * nw_ref[...]).astype(y_ref.dtype)

    @pl.when(c == nc - 1)
    def _():
        st_ref[...] = state_ref[...]


def ssd(proj, hist8, state_t, conv_w, conv_b, dt_bias, a_log, d_skip, ssd_norm_w, *, nbatch, q, valid):
    t = proj.shape[0]
    nc = t // (nbatch * q)
    hp = SSD_INNER

    def pad_heads(v, fill):
        return jnp.concatenate([v.astype(F32), jnp.full((LANES - SSD_HEADS,), fill, F32)]).reshape(1, LANES)

    expand = (jnp.arange(LANES)[:, None] == (jnp.arange(hp)[None, :] // SSD_HEAD_DIM)).astype(F32)
    dsk = jnp.repeat(d_skip.astype(F32), SSD_HEAD_DIM).reshape(1, hp)
    kern = functools.partial(_ssd_kernel, q=q, valid=valid)
    const2 = lambda shape: pl.BlockSpec(shape, lambda b, c: (0, 0))
    return pl.pallas_call(
        kern,
        grid=(nbatch, nc),
        in_specs=[pl.BlockSpec((q, hp), lambda b, c: (b * nc + c, COL["xs"] // hp)),
                  pl.BlockSpec((q, 256), lambda b, c: (b * nc + c, COL["bm"] // 256)),
                  pl.BlockSpec((q, 256), lambda b, c: (b * nc + c, COL["cm"] // 256)),
                  pl.BlockSpec((q, LANES), lambda b, c: (b * nc + c, COL["dt"] // LANES)),
                  pl.BlockSpec((q, hp), lambda b, c: (b * nc + c, COL["z"] // hp)),
                  pl.BlockSpec((None, 8, CONV_DIM), lambda b, c: (b, 0, 0)),
                  pl.BlockSpec((None, SSD_STATE, hp), lambda b, c: (b, 0, 0)),
                  const2((CONV_WIDTH, CONV_DIM)), const2((1, CONV_DIM)),
                  const2((1, LANES)), const2((1, LANES)), const2((1, hp)), const2((1, hp)),
                  const2((LANES, hp))],
        out_specs=[pl.BlockSpec((q, hp), lambda b, c: (b * nc + c, 0)),
                   pl.BlockSpec((None, SSD_STATE, hp), lambda b, c: (b, 0, 0))],
        out_shape=[jax.ShapeDtypeStruct((t, hp), BF16),
                   jax.ShapeDtypeStruct((nbatch, SSD_STATE, hp), F32)],
        scratch_shapes=[pltpu.VMEM((q + 8, CONV_DIM), F32),
                        pltpu.VMEM((SSD_STATE, hp), F32)],
        compiler_params=_cparams(("arbitrary", "arbitrary")),
        name="ssd",
    )(proj, proj, proj, proj, proj, hist8, state_t, conv_w, conv_b.reshape(1, CONV_DIM),
      pad_heads(dt_bias, 0.0), pad_heads(a_log, 0.0), dsk, ssd_norm_w.reshape(1, hp), expand)


def _pool_kernel(u_ref, pg_ref, hist_ref, w_ref, b_ref, sc_ref, y_ref, ext_ref, *, r, start):
    c = pl.program_id(1)

    @pl.when(c == 0)
    def _():
        ext_ref[0:16, :] = hist_ref[...]

    ext_ref[16:16 + r, :] = u_ref[...]
    pos = start + c * r + lax.broadcasted_iota(I32, (r, 1), 0)
    outs = []
    for gi, w in enumerate(POOL_WINDOWS):
        lo = gi * POOL_GROUP_DIM
        cur = ext_ref[16:16 + r, lo:lo + POOL_GROUP_DIM]
        win = cur
        for s in range(1, w):
            win = win + ext_ref[16 - s:16 - s + r, lo:lo + POOL_GROUP_DIM]
        cnt = jnp.minimum(w, pos + 1).astype(F32)
        pooled = win / cnt - cur
        mixed = jnp.dot(pooled.astype(BF16), w_ref[gi], preferred_element_type=F32)
        outs.append(mixed + b_ref[gi:gi + 1, :])
    ext_ref[0:16, :] = ext_ref[r:r + 16, :]
    mixed = jnp.concatenate(outs, axis=1) * sc_ref[...]
    y_ref[...] = (mixed * _silu(pg_ref[...])).astype(y_ref.dtype)


def pool(proj, hist16, pool_w, pool_b, pool_scale, *, nbatch, r, start):
    t = proj.shape[0]
    nc = t // (nbatch * r)
    d = BRANCH_DIM
    kern = functools.partial(_pool_kernel, r=r, start=start)
    return pl.pallas_call(
        kern,
        grid=(nbatch, nc),
        in_specs=[pl.BlockSpec((r, d), lambda b, c: (b * nc + c, COL["u"] // d)),
                  pl.BlockSpec((r, d), lambda b, c: (b * nc + c, COL["pg"] // d)),
                  pl.BlockSpec((None, 16, d), lambda b, c: (b, 0, 0)),
                  pl.BlockSpec((POOL_GROUPS, POOL_GROUP_DIM, POOL_GROUP_DIM), lambda b, c: (0, 0, 0)),
                  pl.BlockSpec((POOL_GROUPS, POOL_GROUP_DIM), lambda b, c: (0, 0)),
                  pl.BlockSpec((1, d), lambda b, c: (0, 0))],
        out_specs=pl.BlockSpec((r, d), lambda b, c: (b * nc + c, 0)),
        out_shape=jax.ShapeDtypeStruct((t, d), BF16),
        scratch_shapes=[pltpu.VMEM((r + 16, d), F32)],
        compiler_params=_cparams(("arbitrary", "arbitrary")),
        name="pool",
    )(proj, proj, hist16, pool_w.astype(BF16), pool_b, pool_scale.reshape(1, d))


def _merge_kernel(y0_ref, y1_ref, y2_ref, g0_ref, g1_ref, g2_ref, w_ref, o_ref):
    acc = jax.nn.sigmoid(g0_ref[...]) * jnp.dot(y0_ref[...], w_ref[0], preferred_element_type=F32)
    acc = acc + jax.nn.sigmoid(g1_ref[...]) * jnp.dot(y1_ref[...], w_ref[1], preferred_element_type=F32)
    acc = acc + jax.nn.sigmoid(g2_ref[...]) * jnp.dot(y2_ref[...], w_ref[2], preferred_element_type=F32)
    o_ref[...] = acc.astype(o_ref.dtype)


def merge(y_ssd, y_att, y_pool, proj, w_branch, tm):
    t = y_ssd.shape[0]
    tn = 1024
    nj = D_MODEL // tn
    ysp = pl.BlockSpec((tm, BRANCH_DIM), lambda i, j: (i, 0))

    def gate_spec(bi):
        return pl.BlockSpec((tm, tn), lambda i, j: (i, (COL["mg"] + bi * D_MODEL) // tn + j))

    return pl.pallas_call(
        _merge_kernel,
        grid=(t // tm, nj),
        in_specs=[ysp, ysp, ysp, gate_spec(0), gate_spec(1), gate_spec(2),
                  pl.BlockSpec((N_BRANCH, BRANCH_DIM, tn), lambda i, j: (0, 0, j))],
        out_specs=pl.BlockSpec((tm, tn), lambda i, j: (i, j)),
        out_shape=jax.ShapeDtypeStruct((t, D_MODEL), BF16),
        compiler_params=_cparams(("arbitrary", "arbitrary")),
        name="merge",
    )(y_ssd, y_att, y_pool, proj, proj, proj, w_branch)


def _outproj_kernel(m_ref, w_ref, x_ref, g_ref, o_ref):
    o_ref[...] = x_ref[...] + g_ref[...] * jnp.dot(m_ref[...], w_ref[...], preferred_element_type=F32)


def outproj(merged, w_out, x, gate, tm, rows_per_mod):
    t = x.shape[0]
    tn = 1024
    return pl.pallas_call(
        _outproj_kernel,
        grid=(t // tm, D_MODEL // tn),
        in_specs=[pl.BlockSpec((tm, D_MODEL), lambda i, j: (i, 0)),
                  pl.BlockSpec((D_MODEL, tn), lambda i, j: (0, j)),
                  pl.BlockSpec((tm, tn), lambda i, j: (i, j)),
                  _mod_spec(gate, tm, tn, rows_per_mod, lambda j: j)],
        out_specs=pl.BlockSpec((tm, tn), lambda i, j: (i, j)),
        out_shape=jax.ShapeDtypeStruct((t, D_MODEL), F32),
        compiler_params=_cparams(("arbitrary", "arbitrary")),
        name="outproj",
    )(merged, w_out, x, gate)


def _pad_to(a, axis, size):
    pad = [(0, 0)] * a.ndim
    pad[axis] = (0, size - a.shape[axis])
    return jnp.pad(a, pad)


def trunk_layer(x, mod, k_past, v_past, ik_past, h0, conv_hist, pool_hist, rel_bias, lw, *, per_row_mod):
    (norm_w, w_in, conv_w, conv_b, dt_bias, a_log, d_skip, ssd_norm_w, q_norm_w, k_norm_w,
     pool_w, pool_b, pool_scale, w_branch, w_out) = lw
    bsz, seq, d = x.shape
    t = bsz * seq
    start = k_past.shape[1]
    n_keys = start + seq
    topk = min(TOPK_MAX, n_keys // 4)
    x2 = x.reshape(t, d)
    shift, scale, gate = mod[:, :d], mod[:, d:2 * d], mod[:, 2 * d:]
    tm = min(512, t)
    if per_row_mod:
        expand = lambda m: jnp.broadcast_to(m[:, None, :], (bsz, seq, d)).reshape(t // tm, tm, d)
    else:
        expand = lambda m: m[:, None, :]
    scale3, shift3, gate3 = expand(scale), expand(shift), expand(gate)

    proj = inproj(x2, scale3, shift3, norm_w, w_in, tm, seq)
    qh, k_new, iqh = prep(proj, q_norm_w, k_norm_w, tm)

    kw = KV_HEADS * HEAD_DIM
    v_new = proj[:, COL["v"]:COL["v"] + kw]
    ik_new = proj[:, COL["ik"]:COL["ik"] + IDX_DIM]
    k_new4 = k_new.reshape(bsz, seq, KV_HEADS, HEAD_DIM)
    v_new4 = v_new.reshape(bsz, seq, KV_HEADS, HEAD_DIM)
    ik_new3 = ik_new.reshape(bsz, seq, IDX_DIM)

    tq = min(256, seq)
    if seq >= 2048:
        tk, ta = 2048, 512
    else:
        tk = ta = None
    s_pad = -(-n_keys // TKS) * TKS
    if tk is None:
        tk, ta = s_pad, s_pad // 3 if (s_pad // 3) % TKS == 0 else s_pad
    s_pad = -(-s_pad // tk) * tk
    k_all = jnp.concatenate([k_past.astype(F32), k_new4], axis=1).astype(BF16)
    v_all = jnp.concatenate([v_past.astype(F32), v_new4], axis=1).astype(BF16)
    ik_all = jnp.concatenate([ik_past.astype(F32), ik_new3], axis=1).astype(BF16)
    kT = _pad_to(jnp.transpose(k_all, (0, 2, 3, 1)), 3, s_pad)
    v4 = _pad_to(jnp.transpose(v_all, (0, 2, 1, 3)), 2, s_pad)
    ikT = _pad_to(jnp.transpose(ik_all, (0, 2, 1)), 2, s_pad)
    bk = near_buckets(tq, start, n_keys)
    y_att = dsa(qh, iqh, proj, ikT, kT, v4, bk, rel_bias, nbatch=bsz, tq=tq, ta=ta, tk=tk,
                start=start, n_keys=n_keys, topk=topk)

    q = 128
    hist8 = jnp.pad(conv_hist.astype(F32), ((0, 0), (8 - (CONV_WIDTH - 1), 0), (0, 0)))
    state_t = jnp.transpose(h0.astype(F32), (0, 3, 1, 2)).reshape(bsz, SSD_STATE, SSD_INNER)
    if seq < q:
        proj_ssd = _pad_to(proj.reshape(bsz, seq, PROJ_DIM), 1, q).reshape(bsz * q, PROJ_DIM)
        valid = seq
    else:
        proj_ssd, valid = proj, q
    y_ssd, st = ssd(proj_ssd, hist8, state_t, conv_w, conv_b, dt_bias, a_log, d_skip, ssd_norm_w,
                    nbatch=bsz, q=q, valid=valid)
    if seq < q:
        y_ssd = y_ssd.reshape(bsz, q, SSD_INNER)[:, :seq].reshape(t, SSD_INNER)
    h_last = jnp.transpose(st.reshape(bsz, SSD_STATE, SSD_HEADS, SSD_HEAD_DIM), (0, 2, 3, 1))

    hist16 = jnp.pad(pool_hist.astype(F32), ((0, 0), (16 - POOL_STATE, 0), (0, 0)))
    y_pool = pool(proj, hist16, pool_w, pool_b, pool_scale, nbatch=bsz, r=min(512, seq), start=start)

    merged = merge(y_ssd, y_att, y_pool, proj, w_branch, tm)
    x_new = outproj(merged, w_out, x2, gate3, tm, seq).reshape(bsz, seq, d)

    xbc = proj[:, COL["xbc"]:COL["xbc"] + CONV_DIM].reshape(bsz, seq, CONV_DIM)
    u = proj[:, COL["u"]:COL["u"] + BRANCH_DIM].reshape(bsz, seq, BRANCH_DIM)
    conv_new = jnp.concatenate([conv_hist.astype(F32), xbc], axis=1)[:, -(CONV_WIDTH - 1):]
    pool_new = jnp.concatenate([pool_hist.astype(F32), u[:, -min(seq, POOL_STATE):]], axis=1)[:, -POOL_STATE:]
    return x_new, k_new4, v_new4, ik_new3, h_last, conv_new, pool_new


def _reorder_w_in(w_in):
    parts = []
    for name in _NEW_ORDER:
        off, size = _ORIG[name]
        seg = w_in[..., off:off + size]
        padw = -(-size // LANES) * LANES - size
        if padw:
            seg = jnp.pad(seg, ((0, 0), (0, 0), (0, padw)))
        parts.append(seg)
    out = jnp.concatenate(parts, axis=-1)
    return _pad_to(out, 2, PROJ_DIM).astype(BF16)


def kernel(x_prompt, x_sample, cache_k, cache_v, cache_idx_k, state_ssm, state_conv, state_pool,
           c_prompt, c_sample, rel_bias, w_ada, b_ada, norm_w, w_in, conv_w, conv_b, dt_bias,
           a_log, d_skip, ssd_norm_w, q_norm_w, k_norm_w, pool_w, pool_b, pool_scale,
           w_branch, w_out):
    bp = x_prompt.shape[0]
    f32 = F32
    mods = ada_mod(jnp.concatenate([c_prompt, c_sample], axis=0), w_ada, b_ada)
    w_in_r = _reorder_w_in(w_in)
    w_branch_b = w_branch.astype(BF16)
    w_out_b = w_out.astype(BF16)

    empty_kv = jnp.zeros((bp, 0, KV_HEADS, HEAD_DIM), f32)
    empty_ik = jnp.zeros((bp, 0, IDX_DIM), f32)
    zero_ssm = jnp.zeros((bp, SSD_HEADS, SSD_HEAD_DIM, SSD_STATE), f32)
    zero_conv = jnp.zeros((bp, CONV_WIDTH - 1, CONV_DIM), f32)
    zero_pool = jnp.zeros((bp, POOL_STATE, BRANCH_DIM), f32)

    xp, xs = x_prompt, x_sample
    outs_p = [[] for _ in range(6)]
    outs_s = [[] for _ in range(6)]
    for l in range(DEPTH):
        lw = (norm_w[l], w_in_r[l], conv_w[l], conv_b[l], dt_bias[l], a_log[l], d_skip[l], ssd_norm_w[l],
              q_norm_w[l], k_norm_w[l], pool_w[l], pool_b[l], pool_scale[l], w_branch_b[l], w_out_b[l])
        rp = trunk_layer(xp, mods[l, :bp], empty_kv, empty_kv, empty_ik, zero_ssm, zero_conv, zero_pool,
                         rel_bias, lw, per_row_mod=False)
        rs = trunk_layer(xs, mods[l, bp:], cache_k[l], cache_v[l], cache_idx_k[l], state_ssm[l],
                         state_conv[l], state_pool[l], rel_bias, lw, per_row_mod=True)
        xp, xs = rp[0], rs[0]
        for n in range(6):
            outs_p[n].append(rp[n + 1])
            outs_s[n].append(rs[n + 1])
    return (xp, xs, *[jnp.stack(o) for o in outs_p], *[jnp.stack(o) for o in outs_s])
```

```python
import functools
import math

import numpy as np
import jax
import jax.numpy as jnp
from jax import lax
from jax.experimental import pallas as pl
from jax.experimental.pallas import tpu as pltpu

F32 = jnp.float32
BF16 = jnp.bfloat16
I32 = jnp.int32

D_MODEL = 2048
DEPTH = 4
CHUNK = 64
N_BRANCH = 3
BRANCH_DIM = 1024
SSD_INNER = BRANCH_DIM
SSD_HEAD_DIM = 64
SSD_HEADS = SSD_INNER // SSD_HEAD_DIM
SSD_GROUPS = 2
SSD_STATE = 128
CONV_WIDTH = 4
CONV_DIM = SSD_INNER + 2 * SSD_GROUPS * SSD_STATE
ATTN_HEADS = 16
KV_HEADS = 4
HEAD_DIM = BRANCH_DIM // ATTN_HEADS
Q_PER_KV = ATTN_HEADS // KV_HEADS
IDX_HEADS = 8
IDX_DIM = 64
TOPK_MAX = 256
N_BUCKETS = 32
MAX_DISTANCE = 128
POOL_WINDOWS = (2, 4, 8, 16)
POOL_GROUPS = 4
POOL_GROUP_DIM = BRANCH_DIM // POOL_GROUPS
POOL_STATE = 15
EPS = 1e-6
NEG = -1e30

LANES = 128
VMEM_LIMIT = 56 * 1024 * 1024

_ORIG = {}
_off = 0
for _name, _size in (("z", SSD_INNER), ("xbc", CONV_DIM), ("dt", SSD_HEADS), ("q", BRANCH_DIM),
                     ("k", KV_HEADS * HEAD_DIM), ("v", KV_HEADS * HEAD_DIM), ("ag", BRANCH_DIM),
                     ("iq", IDX_HEADS * IDX_DIM), ("ik", IDX_DIM), ("iw", IDX_HEADS),
                     ("u", BRANCH_DIM), ("pg", BRANCH_DIM), ("mg", N_BRANCH * D_MODEL)):
    _ORIG[_name] = (_off, _size)
    _off += _size
IN_DIM = _off

_NEW_ORDER = ("mg", "z", "q", "ag", "u", "pg", "xbc", "iq", "k", "v", "ik", "dt", "iw")
COL = {}
_off = 0
for _name in _NEW_ORDER:
    COL[_name] = _off
    _off += -(-_ORIG[_name][1] // LANES) * LANES
PROJ_DIM = -(-_off // 2048) * 2048
COL["xs"] = COL["xbc"]
COL["bm"] = COL["xbc"] + SSD_INNER
COL["cm"] = COL["bm"] + SSD_GROUPS * SSD_STATE


def _sortable_const(v):
    i = int(np.float32(v).view(np.int32))
    return i ^ ((i >> 31) & 0x7FFFFFFF)


NEG_KEY = _sortable_const(NEG)
INT_MIN = -2 ** 31


def _cparams(sem):
    return pltpu.CompilerParams(dimension_semantics=sem, vmem_limit_bytes=VMEM_LIMIT)


def _silu(x):
    return x * jax.nn.sigmoid(x)


def _ada_kernel(c_ref, w_ref, b_ref, o_ref):
    c = c_ref[...]
    o_ref[...] = jnp.dot(_silu(c).astype(BF16), w_ref[...].astype(BF16),
                         preferred_element_type=F32) + b_ref[...]


def ada_mod(c_all, w_ada, b_ada):
    nb, d = c_all.shape
    n = w_ada.shape[-1]
    tn = 512
    return pl.pallas_call(
        _ada_kernel,
        grid=(DEPTH, n // tn),
        in_specs=[pl.BlockSpec((nb, d), lambda l, j: (0, 0)),
                  pl.BlockSpec((None, d, tn), lambda l, j: (l, 0, j)),
                  pl.BlockSpec((None, 1, tn), lambda l, j: (l, 0, j))],
        out_specs=pl.BlockSpec((None, nb, tn), lambda l, j: (l, 0, j)),
        out_shape=jax.ShapeDtypeStruct((DEPTH, nb, n), F32),
        compiler_params=_cparams(("arbitrary", "arbitrary")),
        name="ada",
    )(c_all, w_ada, b_ada.reshape(DEPTH, 1, n))


def _inproj_kernel(x_ref, sc_ref, sh_ref, nw_ref, w_ref, o_ref, h_ref):
    @pl.when(pl.program_id(1) == 0)
    def _():
        x = x_ref[...]
        ms = jnp.mean(x * x, axis=-1, keepdims=True)
        y = x * lax.rsqrt(ms + EPS) * nw_ref[...]
        h_ref[...] = (y * (1.0 + sc_ref[...]) + sh_ref[...]).astype(BF16)

    o_ref[...] = jnp.dot(h_ref[...], w_ref[...], preferred_element_type=F32)


def _mod_spec(mod, tm, tn, rows_per_mod, col_of_j):
    r = mod.shape[1]
    if r == 1:
        tiles = rows_per_mod // tm
        return pl.BlockSpec((None, 1, tn), lambda i, j: (i // tiles, 0, col_of_j(j)))
    return pl.BlockSpec((None, r, tn), lambda i, j: (i, 0, col_of_j(j)))


def inproj(x, scale, shift, norm_w, w, tm, rows_per_mod):
    t, d = x.shape
    n = w.shape[1]
    tn = 1024
    zero = lambda j: 0
    return pl.pallas_call(
        _inproj_kernel,
        grid=(t // tm, n // tn),
        in_specs=[pl.BlockSpec((tm, d), lambda i, j: (i, 0)),
                  _mod_spec(scale, tm, d, rows_per_mod, zero),
                  _mod_spec(shift, tm, d, rows_per_mod, zero),
                  pl.BlockSpec((1, d), lambda i, j: (0, 0)),
                  pl.BlockSpec((d, tn), lambda i, j: (0, j))],
        out_specs=pl.BlockSpec((tm, tn), lambda i, j: (i, j)),
        out_shape=jax.ShapeDtypeStruct((t, n), F32),
        scratch_shapes=[pltpu.VMEM((tm, d), BF16)],
        compiler_params=_cparams(("arbitrary", "arbitrary")),
        name="inproj",
    )(x, scale, shift, norm_w.reshape(1, d), w)


def _prep_kernel(q_ref, k_ref, iq_ref, qw_ref, kw_ref, qh_ref, ko_ref, iqh_ref):
    def head_norm(xs, w):
        ms = jnp.mean(xs * xs, axis=-1, keepdims=True)
        return xs * lax.rsqrt(ms + EPS) * w

    q = q_ref[...]
    for h in range(ATTN_HEADS):
        qn = head_norm(q[:, h * HEAD_DIM:(h + 1) * HEAD_DIM], qw_ref[...])
        qh_ref[h] = (qn * HEAD_DIM ** -0.5).astype(BF16)
    k = k_ref[...]
    for g in range(KV_HEADS):
        ko_ref[:, g * HEAD_DIM:(g + 1) * HEAD_DIM] = head_norm(k[:, g * HEAD_DIM:(g + 1) * HEAD_DIM], kw_ref[...])
    iq = iq_ref[...]
    for h in range(IDX_HEADS):
        iqh_ref[h] = (iq[:, h * IDX_DIM:(h + 1) * IDX_DIM] * IDX_DIM ** -0.5).astype(BF16)


def prep(proj, q_norm_w, k_norm_w, tm):
    t = proj.shape[0]
    kw = KV_HEADS * HEAD_DIM
    iqw = IDX_HEADS * IDX_DIM
    return pl.pallas_call(
        _prep_kernel,
        grid=(t // tm,),
        in_specs=[pl.BlockSpec((tm, BRANCH_DIM), lambda i: (i, COL["q"] // BRANCH_DIM)),
                  pl.BlockSpec((tm, kw), lambda i: (i, COL["k"] // kw)),
                  pl.BlockSpec((tm, iqw), lambda i: (i, COL["iq"] // iqw)),
                  pl.BlockSpec((1, HEAD_DIM), lambda i: (0, 0)),
                  pl.BlockSpec((1, HEAD_DIM), lambda i: (0, 0))],
        out_specs=[pl.BlockSpec((ATTN_HEADS, tm, HEAD_DIM), lambda i: (0, i, 0)),
                   pl.BlockSpec((tm, kw), lambda i: (i, 0)),
                   pl.BlockSpec((IDX_HEADS, tm, IDX_DIM), lambda i: (0, i, 0))],
        out_shape=[jax.ShapeDtypeStruct((ATTN_HEADS, t, HEAD_DIM), BF16),
                   jax.ShapeDtypeStruct((t, kw), F32),
                   jax.ShapeDtypeStruct((IDX_HEADS, t, IDX_DIM), BF16)],
        compiler_params=_cparams(("arbitrary",)),
        name="prep",
    )(proj, proj, proj, q_norm_w.reshape(1, HEAD_DIM), k_norm_w.reshape(1, HEAD_DIM))


TKS = 256
NEAR_W = 2 * TKS
FAR_BUCKET = N_BUCKETS // 2 - 1
J_ALL = 2 ** 30


def _dsa_kernel(tab_ref, qh_ref, iqh_ref, iw_ref, ikT_ref, kT_ref, v_ref, bk_ref, ag_ref,
                o_ref, keys_ref, nb_ref, acc_ref, m_ref, l_ref, j_ref, *,
                tq, ta, tk, start, n_keys, topk):
    b = pl.program_id(0)
    i = pl.program_id(1)
    kk = pl.program_id(2)
    nk = pl.num_programs(2)
    q0 = start + i * tq
    vis_end = ((q0 + tq - 1) // CHUNK + 1) * CHUNK
    n_cols = jnp.minimum(vis_end, n_keys)
    n_a = (n_cols + ta - 1) // ta
    n_sub = (n_cols + TKS - 1) // TKS
    extra = jnp.maximum(n_keys - n_a * ta, 0).astype(F32)
    sub_per_tile = tk // TKS

    def vis_mask(t, width):
        kpos = t * width + lax.broadcasted_iota(I32, (tq, width), 1)
        qpos = q0 + lax.broadcasted_iota(I32, (tq, width), 0)
        vis = (kpos // CHUNK) <= (qpos // CHUNK)
        return vis, kpos

    @pl.when((b == 0) & (i == 0) & (kk == 0))
    def _():
        bk = bk_ref[...]

        def per_head(h, _):
            def per_bucket(n, val):
                return jnp.where(bk == n, tab_ref[n, h], val)
            val = lax.fori_loop(0, N_BUCKETS, per_bucket, jnp.zeros((tq, NEAR_W), F32))
            nb_ref[h] = val - tab_ref[FAR_BUCKET, h]
            return 0
        lax.fori_loop(0, ATTN_HEADS, per_head, 0)

    @pl.when(kk == 0)
    def _():
        iw = iw_ref[...] * IDX_HEADS ** -0.5

        def score_tile(t, _):
            c0 = pl.multiple_of(t * ta, ta)
            ikt = ikT_ref[:, pl.ds(c0, ta)]
            acc = jnp.zeros((tq, ta), F32)
            for h in range(IDX_HEADS):
                s = jnp.dot(iqh_ref[h], ikt, preferred_element_type=F32)
                acc = acc + iw[:, h:h + 1] * jnp.maximum(s, 0.0)
            vis, kpos = vis_mask(t, ta)
            acc = jnp.where(vis, acc, NEG)
            bits = pltpu.bitcast(acc, I32)
            key = bits ^ ((bits >> 31) & 0x7FFFFFFF)
            key = jnp.where(kpos < n_keys, key, INT_MIN)
            keys_ref[:, pl.ds(c0, ta)] = key
            return 0
        lax.fori_loop(0, n_a, score_tile, 0)

        def count(pred):
            def body(t, cnt):
                c0 = pl.multiple_of(t * ta, ta)
                blk = keys_ref[:, pl.ds(c0, ta)]
                c = jnp.where(pred(blk, t), 1.0, 0.0)
                for jj in range(ta // LANES):
                    cnt = cnt + c[:, jj * LANES:(jj + 1) * LANES]
                return cnt
            cnt = lax.fori_loop(0, n_a, body, jnp.zeros((tq, LANES), F32))
            return jnp.sum(cnt, axis=1, keepdims=True)

        def bit_step(it, prefix_u):
            bit = jnp.left_shift(jnp.int32(1), 31 - it)
            cand_u = prefix_u | bit
            cand_s = cand_u ^ INT_MIN
            cnt = count(lambda blk, t: blk >= cand_s) + jnp.where(NEG_KEY >= cand_s, extra, 0.0)
            return jnp.where(cnt >= topk, cand_u, prefix_u)
        prefix_u = lax.fori_loop(0, 32, bit_step, jnp.zeros((tq, 1), I32))
        thr = prefix_u ^ INT_MIN

        cnt_gt = count(lambda blk, t: blk > thr) + jnp.where(NEG_KEY > thr, extra, 0.0)
        cnt_eq = count(lambda blk, t: blk == thr)
        need = topk - cnt_gt
        j_ref[...] = jnp.full((tq, LANES), J_ALL, I32)

        @pl.when(jnp.max(cnt_eq - need) > 0.0)
        def _():
            def idx_step(it, jmax):
                cand = jmax | jnp.left_shift(jnp.int32(1), 14 - it)

                def pred(blk, t):
                    _, kpos = vis_mask(t, ta)
                    return (blk == thr) & (kpos < cand)
                g = count(pred)
                return jnp.where(g < need, cand, jmax)
            jmax = lax.fori_loop(0, 15, idx_step, jnp.zeros((tq, 1), I32))
            j_ref[...] = jnp.broadcast_to(jmax, (tq, LANES))

        jmax = j_ref[:, 0:1]

        def mask_tile(t, _):
            c0 = pl.multiple_of(t * ta, ta)
            blk = keys_ref[:, pl.ds(c0, ta)]
            vis, kpos = vis_mask(t, ta)
            sel = (blk > thr) | ((blk == thr) & (kpos <= jmax))
            ok = sel & vis & (kpos < n_keys)
            madd = jnp.where(ok, 0.0, NEG).astype(F32)
            keys_ref[:, pl.ds(c0, ta)] = pltpu.bitcast(madd, I32)
            return 0
        lax.fori_loop(0, n_a, mask_tile, 0)

        m_ref[...] = jnp.full(m_ref.shape, -jnp.inf, F32)
        l_ref[...] = jnp.zeros(l_ref.shape, F32)
        acc_ref[...] = jnp.zeros(acc_ref.shape, F32)

    def attend(jl, boff):
        lo = pl.multiple_of(jl * TKS, TKS)
        c0 = pl.multiple_of(kk * tk + lo, TKS)
        madd = pltpu.bitcast(keys_ref[:, pl.ds(c0, TKS)], F32)

        def group(g, _):
            kt = kT_ref[g, :, pl.ds(lo, TKS)]
            vv = v_ref[g, pl.ds(lo, TKS), :]
            for r in range(Q_PER_KV):
                h = g * Q_PER_KV + r
                s = jnp.dot(qh_ref[h], kt, preferred_element_type=F32) + madd
                if boff is not None:
                    s = s + nb_ref[h, :, boff:boff + TKS]
                m_prev = m_ref[h]
                m_new = jnp.maximum(m_prev, jnp.max(s, axis=1, keepdims=True))
                alpha = jnp.exp(m_prev - m_new)
                p = jnp.exp(s - m_new[:, 0:1])
                l_ref[h] = alpha * l_ref[h] + jnp.sum(p, axis=1, keepdims=True)
                acc_ref[h] = alpha[:, 0:HEAD_DIM] * acc_ref[h] + jnp.dot(
                    p.astype(BF16), vv, preferred_element_type=F32)
                m_ref[h] = m_new
            return 0
        lax.fori_loop(0, KV_HEADS, group, 0)

    n_here = jnp.clip(n_sub - kk * sub_per_tile, 0, sub_per_tile)

    def sub_body(jl, _):
        u = kk * sub_per_tile + jl

        @pl.when(u < n_sub - 2)
        def _():
            attend(jl, None)

        @pl.when(u == n_sub - 2)
        def _():
            attend(jl, 0)

        @pl.when(u == n_sub - 1)
        def _():
            attend(jl, TKS)
        return 0
    lax.fori_loop(0, n_here, sub_body, 0)

    @pl.when(kk == nk - 1)
    def _():
        ag = ag_ref[...]
        for h in range(ATTN_HEADS):
            o = acc_ref[h] / l_ref[h][:, 0:HEAD_DIM]
            sl = slice(h * HEAD_DIM, (h + 1) * HEAD_DIM)
            o_ref[:, sl] = (o * _silu(ag[:, sl])).astype(o_ref.dtype)


def dsa(qh, iqh, proj, ikT, kT, v4, bk, rel_bias, *, nbatch, tq, ta, tk, start, n_keys, topk):
    t = qh.shape[1]
    s_pad = ikT.shape[-1]
    nq = t // (nbatch * tq)
    nk = s_pad // tk

    def kt_idx(b, i, kk):
        q0 = start + i * tq
        n_cols = jnp.minimum(((q0 + tq - 1) // CHUNK + 1) * CHUNK, n_keys)
        return jnp.minimum(kk, (n_cols + tk - 1) // tk - 1)

    kern = functools.partial(_dsa_kernel, tq=tq, ta=ta, tk=tk, start=start, n_keys=n_keys, topk=topk)
    return pl.pallas_call(
        kern,
        grid=(nbatch, nq, nk),
        in_specs=[pl.BlockSpec(memory_space=pltpu.SMEM),
                  pl.BlockSpec((ATTN_HEADS, tq, HEAD_DIM), lambda b, i, kk: (0, b * nq + i, 0)),
                  pl.BlockSpec((IDX_HEADS, tq, IDX_DIM), lambda b, i, kk: (0, b * nq + i, 0)),
                  pl.BlockSpec((tq, LANES), lambda b, i, kk: (b * nq + i, COL["iw"] // LANES)),
                  pl.BlockSpec((None, IDX_DIM, s_pad), lambda b, i, kk: (b, 0, 0)),
                  pl.BlockSpec((None, KV_HEADS, HEAD_DIM, tk), lambda b, i, kk: (b, 0, 0, kt_idx(b, i, kk))),
                  pl.BlockSpec((None, KV_HEADS, tk, HEAD_DIM), lambda b, i, kk: (b, 0, kt_idx(b, i, kk), 0)),
                  pl.BlockSpec((tq, NEAR_W), lambda b, i, kk: (0, 0)),
                  pl.BlockSpec((tq, BRANCH_DIM), lambda b, i, kk: (b * nq + i, COL["ag"] // BRANCH_DIM))],
        out_specs=pl.BlockSpec((tq, BRANCH_DIM), lambda b, i, kk: (b * nq + i, 0)),
        out_shape=jax.ShapeDtypeStruct((t, BRANCH_DIM), BF16),
        scratch_shapes=[pltpu.VMEM((tq, s_pad), I32),
                        pltpu.VMEM((ATTN_HEADS, tq, NEAR_W), F32),
                        pltpu.VMEM((ATTN_HEADS, tq, HEAD_DIM), F32),
                        pltpu.VMEM((ATTN_HEADS, tq, LANES), F32),
                        pltpu.VMEM((ATTN_HEADS, tq, LANES), F32),
                        pltpu.VMEM((tq, LANES), I32)],
        compiler_params=_cparams(("arbitrary", "arbitrary", "arbitrary")),
        name="dsa",
    )(rel_bias, qh, iqh, proj, ikT, kT, v4, bk, proj)


AUG_K = 128
V_ROWS = 80
CNT_ROWS = 64
FAST_BOUND = 30.0


def _dsat_kernel(tab_ref, kmax_ref, qT_ref, iqT_ref, iw_ref, ik_ref, k_ref, vT_ref, bk_ref, ag_ref,
                 o_ref, keys_ref, nb_ref, acc_ref, m_ref, qa_ref, j_ref, bmax_ref, fast_ref, *,
                 tq, ta, tk, start, n_keys, topk):
    b = pl.program_id(0)
    i = pl.program_id(1)
    kk = pl.program_id(2)
    nk = pl.num_programs(2)
    q0 = start + i * tq
    vis_end = ((q0 + tq - 1) // CHUNK + 1) * CHUNK
    n_cols = jnp.minimum(vis_end, n_keys)
    n_a = (n_cols + ta - 1) // ta
    n_sub = (n_cols + TKS - 1) // TKS
    extra = jnp.maximum(n_keys - n_a * ta, 0).astype(F32)
    sub_per_tile = tk // TKS

    def key_pos(t):
        return t * ta + lax.broadcasted_iota(I32, (ta, tq), 0)

    def vis_of(kpos):
        qpos = q0 + lax.broadcasted_iota(I32, (ta, tq), 1)
        return (kpos >> 6) <= (qpos >> 6)

    def colsum(x):
        return jnp.sum(x.reshape(ta // CNT_ROWS, CNT_ROWS, tq), axis=0)

    @pl.when((b == 0) & (i == 0) & (kk == 0))
    def _():
        bk = bk_ref[...]

        def per_head(h, bmax):
            def per_bucket(n, val):
                return jnp.where(bk == n, tab_ref[n, h], val)
            val = lax.fori_loop(0, N_BUCKETS, per_bucket, jnp.zeros((NEAR_W, tq), F32))
            val = val - tab_ref[FAR_BUCKET, h]
            nb_ref[h] = val
            return jnp.maximum(bmax, jnp.max(val))
        bmax_ref[0] = lax.fori_loop(0, ATTN_HEADS, per_head, jnp.float32(0.0))

    @pl.when(kk == 0)
    def _():
        rows = lax.broadcasted_iota(I32, (AUG_K, tq), 0)
        worst = jnp.zeros((1, tq), F32)
        for h in range(ATTN_HEADS):
            q = qT_ref[h].astype(F32)
            nrm = jnp.sqrt(jnp.sum(q * q, axis=0, keepdims=True))
            bound = nrm * kmax_ref[b, h // Q_PER_KV] + bmax_ref[0]
            worst = jnp.maximum(worst, bound)
            qpad = jnp.concatenate([q, jnp.zeros((AUG_K - HEAD_DIM, tq), F32)], axis=0)
            qa_ref[h] = jnp.where(rows == HEAD_DIM, -bound, qpad).astype(BF16)
        fast_ref[0] = (jnp.max(worst) <= FAST_BOUND).astype(I32)

        iw = iw_ref[...] * IDX_HEADS ** -0.5

        def score_tile(t, edge):
            c0 = pl.multiple_of(t * ta, ta)
            ikt = ik_ref[pl.ds(c0, ta), :]
            acc = jnp.zeros((ta, tq), F32)
            for h in range(IDX_HEADS):
                s = jnp.dot(ikt, iqT_ref[h], preferred_element_type=F32)
                acc = acc + iw[h:h + 1, :] * jnp.maximum(s, 0.0)
            if edge:
                kpos = key_pos(t)
                acc = jnp.where(vis_of(kpos), acc, NEG)
            bits = pltpu.bitcast(acc, I32)
            key = bits ^ ((bits >> 31) & 0x7FFFFFFF)
            if edge:
                key = jnp.where(kpos < n_keys, key, INT_MIN)
            keys_ref[pl.ds(c0, ta), :] = key
        lax.fori_loop(0, n_a - 1, lambda t, _: (score_tile(t, False), 0)[1], 0)
        score_tile(n_a - 1, True)

        def count(pred):
            def body(t, cnt):
                c0 = pl.multiple_of(t * ta, ta)
                blk = keys_ref[pl.ds(c0, ta), :]
                return cnt + colsum(jnp.where(pred(blk, t), 1.0, 0.0))
            cnt = lax.fori_loop(0, n_a, body, jnp.zeros((CNT_ROWS, tq), F32))
            return jnp.sum(cnt, axis=0, keepdims=True)

        def bit_step(it, prefix_u):
            bit = jnp.left_shift(jnp.int32(1), 31 - it)
            cand_u = prefix_u | bit
            cand_s = cand_u ^ INT_MIN
            cnt = count(lambda blk, t: blk >= cand_s) + jnp.where(NEG_KEY >= cand_s, extra, 0.0)
            return jnp.where(cnt >= topk, cand_u, prefix_u)
        prefix_u = lax.fori_loop(0, 32, bit_step, jnp.zeros((1, tq), I32))
        thr = prefix_u ^ INT_MIN

        cnt_gt = count(lambda blk, t: blk > thr) + jnp.where(NEG_KEY > thr, extra, 0.0)
        cnt_eq = count(lambda blk, t: blk == thr)
        need = topk - cnt_gt
        j_ref[...] = jnp.full((8, tq), J_ALL, I32)

        @pl.when(jnp.max(cnt_eq - need) > 0.0)
        def _():
            def idx_step(it, jmax):
                cand = jmax | jnp.left_shift(jnp.int32(1), 14 - it)
                g = count(lambda blk, t: (blk == thr) & (key_pos(t) < cand))
                return jnp.where(g < need, cand, jmax)
            jmax = lax.fori_loop(0, 15, idx_step, jnp.zeros((1, tq), I32))
            j_ref[...] = jnp.broadcast_to(jmax, (8, tq))

        jmax = j_ref[0:1, :]

        def mask_tile(t, edge):
            c0 = pl.multiple_of(t * ta, ta)
            blk = keys_ref[pl.ds(c0, ta), :]
            kpos = key_pos(t)
            ok = (blk > thr) | ((blk == thr) & (kpos <= jmax))
            if edge:
                ok = ok & vis_of(kpos) & (kpos < n_keys)
            keys_ref[pl.ds(c0, ta), :] = pltpu.bitcast(jnp.where(ok, 1.0, 0.0).astype(F32), I32)
        lax.fori_loop(0, n_a - 1, lambda t, _: (mask_tile(t, False), 0)[1], 0)
        mask_tile(n_a - 1, True)

        m_ref[...] = jnp.full(m_ref.shape, -jnp.inf, F32)
        acc_ref[...] = jnp.zeros(acc_ref.shape, F32)

    def attend_fast(jl, boff):
        lo = pl.multiple_of(jl * TKS, TKS)
        c0 = pl.multiple_of(kk * tk + lo, TKS)
        mask = pltpu.bitcast(keys_ref[pl.ds(c0, TKS), :], F32)

        def logits(g):
            kt = k_ref[g, pl.ds(lo, TKS), :]
            return [jnp.dot(kt, qa_ref[g * Q_PER_KV + r], preferred_element_type=F32)
                    for r in range(Q_PER_KV)]

        s_cur = logits(0)
        for g in range(KV_HEADS):
            s_next = logits(g + 1) if g + 1 < KV_HEADS else None
            vt = vT_ref[g, :, pl.ds(lo, TKS)]
            for r in range(Q_PER_KV):
                h = g * Q_PER_KV + r
                s = s_cur[r]
                if boff is not None:
                    s = s + nb_ref[h, boff:boff + TKS, :]
                p = (jnp.exp(s) * mask).astype(BF16)
                acc_ref[h] = acc_ref[h] + jnp.dot(vt, p, preferred_element_type=F32)
            s_cur = s_next

    def attend_slow(jl, boff):
        lo = pl.multiple_of(jl * TKS, TKS)
        c0 = pl.multiple_of(kk * tk + lo, TKS)
        madd = (pltpu.bitcast(keys_ref[pl.ds(c0, TKS), :], F32) - 1.0) * (-NEG)

        def group(g, _):
            kt = k_ref[g, pl.ds(lo, TKS), :]
            vt = vT_ref[g, :, pl.ds(lo, TKS)]
            for r in range(Q_PER_KV):
                h = g * Q_PER_KV + r
                s = jnp.dot(kt, qa_ref[h], preferred_element_type=F32) + madd
                if boff is not None:
                    s = s + nb_ref[h, boff:boff + TKS, :]
                m_prev = m_ref[h]
                m_new = jnp.maximum(m_prev, jnp.max(s, axis=0, keepdims=True))
                alpha = jnp.exp(m_prev - m_new)
                p = jnp.exp(s - m_new[0:1, :])
                acc_ref[h] = alpha[0:1, :] * acc_ref[h] + jnp.dot(
                    vt, p.astype(BF16), preferred_element_type=F32)
                m_ref[h] = m_new
            return 0
        lax.fori_loop(0, KV_HEADS, group, 0)

    n_here = jnp.clip(n_sub - kk * sub_per_tile, 0, sub_per_tile)
    fast = fast_ref[0] == 1

    def sub_loop(attend):
        def sub_body(jl, _):
            u = kk * sub_per_tile + jl

            @pl.when(u < n_sub - 2)
            def _():
                attend(jl, None)

            @pl.when(u == n_sub - 2)
            def _():
                attend(jl, 0)

            @pl.when(u == n_sub - 1)
            def _():
                attend(jl, TKS)
            return 0
        lax.fori_loop(0, n_here, sub_body, 0)

    @pl.when(fast)
    def _():
        sub_loop(attend_fast)

    @pl.when(jnp.logical_not(fast))
    def _():
        sub_loop(attend_slow)

    @pl.when(kk == nk - 1)
    def _():
        outs = []
        for h in range(ATTN_HEADS):
            a = acc_ref[h]
            outs.append(a[0:HEAD_DIM, :] / a[HEAD_DIM:HEAD_DIM + 1, :])
        o = jnp.concatenate(outs, axis=0).T
        o_ref[...] = (o * _silu(ag_ref[...])).astype(o_ref.dtype)


def dsa_t(qT, iqT, iwT, proj, ik, k4, vT, bkT, rel_bias, *, nbatch, tq, ta, tk, start, n_keys, topk):
    t = qT.shape[2]
    s_pad = ik.shape[1]
    nq = t // (nbatch * tq)
    nk = s_pad // tk
    assert tq % LANES == 0 and ta % tq == 0 and tk % TKS == 0 and s_pad % ta == 0 and ta % CNT_ROWS == 0

    kf = k4.astype(F32)
    kmax = jnp.sqrt(jnp.max(jnp.sum(kf * kf, axis=-1), axis=-1))
    real = (jnp.arange(s_pad) < n_keys).astype(BF16)
    k_aug = jnp.concatenate([k4, jnp.broadcast_to(real[None, None, :, None], k4.shape[:3] + (1,)),
                             jnp.zeros(k4.shape[:3] + (AUG_K - HEAD_DIM - 1,), BF16)], axis=-1)
    v_aug = jnp.concatenate([vT, jnp.broadcast_to(real[None, None, None, :], vT.shape[:2] + (1, s_pad)),
                             jnp.zeros(vT.shape[:2] + (V_ROWS - HEAD_DIM - 1, s_pad), BF16)], axis=2)

    def kt_idx(b, i, kk):
        q0 = start + i * tq
        n_cols = jnp.minimum(((q0 + tq - 1) // CHUNK + 1) * CHUNK, n_keys)
        return jnp.minimum(kk, (n_cols + tk - 1) // tk - 1)

    kern = functools.partial(_dsat_kernel, tq=tq, ta=ta, tk=tk, start=start, n_keys=n_keys, topk=topk)
    smem = pl.BlockSpec(memory_space=pltpu.SMEM)
    return pl.pallas_call(
        kern,
        grid=(nbatch, nq, nk),
        in_specs=[smem, smem,
                  pl.BlockSpec((ATTN_HEADS, HEAD_DIM, tq), lambda b, i, kk: (0, 0, b * nq + i)),
                  pl.BlockSpec((IDX_HEADS, IDX_DIM, tq), lambda b, i, kk: (0, 0, b * nq + i)),
                  pl.BlockSpec((IDX_HEADS, tq), lambda b, i, kk: (0, b * nq + i)),
                  pl.BlockSpec((None, s_pad, IDX_DIM), lambda b, i, kk: (b, 0, 0)),
                  pl.BlockSpec((None, KV_HEADS, tk, AUG_K), lambda b, i, kk: (b, 0, kt_idx(b, i, kk), 0)),
                  pl.BlockSpec((None, KV_HEADS, V_ROWS, tk), lambda b, i, kk: (b, 0, 0, kt_idx(b, i, kk))),
                  pl.BlockSpec((NEAR_W, tq), lambda b, i, kk: (0, 0)),
                  pl.BlockSpec((tq, BRANCH_DIM), lambda b, i, kk: (b * nq + i, COL["ag"] // BRANCH_DIM))],
        out_specs=pl.BlockSpec((tq, BRANCH_DIM), lambda b, i, kk: (b * nq + i, 0)),
        out_shape=jax.ShapeDtypeStruct((t, BRANCH_DIM), BF16),
        scratch_shapes=[pltpu.VMEM((s_pad, tq), I32),
                        pltpu.VMEM((ATTN_HEADS, NEAR_W, tq), F32),
                        pltpu.VMEM((ATTN_HEADS, V_ROWS, tq), F32),
                        pltpu.VMEM((ATTN_HEADS, 8, tq), F32),
                        pltpu.VMEM((ATTN_HEADS, AUG_K, tq), BF16),
                        pltpu.VMEM((8, tq), I32),
                        pltpu.SMEM((1,), F32),
                        pltpu.SMEM((1,), I32)],
        compiler_params=_cparams(("arbitrary", "arbitrary", "arbitrary")),
        name="dsa_t",
    )(rel_bias, kmax, qT, iqT, iwT, ik, k_aug, v_aug, bkT, proj)


def t5_bucket(rel):
    half = N_BUCKETS // 2
    max_exact = half // 2
    ret = jnp.where(rel > 0, half, 0)
    n = jnp.abs(rel)
    nf = jnp.maximum(n, max_exact).astype(jnp.float32)
    large = max_exact + (jnp.log(nf / max_exact) / math.log(MAX_DISTANCE / max_exact)
                         * (half - max_exact)).astype(jnp.int32)
    large = jnp.minimum(large, half - 1)
    return ret + jnp.where(n < max_exact, n, large)


def near_buckets(tq, q0, n_keys):
    vis_end = ((q0 + tq - 1) // CHUNK + 1) * CHUNK
    end = -(-min(vis_end, n_keys) // TKS) * TKS
    assert end - NEAR_W <= q0 - MAX_DISTANCE + 1, "near window must cover every non-saturated offset"
    kpos = end - NEAR_W + jnp.arange(NEAR_W, dtype=jnp.int32)[None, :]
    qpos = q0 + jnp.arange(tq, dtype=jnp.int32)[:, None]
    return t5_bucket(kpos - qpos)


def _ssd_kernel(xs_ref, bm_ref, cm_ref, dt_ref, z_ref, hist_ref, st0_ref, cw_ref, cb_ref,
                dtb_ref, alog_ref, dsk_ref, nw_ref, exp_ref, y_ref, st_ref,
                ext_ref, state_ref, *, q, valid):
    c = pl.program_id(1)
    nc = pl.num_programs(1)
    hp = SSD_INNER
    gw = hp // SSD_GROUPS
    hi = lax.Precision.HIGHEST

    @pl.when(c == 0)
    def _():
        ext_ref[0:8, :] = hist_ref[...]
        state_ref[...] = st0_ref[...]

    ext_ref[8:8 + q, 0:hp] = xs_ref[...]
    ext_ref[8:8 + q, hp:hp + 256] = bm_ref[...]
    ext_ref[8:8 + q, hp + 256:hp + 512] = cm_ref[...]
    conv = jnp.zeros((q, CONV_DIM), F32) + cb_ref[...]
    for j in range(CONV_WIDTH):
        conv = conv + ext_ref[5 + j:5 + j + q, :] * cw_ref[j:j + 1, :]
    ext_ref[0:8, :] = ext_ref[q:q + 8, :]
    conv = _silu(conv)
    xs = conv[:, 0:hp]
    bmat = conv[:, hp:hp + 256]
    cmat = conv[:, hp + 256:hp + 512]

    xdt = dt_ref[...] + dtb_ref[...]
    dt = jnp.maximum(xdt, 0.0) + jnp.log1p(jnp.exp(-jnp.abs(xdt)))
    if valid < q:
        rows = lax.broadcasted_iota(I32, (q, LANES), 0)
        dt = jnp.where(rows < valid, dt, 0.0)
    adt = dt * (-jnp.exp(alog_ref[...]))
    rr = lax.broadcasted_iota(I32, (q, q), 0)
    cc = lax.broadcasted_iota(I32, (q, q), 1)
    tri = rr >= cc
    acum = jnp.dot(tri.astype(F32), adt, precision=hi, preferred_element_type=F32)
    acum_t = acum.T
    dt_t = dt.T
    alast = acum[q - 1:q, :]

    expand = exp_ref[...]
    e_acum = jnp.dot(jnp.exp(acum), expand, precision=hi, preferred_element_type=F32)
    e_tail = jnp.dot(jnp.exp(alast - acum) * dt, expand, precision=hi, preferred_element_type=F32)
    e_last = e_acum[q - 1:q, :]

    xw = (xs * e_tail).astype(BF16)
    xb = xs.astype(BF16)
    y_parts = []
    new_state = []
    for g in range(SSD_GROUPS):
        bg = bmat[:, g * SSD_STATE:(g + 1) * SSD_STATE]
        cg = cmat[:, g * SSD_STATE:(g + 1) * SSD_STATE].astype(BF16)
        bg_t = bg.T.astype(BF16)
        cb = jnp.dot(cg, bg_t, preferred_element_type=F32)
        st_g = state_ref[:, g * gw:(g + 1) * gw]
        y_off = jnp.dot(cg, st_g.astype(BF16), preferred_element_type=F32)
        new_state.append(jnp.dot(bg_t, xw[:, g * gw:(g + 1) * gw], preferred_element_type=F32))
        heads = []
        for r in range(SSD_HEADS // SSD_GROUPS):
            h = g * (SSD_HEADS // SSD_GROUPS) + r
            seg = acum[:, h:h + 1] - acum_t[h:h + 1, :]
            decay = jnp.where(tri, jnp.exp(jnp.where(tri, seg, 0.0)), 0.0)
            wmat = (cb * decay * dt_t[h:h + 1, :]).astype(BF16)
            heads.append(jnp.dot(wmat, xb[:, h * SSD_HEAD_DIM:(h + 1) * SSD_HEAD_DIM],
                                 preferred_element_type=F32))
        y_parts.append(jnp.concatenate(heads, axis=1) + y_off * e_acum[:, g * gw:(g + 1) * gw])
    y = jnp.concatenate(y_parts, axis=1)
    state_ref[...] = state_ref[...] * e_last + jnp.concatenate(new_state, axis=1)

    y = (y + dsk_ref[...] * xs) * _silu(z_ref[...])
    ms = jnp.mean(y * y, axis=-1, keepdims=True)
    y_ref[...] = (y * lax.rsqrt(ms + EPS) * nw_ref[...]).astype(y_ref.dtype)

    @pl.when(c == nc - 1)
    def _():
        st_ref[...] = state_ref[...]


def ssd(proj, hist8, state_t, conv_w, conv_b, dt_bias, a_log, d_skip, ssd_norm_w, *, nbatch, q, valid):
    t = proj.shape[0]
    nc = t // (nbatch * q)
    hp = SSD_INNER

    def pad_heads(v, fill):
        return jnp.concatenate([v.astype(F32), jnp.full((LANES - SSD_HEADS,), fill, F32)]).reshape(1, LANES)

    expand = (jnp.arange(LANES)[:, None] == (jnp.arange(hp)[None, :] // SSD_HEAD_DIM)).astype(F32)
    dsk = jnp.repeat(d_skip.astype(F32), SSD_HEAD_DIM).reshape(1, hp)
    kern = functools.partial(_ssd_kernel, q=q, valid=valid)
    const2 = lambda shape: pl.BlockSpec(shape, lambda b, c: (0, 0))
    return pl.pallas_call(
        kern,
        grid=(nbatch, nc),
        in_specs=[pl.BlockSpec((q, hp), lambda b, c: (b * nc + c, COL["xs"] // hp)),
                  pl.BlockSpec((q, 256), lambda b, c: (b * nc + c, COL["bm"] // 256)),
                  pl.BlockSpec((q, 256), lambda b, c: (b * nc + c, COL["cm"] // 256)),
                  pl.BlockSpec((q, LANES), lambda b, c: (b * nc + c, COL["dt"] // LANES)),
                  pl.BlockSpec((q, hp), lambda b, c: (b * nc + c, COL["z"] // hp)),
                  pl.BlockSpec((None, 8, CONV_DIM), lambda b, c: (b, 0, 0)),
                  pl.BlockSpec((None, SSD_STATE, hp), lambda b, c: (b, 0, 0)),
                  const2((CONV_WIDTH, CONV_DIM)), const2((1, CONV_DIM)),
                  const2((1, LANES)), const2((1, LANES)), const2((1, hp)), const2((1, hp)),
                  const2((LANES, hp))],
        out_specs=[pl.BlockSpec((q, hp), lambda b, c: (b * nc + c, 0)),
                   pl.BlockSpec((None, SSD_STATE, hp), lambda b, c: (b, 0, 0))],
        out_shape=[jax.ShapeDtypeStruct((t, hp), BF16),
                   jax.ShapeDtypeStruct((nbatch, SSD_STATE, hp), F32)],
        scratch_shapes=[pltpu.VMEM((q + 8, CONV_DIM), F32),
                        pltpu.VMEM((SSD_STATE, hp), F32)],
        compiler_params=_cparams(("arbitrary", "arbitrary")),
        name="ssd",
    )(proj, proj, proj, proj, proj, hist8, state_t, conv_w, conv_b.reshape(1, CONV_DIM),
      pad_heads(dt_bias, 0.0), pad_heads(a_log, 0.0), dsk, ssd_norm_w.reshape(1, hp), expand)


def _pool_kernel(u_ref, pg_ref, hist_ref, w_ref, b_ref, sc_ref, y_ref, ext_ref, *, r, start):
    c = pl.program_id(1)

    @pl.when(c == 0)
    def _():
        ext_ref[0:16, :] = hist_ref[...]

    ext_ref[16:16 + r, :] = u_ref[...]
    pos = start + c * r + lax.broadcasted_iota(I32, (r, 1), 0)
    outs = []
    for gi, w in enumerate(POOL_WINDOWS):
        lo = gi * POOL_GROUP_DIM
        cur = ext_ref[16:16 + r, lo:lo + POOL_GROUP_DIM]
        win = cur
        for s in range(1, w):
            win = win + ext_ref[16 - s:16 - s + r, lo:lo + POOL_GROUP_DIM]
        cnt = jnp.minimum(w, pos + 1).astype(F32)
        pooled = win / cnt - cur
        mixed = jnp.dot(pooled.astype(BF16), w_ref[gi], preferred_element_type=F32)
        outs.append(mixed + b_ref[gi:gi + 1, :])
    ext_ref[0:16, :] = ext_ref[r:r + 16, :]
    mixed = jnp.concatenate(outs, axis=1) * sc_ref[...]
    y_ref[...] = (mixed * _silu(pg_ref[...])).astype(y_ref.dtype)


def pool(proj, hist16, pool_w, pool_b, pool_scale, *, nbatch, r, start):
    t = proj.shape[0]
    nc = t // (nbatch * r)
    d = BRANCH_DIM
    kern = functools.partial(_pool_kernel, r=r, start=start)
    return pl.pallas_call(
        kern,
        grid=(nbatch, nc),
        in_specs=[pl.BlockSpec((r, d), lambda b, c: (b * nc + c, COL["u"] // d)),
                  pl.BlockSpec((r, d), lambda b, c: (b * nc + c, COL["pg"] // d)),
                  pl.BlockSpec((None, 16, d), lambda b, c: (b, 0, 0)),
                  pl.BlockSpec((POOL_GROUPS, POOL_GROUP_DIM, POOL_GROUP_DIM), lambda b, c: (0, 0, 0)),
                  pl.BlockSpec((POOL_GROUPS, POOL_GROUP_DIM), lambda b, c: (0, 0)),
                  pl.BlockSpec((1, d), lambda b, c: (0, 0))],
        out_specs=pl.BlockSpec((r, d), lambda b, c: (b * nc + c, 0)),
        out_shape=jax.ShapeDtypeStruct((t, d), BF16),
        scratch_shapes=[pltpu.VMEM((r + 16, d), F32)],
        compiler_params=_cparams(("arbitrary", "arbitrary")),
        name="pool",
    )(proj, proj, hist16, pool_w.astype(BF16), pool_b, pool_scale.reshape(1, d))


def _merge_kernel(y0_ref, y1_ref, y2_ref, g0_ref, g1_ref, g2_ref, w_ref, o_ref):
    acc = jax.nn.sigmoid(g0_ref[...]) * jnp.dot(y0_ref[...], w_ref[0], preferred_element_type=F32)
    acc = acc + jax.nn.sigmoid(g1_ref[...]) * jnp.dot(y1_ref[...], w_ref[1], preferred_element_type=F32)
    acc = acc + jax.nn.sigmoid(g2_ref[...]) * jnp.dot(y2_ref[...], w_ref[2], preferred_element_type=F32)
    o_ref[...] = acc.astype(o_ref.dtype)


def merge(y_ssd, y_att, y_pool, proj, w_branch, tm):
    t = y_ssd.shape[0]
    tn = 1024
    nj = D_MODEL // tn
    ysp = pl.BlockSpec((tm, BRANCH_DIM), lambda i, j: (i, 0))

    def gate_spec(bi):
        return pl.BlockSpec((tm, tn), lambda i, j: (i, (COL["mg"] + bi * D_MODEL) // tn + j))

    return pl.pallas_call(
        _merge_kernel,
        grid=(t // tm, nj),
        in_specs=[ysp, ysp, ysp, gate_spec(0), gate_spec(1), gate_spec(2),
                  pl.BlockSpec((N_BRANCH, BRANCH_DIM, tn), lambda i, j: (0, 0, j))],
        out_specs=pl.BlockSpec((tm, tn), lambda i, j: (i, j)),
        out_shape=jax.ShapeDtypeStruct((t, D_MODEL), BF16),
        compiler_params=_cparams(("arbitrary", "arbitrary")),
        name="merge",
    )(y_ssd, y_att, y_pool, proj, proj, proj, w_branch)


def _outproj_kernel(m_ref, w_ref, x_ref, g_ref, o_ref):
    o_ref[...] = x_ref[...] + g_ref[...] * jnp.dot(m_ref[...], w_ref[...], preferred_element_type=F32)


def outproj(merged, w_out, x, gate, tm, rows_per_mod):
    t = x.shape[0]
    tn = 1024
    return pl.pallas_call(
        _outproj_kernel,
        grid=(t // tm, D_MODEL // tn),
        in_specs=[pl.BlockSpec((tm, D_MODEL), lambda i, j: (i, 0)),
                  pl.BlockSpec((D_MODEL, tn), lambda i, j: (0, j)),
                  pl.BlockSpec((tm, tn), lambda i, j: (i, j)),
                  _mod_spec(gate, tm, tn, rows_per_mod, lambda j: j)],
        out_specs=pl.BlockSpec((tm, tn), lambda i, j: (i, j)),
        out_shape=jax.ShapeDtypeStruct((t, D_MODEL), F32),
        compiler_params=_cparams(("arbitrary", "arbitrary")),
        name="outproj",
    )(merged, w_out, x, gate)


def _pad_to(a, axis, size):
    pad = [(0, 0)] * a.ndim
    pad[axis] = (0, size - a.shape[axis])
    return jnp.pad(a, pad)


def trunk_layer(x, mod, k_past, v_past, ik_past, h0, conv_hist, pool_hist, rel_bias, lw, *, per_row_mod):
    (norm_w, w_in, conv_w, conv_b, dt_bias, a_log, d_skip, ssd_norm_w, q_norm_w, k_norm_w,
     pool_w, pool_b, pool_scale, w_branch, w_out) = lw
    bsz, seq, d = x.shape
    t = bsz * seq
    start = k_past.shape[1]
    n_keys = start + seq
    topk = min(TOPK_MAX, n_keys // 4)
    x2 = x.reshape(t, d)
    shift, scale, gate = mod[:, :d], mod[:, d:2 * d], mod[:, 2 * d:]
    tm = min(512, t)
    if per_row_mod:
        expand = lambda m: jnp.broadcast_to(m[:, None, :], (bsz, seq, d)).reshape(t // tm, tm, d)
    else:
        expand = lambda m: m[:, None, :]
    scale3, shift3, gate3 = expand(scale), expand(shift), expand(gate)

    proj = inproj(x2, scale3, shift3, norm_w, w_in, tm, seq)
    qh, k_new, iqh = prep(proj, q_norm_w, k_norm_w, tm)

    kw = KV_HEADS * HEAD_DIM
    v_new = proj[:, COL["v"]:COL["v"] + kw]
    ik_new = proj[:, COL["ik"]:COL["ik"] + IDX_DIM]
    k_new4 = k_new.reshape(bsz, seq, KV_HEADS, HEAD_DIM)
    v_new4 = v_new.reshape(bsz, seq, KV_HEADS, HEAD_DIM)
    ik_new3 = ik_new.reshape(bsz, seq, IDX_DIM)

    tq = min(256, seq)
    if seq >= 2048:
        tk, ta = 2048, 512
    else:
        tk = ta = None
    s_pad = -(-n_keys // TKS) * TKS
    if tk is None:
        tk, ta = s_pad, s_pad // 3 if (s_pad // 3) % TKS == 0 else s_pad
    s_pad = -(-s_pad // tk) * tk
    k_all = jnp.concatenate([k_past.astype(F32), k_new4], axis=1).astype(BF16)
    v_all = jnp.concatenate([v_past.astype(F32), v_new4], axis=1).astype(BF16)
    ik_all = jnp.concatenate([ik_past.astype(F32), ik_new3], axis=1).astype(BF16)
    bk = near_buckets(tq, start, n_keys)
    dsa_args = dict(nbatch=bsz, tq=tq, ta=ta, tk=tk, start=start, n_keys=n_keys, topk=topk)
    if tq % LANES == 0:
        k4 = _pad_to(jnp.transpose(k_all, (0, 2, 1, 3)), 2, s_pad)
        vT = _pad_to(jnp.transpose(v_all, (0, 2, 3, 1)), 3, s_pad)
        ik_p = _pad_to(ik_all, 1, s_pad)
        iwT = proj[:, COL["iw"]:COL["iw"] + IDX_HEADS].T
        y_att = dsa_t(jnp.swapaxes(qh, 1, 2), jnp.swapaxes(iqh, 1, 2), iwT, proj, ik_p, k4, vT, bk.T,
                      rel_bias, **dsa_args)
    else:
        kT = _pad_to(jnp.transpose(k_all, (0, 2, 3, 1)), 3, s_pad)
        v4 = _pad_to(jnp.transpose(v_all, (0, 2, 1, 3)), 2, s_pad)
        ikT = _pad_to(jnp.transpose(ik_all, (0, 2, 1)), 2, s_pad)
        y_att = dsa(qh, iqh, proj, ikT, kT, v4, bk, rel_bias, **dsa_args)

    q = 128
    hist8 = jnp.pad(conv_hist.astype(F32), ((0, 0), (8 - (CONV_WIDTH - 1), 0), (0, 0)))
    state_t = jnp.transpose(h0.astype(F32), (0, 3, 1, 2)).reshape(bsz, SSD_STATE, SSD_INNER)
    if seq < q:
        proj_ssd = _pad_to(proj.reshape(bsz, seq, PROJ_DIM), 1, q).reshape(bsz * q, PROJ_DIM)
        valid = seq
    else:
        proj_ssd, valid = proj, q
    y_ssd, st = ssd(proj_ssd, hist8, state_t, conv_w, conv_b, dt_bias, a_log, d_skip, ssd_norm_w,
                    nbatch=bsz, q=q, valid=valid)
    if seq < q:
        y_ssd = y_ssd.reshape(bsz, q, SSD_INNER)[:, :seq].reshape(t, SSD_INNER)
    h_last = jnp.transpose(st.reshape(bsz, SSD_STATE, SSD_HEADS, SSD_HEAD_DIM), (0, 2, 3, 1))

    hist16 = jnp.pad(pool_hist.astype(F32), ((0, 0), (16 - POOL_STATE, 0), (0, 0)))
    y_pool = pool(proj, hist16, pool_w, pool_b, pool_scale, nbatch=bsz, r=min(512, seq), start=start)

    merged = merge(y_ssd, y_att, y_pool, proj, w_branch, tm)
    x_new = outproj(merged, w_out, x2, gate3, tm, seq).reshape(bsz, seq, d)

    xbc = proj[:, COL["xbc"]:COL["xbc"] + CONV_DIM].reshape(bsz, seq, CONV_DIM)
    u = proj[:, COL["u"]:COL["u"] + BRANCH_DIM].reshape(bsz, seq, BRANCH_DIM)
    conv_new = jnp.concatenate([conv_hist.astype(F32), xbc], axis=1)[:, -(CONV_WIDTH - 1):]
    pool_new = jnp.concatenate([pool_hist.astype(F32), u[:, -min(seq, POOL_STATE):]], axis=1)[:, -POOL_STATE:]
    return x_new, k_new4, v_new4, ik_new3, h_last, conv_new, pool_new


def _reorder_w_in(w_in):
    parts = []
    for name in _NEW_ORDER:
        off, size = _ORIG[name]
        seg = w_in[..., off:off + size]
        padw = -(-size // LANES) * LANES - size
        if padw:
            seg = jnp.pad(seg, ((0, 0), (0, 0), (0, padw)))
        parts.append(seg)
    out = jnp.concatenate(parts, axis=-1)
    return _pad_to(out, 2, PROJ_DIM).astype(BF16)


def kernel(x_prompt, x_sample, cache_k, cache_v, cache_idx_k, state_ssm, state_conv, state_pool,
           c_prompt, c_sample, rel_bias, w_ada, b_ada, norm_w, w_in, conv_w, conv_b, dt_bias,
           a_log, d_skip, ssd_norm_w, q_norm_w, k_norm_w, pool_w, pool_b, pool_scale,
           w_branch, w_out):
    bp = x_prompt.shape[0]
    f32 = F32
    mods = ada_mod(jnp.concatenate([c_prompt, c_sample], axis=0), w_ada, b_ada)
    w_in_r = _reorder_w_in(w_in)
    w_branch_b = w_branch.astype(BF16)
    w_out_b = w_out.astype(BF16)

    empty_kv = jnp.zeros((bp, 0, KV_HEADS, HEAD_DIM), f32)
    empty_ik = jnp.zeros((bp, 0, IDX_DIM), f32)
    zero_ssm = jnp.zeros((bp, SSD_HEADS, SSD_HEAD_DIM, SSD_STATE), f32)
    zero_conv = jnp.zeros((bp, CONV_WIDTH - 1, CONV_DIM), f32)
    zero_pool = jnp.zeros((bp, POOL_STATE, BRANCH_DIM), f32)

    xp, xs = x_prompt, x_sample
    outs_p = [[] for _ in range(6)]
    outs_s = [[] for _ in range(6)]
    for l in range(DEPTH):
        lw = (norm_w[l], w_in_r[l], conv_w[l], conv_b[l], dt_bias[l], a_log[l], d_skip[l], ssd_norm_w[l],
              q_norm_w[l], k_norm_w[l], pool_w[l], pool_b[l], pool_scale[l], w_branch_b[l], w_out_b[l])
        rp = trunk_layer(xp, mods[l, :bp], empty_kv, empty_kv, empty_ik, zero_ssm, zero_conv, zero_pool,
                         rel_bias, lw, per_row_mod=False)
        rs = trunk_layer(xs, mods[l, bp:], cache_k[l], cache_v[l], cache_idx_k[l], state_ssm[l],
                         state_conv[l], state_pool[l], rel_bias, lw, per_row_mod=True)
        xp, xs = rp[0], rs[0]
        for n in range(6):
            outs_p[n].append(rp[n + 1])
            outs_s[n].append(rs[n + 1])
    return (xp, xs, *[jnp.stack(o) for o in outs_p], *[jnp.stack(o) for o in outs_s])
```

```python
import functools
import math

import numpy as np
import jax
import jax.numpy as jnp
from jax import lax
from jax.experimental import pallas as pl
from jax.experimental.pallas import tpu as pltpu

F32 = jnp.float32
BF16 = jnp.bfloat16
I32 = jnp.int32

D_MODEL = 2048
DEPTH = 4
CHUNK = 64
N_BRANCH = 3
BRANCH_DIM = 1024
SSD_INNER = BRANCH_DIM
SSD_HEAD_DIM = 64
SSD_HEADS = SSD_INNER // SSD_HEAD_DIM
SSD_GROUPS = 2
SSD_STATE = 128
CONV_WIDTH = 4
CONV_DIM = SSD_INNER + 2 * SSD_GROUPS * SSD_STATE
ATTN_HEADS = 16
KV_HEADS = 4
HEAD_DIM = BRANCH_DIM // ATTN_HEADS
Q_PER_KV = ATTN_HEADS // KV_HEADS
IDX_HEADS = 8
IDX_DIM = 64
TOPK_MAX = 256
N_BUCKETS = 32
MAX_DISTANCE = 128
POOL_WINDOWS = (2, 4, 8, 16)
POOL_GROUPS = 4
POOL_GROUP_DIM = BRANCH_DIM // POOL_GROUPS
POOL_STATE = 15
EPS = 1e-6
NEG = -1e30

LANES = 128
VMEM_LIMIT = 56 * 1024 * 1024

_ORIG = {}
_off = 0
for _name, _size in (("z", SSD_INNER), ("xbc", CONV_DIM), ("dt", SSD_HEADS), ("q", BRANCH_DIM),
                     ("k", KV_HEADS * HEAD_DIM), ("v", KV_HEADS * HEAD_DIM), ("ag", BRANCH_DIM),
                     ("iq", IDX_HEADS * IDX_DIM), ("ik", IDX_DIM), ("iw", IDX_HEADS),
                     ("u", BRANCH_DIM), ("pg", BRANCH_DIM), ("mg", N_BRANCH * D_MODEL)):
    _ORIG[_name] = (_off, _size)
    _off += _size
IN_DIM = _off

_NEW_ORDER = ("mg", "z", "q", "ag", "u", "pg", "xbc", "iq", "k", "v", "ik", "dt", "iw")
COL = {}
_off = 0
for _name in _NEW_ORDER:
    COL[_name] = _off
    _off += -(-_ORIG[_name][1] // LANES) * LANES
PROJ_DIM = -(-_off // 2048) * 2048
COL["xs"] = COL["xbc"]
COL["bm"] = COL["xbc"] + SSD_INNER
COL["cm"] = COL["bm"] + SSD_GROUPS * SSD_STATE


def _sortable_const(v):
    i = int(np.float32(v).view(np.int32))
    return i ^ ((i >> 31) & 0x7FFFFFFF)


NEG_KEY = _sortable_const(NEG)
INT_MIN = -2 ** 31


def _cparams(sem):
    return pltpu.CompilerParams(dimension_semantics=sem, vmem_limit_bytes=VMEM_LIMIT)


def _silu(x):
    return x * jax.nn.sigmoid(x)


def _ada_kernel(c_ref, w_ref, b_ref, o_ref):
    c = c_ref[...]
    o_ref[...] = jnp.dot(_silu(c).astype(BF16), w_ref[...].astype(BF16),
                         preferred_element_type=F32) + b_ref[...]


def ada_mod(c_all, w_ada, b_ada):
    nb, d = c_all.shape
    n = w_ada.shape[-1]
    tn = 512
    return pl.pallas_call(
        _ada_kernel,
        grid=(DEPTH, n // tn),
        in_specs=[pl.BlockSpec((nb, d), lambda l, j: (0, 0)),
                  pl.BlockSpec((None, d, tn), lambda l, j: (l, 0, j)),
                  pl.BlockSpec((None, 1, tn), lambda l, j: (l, 0, j))],
        out_specs=pl.BlockSpec((None, nb, tn), lambda l, j: (l, 0, j)),
        out_shape=jax.ShapeDtypeStruct((DEPTH, nb, n), F32),
        compiler_params=_cparams(("arbitrary", "arbitrary")),
        name="ada",
    )(c_all, w_ada, b_ada.reshape(DEPTH, 1, n))


def _inproj_kernel(x_ref, sc_ref, sh_ref, nw_ref, w_ref, o_ref, h_ref):
    @pl.when(pl.program_id(1) == 0)
    def _():
        x = x_ref[...]
        ms = jnp.mean(x * x, axis=-1, keepdims=True)
        y = x * lax.rsqrt(ms + EPS) * nw_ref[...]
        h_ref[...] = (y * (1.0 + sc_ref[...]) + sh_ref[...]).astype(BF16)

    o_ref[...] = jnp.dot(h_ref[...], w_ref[...], preferred_element_type=F32)


def _mod_spec(mod, tm, tn, rows_per_mod, col_of_j):
    r = mod.shape[1]
    if r == 1:
        tiles = rows_per_mod // tm
        return pl.BlockSpec((None, 1, tn), lambda i, j: (i // tiles, 0, col_of_j(j)))
    return pl.BlockSpec((None, r, tn), lambda i, j: (i, 0, col_of_j(j)))


def inproj(x, scale, shift, norm_w, w, tm, rows_per_mod):
    t, d = x.shape
    n = w.shape[1]
    tn = 1024
    zero = lambda j: 0
    return pl.pallas_call(
        _inproj_kernel,
        grid=(t // tm, n // tn),
        in_specs=[pl.BlockSpec((tm, d), lambda i, j: (i, 0)),
                  _mod_spec(scale, tm, d, rows_per_mod, zero),
                  _mod_spec(shift, tm, d, rows_per_mod, zero),
                  pl.BlockSpec((1, d), lambda i, j: (0, 0)),
                  pl.BlockSpec((d, tn), lambda i, j: (0, j))],
        out_specs=pl.BlockSpec((tm, tn), lambda i, j: (i, j)),
        out_shape=jax.ShapeDtypeStruct((t, n), F32),
        scratch_shapes=[pltpu.VMEM((tm, d), BF16)],
        compiler_params=_cparams(("arbitrary", "arbitrary")),
        name="inproj",
    )(x, scale, shift, norm_w.reshape(1, d), w)


def _prep_kernel(q_ref, k_ref, iq_ref, qw_ref, kw_ref, qh_ref, ko_ref, iqh_ref):
    def head_norm(xs, w):
        ms = jnp.mean(xs * xs, axis=-1, keepdims=True)
        return xs * lax.rsqrt(ms + EPS) * w

    q = q_ref[...]
    for h in range(ATTN_HEADS):
        qn = head_norm(q[:, h * HEAD_DIM:(h + 1) * HEAD_DIM], qw_ref[...])
        qh_ref[h] = (qn * HEAD_DIM ** -0.5).astype(BF16)
    k = k_ref[...]
    for g in range(KV_HEADS):
        ko_ref[:, g * HEAD_DIM:(g + 1) * HEAD_DIM] = head_norm(k[:, g * HEAD_DIM:(g + 1) * HEAD_DIM], kw_ref[...])
    iq = iq_ref[...]
    for h in range(IDX_HEADS):
        iqh_ref[h] = (iq[:, h * IDX_DIM:(h + 1) * IDX_DIM] * IDX_DIM ** -0.5).astype(BF16)


def prep(proj, q_norm_w, k_norm_w, tm):
    t = proj.shape[0]
    kw = KV_HEADS * HEAD_DIM
    iqw = IDX_HEADS * IDX_DIM
    return pl.pallas_call(
        _prep_kernel,
        grid=(t // tm,),
        in_specs=[pl.BlockSpec((tm, BRANCH_DIM), lambda i: (i, COL["q"] // BRANCH_DIM)),
                  pl.BlockSpec((tm, kw), lambda i: (i, COL["k"] // kw)),
                  pl.BlockSpec((tm, iqw), lambda i: (i, COL["iq"] // iqw)),
                  pl.BlockSpec((1, HEAD_DIM), lambda i: (0, 0)),
                  pl.BlockSpec((1, HEAD_DIM), lambda i: (0, 0))],
        out_specs=[pl.BlockSpec((ATTN_HEADS, tm, HEAD_DIM), lambda i: (0, i, 0)),
                   pl.BlockSpec((tm, kw), lambda i: (i, 0)),
                   pl.BlockSpec((IDX_HEADS, tm, IDX_DIM), lambda i: (0, i, 0))],
        out_shape=[jax.ShapeDtypeStruct((ATTN_HEADS, t, HEAD_DIM), BF16),
                   jax.ShapeDtypeStruct((t, kw), F32),
                   jax.ShapeDtypeStruct((IDX_HEADS, t, IDX_DIM), BF16)],
        compiler_params=_cparams(("arbitrary",)),
        name="prep",
    )(proj, proj, proj, q_norm_w.reshape(1, HEAD_DIM), k_norm_w.reshape(1, HEAD_DIM))


TKS = 256
NEAR_W = 2 * TKS
FAR_BUCKET = N_BUCKETS // 2 - 1
J_ALL = 2 ** 30


def _dsa_kernel(tab_ref, qh_ref, iqh_ref, iw_ref, ikT_ref, kT_ref, v_ref, bk_ref, ag_ref,
                o_ref, keys_ref, nb_ref, acc_ref, m_ref, l_ref, j_ref, *,
                tq, ta, tk, start, n_keys, topk):
    b = pl.program_id(0)
    i = pl.program_id(1)
    kk = pl.program_id(2)
    nk = pl.num_programs(2)
    q0 = start + i * tq
    vis_end = ((q0 + tq - 1) // CHUNK + 1) * CHUNK
    n_cols = jnp.minimum(vis_end, n_keys)
    n_a = (n_cols + ta - 1) // ta
    n_sub = (n_cols + TKS - 1) // TKS
    extra = jnp.maximum(n_keys - n_a * ta, 0).astype(F32)
    sub_per_tile = tk // TKS

    def vis_mask(t, width):
        kpos = t * width + lax.broadcasted_iota(I32, (tq, width), 1)
        qpos = q0 + lax.broadcasted_iota(I32, (tq, width), 0)
        vis = (kpos // CHUNK) <= (qpos // CHUNK)
        return vis, kpos

    @pl.when((b == 0) & (i == 0) & (kk == 0))
    def _():
        bk = bk_ref[...]

        def per_head(h, _):
            def per_bucket(n, val):
                return jnp.where(bk == n, tab_ref[n, h], val)
            val = lax.fori_loop(0, N_BUCKETS, per_bucket, jnp.zeros((tq, NEAR_W), F32))
            nb_ref[h] = val - tab_ref[FAR_BUCKET, h]
            return 0
        lax.fori_loop(0, ATTN_HEADS, per_head, 0)

    @pl.when(kk == 0)
    def _():
        iw = iw_ref[...] * IDX_HEADS ** -0.5

        def score_tile(t, _):
            c0 = pl.multiple_of(t * ta, ta)
            ikt = ikT_ref[:, pl.ds(c0, ta)]
            acc = jnp.zeros((tq, ta), F32)
            for h in range(IDX_HEADS):
                s = jnp.dot(iqh_ref[h], ikt, preferred_element_type=F32)
                acc = acc + iw[:, h:h + 1] * jnp.maximum(s, 0.0)
            vis, kpos = vis_mask(t, ta)
            acc = jnp.where(vis, acc, NEG)
            bits = pltpu.bitcast(acc, I32)
            key = bits ^ ((bits >> 31) & 0x7FFFFFFF)
            key = jnp.where(kpos < n_keys, key, INT_MIN)
            keys_ref[:, pl.ds(c0, ta)] = key
            return 0
        lax.fori_loop(0, n_a, score_tile, 0)

        def count(pred):
            def body(t, cnt):
                c0 = pl.multiple_of(t * ta, ta)
                blk = keys_ref[:, pl.ds(c0, ta)]
                c = jnp.where(pred(blk, t), 1.0, 0.0)
                for jj in range(ta // LANES):
                    cnt = cnt + c[:, jj * LANES:(jj + 1) * LANES]
                return cnt
            cnt = lax.fori_loop(0, n_a, body, jnp.zeros((tq, LANES), F32))
            return jnp.sum(cnt, axis=1, keepdims=True)

        def bit_step(it, prefix_u):
            bit = jnp.left_shift(jnp.int32(1), 31 - it)
            cand_u = prefix_u | bit
            cand_s = cand_u ^ INT_MIN
            cnt = count(lambda blk, t: blk >= cand_s) + jnp.where(NEG_KEY >= cand_s, extra, 0.0)
            return jnp.where(cnt >= topk, cand_u, prefix_u)
        prefix_u = lax.fori_loop(0, 32, bit_step, jnp.zeros((tq, 1), I32))
        thr = prefix_u ^ INT_MIN

        cnt_gt = count(lambda blk, t: blk > thr) + jnp.where(NEG_KEY > thr, extra, 0.0)
        cnt_eq = count(lambda blk, t: blk == thr)
        need = topk - cnt_gt
        j_ref[...] = jnp.full((tq, LANES), J_ALL, I32)

        @pl.when(jnp.max(cnt_eq - need) > 0.0)
        def _():
            def idx_step(it, jmax):
                cand = jmax | jnp.left_shift(jnp.int32(1), 14 - it)

                def pred(blk, t):
                    _, kpos = vis_mask(t, ta)
                    return (blk == thr) & (kpos < cand)
                g = count(pred)
                return jnp.where(g < need, cand, jmax)
            jmax = lax.fori_loop(0, 15, idx_step, jnp.zeros((tq, 1), I32))
            j_ref[...] = jnp.broadcast_to(jmax, (tq, LANES))

        jmax = j_ref[:, 0:1]

        def mask_tile(t, _):
            c0 = pl.multiple_of(t * ta, ta)
            blk = keys_ref[:, pl.ds(c0, ta)]
            vis, kpos = vis_mask(t, ta)
            sel = (blk > thr) | ((blk == thr) & (kpos <= jmax))
            ok = sel & vis & (kpos < n_keys)
            madd = jnp.where(ok, 0.0, NEG).astype(F32)
            keys_ref[:, pl.ds(c0, ta)] = pltpu.bitcast(madd, I32)
            return 0
        lax.fori_loop(0, n_a, mask_tile, 0)

        m_ref[...] = jnp.full(m_ref.shape, -jnp.inf, F32)
        l_ref[...] = jnp.zeros(l_ref.shape, F32)
        acc_ref[...] = jnp.zeros(acc_ref.shape, F32)

    def attend(jl, boff):
        lo = pl.multiple_of(jl * TKS, TKS)
        c0 = pl.multiple_of(kk * tk + lo, TKS)
        madd = pltpu.bitcast(keys_ref[:, pl.ds(c0, TKS)], F32)

        def group(g, _):
            kt = kT_ref[g, :, pl.ds(lo, TKS)]
            vv = v_ref[g, pl.ds(lo, TKS), :]
            for r in range(Q_PER_KV):
                h = g * Q_PER_KV + r
                s = jnp.dot(qh_ref[h], kt, preferred_element_type=F32) + madd
                if boff is not None:
                    s = s + nb_ref[h, :, boff:boff + TKS]
                m_prev = m_ref[h]
                m_new = jnp.maximum(m_prev, jnp.max(s, axis=1, keepdims=True))
                alpha = jnp.exp(m_prev - m_new)
                p = jnp.exp(s - m_new[:, 0:1])
                l_ref[h] = alpha * l_ref[h] + jnp.sum(p, axis=1, keepdims=True)
                acc_ref[h] = alpha[:, 0:HEAD_DIM] * acc_ref[h] + jnp.dot(
                    p.astype(BF16), vv, preferred_element_type=F32)
                m_ref[h] = m_new
            return 0
        lax.fori_loop(0, KV_HEADS, group, 0)

    n_here = jnp.clip(n_sub - kk * sub_per_tile, 0, sub_per_tile)

    def sub_body(jl, _):
        u = kk * sub_per_tile + jl

        @pl.when(u < n_sub - 2)
        def _():
            attend(jl, None)

        @pl.when(u == n_sub - 2)
        def _():
            attend(jl, 0)

        @pl.when(u == n_sub - 1)
        def _():
            attend(jl, TKS)
        return 0
    lax.fori_loop(0, n_here, sub_body, 0)

    @pl.when(kk == nk - 1)
    def _():
        ag = ag_ref[...]
        for h in range(ATTN_HEADS):
            o = acc_ref[h] / l_ref[h][:, 0:HEAD_DIM]
            sl = slice(h * HEAD_DIM, (h + 1) * HEAD_DIM)
            o_ref[:, sl] = (o * _silu(ag[:, sl])).astype(o_ref.dtype)


def dsa(qh, iqh, proj, ikT, kT, v4, bk, rel_bias, *, nbatch, tq, ta, tk, start, n_keys, topk):
    t = qh.shape[1]
    s_pad = ikT.shape[-1]
    nq = t // (nbatch * tq)
    nk = s_pad // tk

    def kt_idx(b, i, kk):
        q0 = start + i * tq
        n_cols = jnp.minimum(((q0 + tq - 1) // CHUNK + 1) * CHUNK, n_keys)
        return jnp.minimum(kk, (n_cols + tk - 1) // tk - 1)

    kern = functools.partial(_dsa_kernel, tq=tq, ta=ta, tk=tk, start=start, n_keys=n_keys, topk=topk)
    return pl.pallas_call(
        kern,
        grid=(nbatch, nq, nk),
        in_specs=[pl.BlockSpec(memory_space=pltpu.SMEM),
                  pl.BlockSpec((ATTN_HEADS, tq, HEAD_DIM), lambda b, i, kk: (0, b * nq + i, 0)),
                  pl.BlockSpec((IDX_HEADS, tq, IDX_DIM), lambda b, i, kk: (0, b * nq + i, 0)),
                  pl.BlockSpec((tq, LANES), lambda b, i, kk: (b * nq + i, COL["iw"] // LANES)),
                  pl.BlockSpec((None, IDX_DIM, s_pad), lambda b, i, kk: (b, 0, 0)),
                  pl.BlockSpec((None, KV_HEADS, HEAD_DIM, tk), lambda b, i, kk: (b, 0, 0, kt_idx(b, i, kk))),
                  pl.BlockSpec((None, KV_HEADS, tk, HEAD_DIM), lambda b, i, kk: (b, 0, kt_idx(b, i, kk), 0)),
                  pl.BlockSpec((tq, NEAR_W), lambda b, i, kk: (0, 0)),
                  pl.BlockSpec((tq, BRANCH_DIM), lambda b, i, kk: (b * nq + i, COL["ag"] // BRANCH_DIM))],
        out_specs=pl.BlockSpec((tq, BRANCH_DIM), lambda b, i, kk: (b * nq + i, 0)),
        out_shape=jax.ShapeDtypeStruct((t, BRANCH_DIM), BF16),
        scratch_shapes=[pltpu.VMEM((tq, s_pad), I32),
                        pltpu.VMEM((ATTN_HEADS, tq, NEAR_W), F32),
                        pltpu.VMEM((ATTN_HEADS, tq, HEAD_DIM), F32),
                        pltpu.VMEM((ATTN_HEADS, tq, LANES), F32),
                        pltpu.VMEM((ATTN_HEADS, tq, LANES), F32),
                        pltpu.VMEM((tq, LANES), I32)],
        compiler_params=_cparams(("arbitrary", "arbitrary", "arbitrary")),
        name="dsa",
    )(rel_bias, qh, iqh, proj, ikT, kT, v4, bk, proj)


AUG_K = 128
V_ROWS = 80
CNT_ROWS = 64
FAST_BOUND = 30.0
GM_ROWS = 256


def _dsat_kernel(tab_ref, kmax_ref, qT_ref, iqT_ref, iw_ref, ik_ref, k_ref, vT_ref, bk_ref, ag_ref,
                 o_ref, keys_ref, nb_ref, acc_ref, m_ref, qa_ref, j_ref, gm_ref, bmax_ref, fast_ref, *,
                 tq, ta, tk, start, n_keys, topk):
    b = pl.program_id(0)
    i = pl.program_id(1)
    kk = pl.program_id(2)
    nk = pl.num_programs(2)
    q0 = start + i * tq
    vis_end = ((q0 + tq - 1) // CHUNK + 1) * CHUNK
    n_cols = jnp.minimum(vis_end, n_keys)
    n_a = (n_cols + ta - 1) // ta
    n_sub = (n_cols + TKS - 1) // TKS
    extra = jnp.maximum(n_keys - n_a * ta, 0).astype(F32)
    sub_per_tile = tk // TKS

    def key_pos(t):
        return t * ta + lax.broadcasted_iota(I32, (ta, tq), 0)

    def vis_of(kpos):
        qpos = q0 + lax.broadcasted_iota(I32, (ta, tq), 1)
        return (kpos >> 6) <= (qpos >> 6)

    def colsum(x):
        return jnp.sum(x.reshape(ta // CNT_ROWS, CNT_ROWS, tq), axis=0)

    @pl.when((b == 0) & (i == 0) & (kk == 0))
    def _():
        bk = bk_ref[...]

        def per_head(h, bmax):
            def per_bucket(n, val):
                return jnp.where(bk == n, tab_ref[n, h], val)
            val = lax.fori_loop(0, N_BUCKETS, per_bucket, jnp.zeros((NEAR_W, tq), F32))
            val = val - tab_ref[FAR_BUCKET, h]
            nb_ref[h] = val
            return jnp.maximum(bmax, jnp.max(val))
        bmax_ref[0] = lax.fori_loop(0, ATTN_HEADS, per_head, jnp.float32(0.0))

    @pl.when(kk == 0)
    def _():
        rows = lax.broadcasted_iota(I32, (AUG_K, tq), 0)
        worst = jnp.zeros((1, tq), F32)
        for h in range(ATTN_HEADS):
            q = qT_ref[h].astype(F32)
            nrm = jnp.sqrt(jnp.sum(q * q, axis=0, keepdims=True))
            bound = nrm * kmax_ref[b, h // Q_PER_KV] + bmax_ref[0]
            worst = jnp.maximum(worst, bound)
            qpad = jnp.concatenate([q, jnp.zeros((AUG_K - HEAD_DIM, tq), F32)], axis=0)
            qa_ref[h] = jnp.where(rows == HEAD_DIM, -bound, qpad).astype(BF16)
        fast_ref[0] = (jnp.max(worst) <= FAST_BOUND).astype(I32)

        iw = iw_ref[...] * IDX_HEADS ** -0.5

        def score_tile(t, edge):
            c0 = pl.multiple_of(t * ta, ta)
            ikt = ik_ref[pl.ds(c0, ta), :]
            acc = jnp.zeros((ta, tq), F32)
            for h in range(IDX_HEADS):
                s = jnp.dot(ikt, iqT_ref[h], preferred_element_type=F32)
                acc = acc + iw[h:h + 1, :] * jnp.maximum(s, 0.0)
            if edge:
                kpos = key_pos(t)
                acc = jnp.where(vis_of(kpos), acc, NEG)
            bits = pltpu.bitcast(acc, I32)
            key = bits ^ ((bits >> 31) & 0x7FFFFFFF)
            if edge:
                key = jnp.where(kpos < n_keys, key, INT_MIN)
                acc = jnp.where(kpos < n_keys, acc, -jnp.inf)
            keys_ref[pl.ds(c0, ta), :] = key
            gm_ref[...] = jnp.maximum(gm_ref[...], jnp.max(acc.reshape(ta // GM_ROWS, GM_ROWS, tq), axis=0))
        gm_ref[...] = jnp.full((GM_ROWS, tq), -jnp.inf, F32)
        lax.fori_loop(0, n_a - 1, lambda t, _: (score_tile(t, False), 0)[1], 0)
        score_tile(n_a - 1, True)

        def count(pred):
            def body(t, cnt):
                c0 = pl.multiple_of(t * ta, ta)
                blk = keys_ref[pl.ds(c0, ta), :]
                return cnt + colsum(jnp.where(pred(blk, t), 1.0, 0.0))
            cnt = lax.fori_loop(0, n_a, body, jnp.zeros((CNT_ROWS, tq), F32))
            return jnp.sum(cnt, axis=0, keepdims=True)

        def count_ge(p):
            return count(lambda blk, t: blk >= p) + jnp.where(NEG_KEY >= p, extra, 0.0)

        def key_of(f):
            bits = pltpu.bitcast(f, I32)
            return bits ^ ((bits >> 31) & 0x7FFFFFFF)

        gm = gm_ref[...]
        lo0 = key_of(jnp.min(gm, axis=0, keepdims=True))
        hi0 = key_of(jnp.max(gm, axis=0, keepdims=True)) + 1
        c_lo0 = count_ge(lo0)

        def active_of(lo, hi, c_lo):
            return (c_lo != topk) & (lo + 1 < hi)

        def search_cond(st):
            return st[5] > 0.0

        def search_body(st):
            lo, hi, c_lo, c_hi, it, _ = st
            mid = (lo >> 1) + (hi >> 1) + (lo & hi & 1)
            zero_pivot = (it < 2) & (lo < it) & (it < hi)
            p = jnp.where(zero_pivot, it, mid)
            p = jnp.minimum(jnp.maximum(p, lo + 1), hi - 1)
            c = count_ge(p)
            act = active_of(lo, hi, c_lo)
            up = act & (c >= topk)
            dn = act & (c < topk)
            lo, c_lo = jnp.where(up, p, lo), jnp.where(up, c, c_lo)
            hi, c_hi = jnp.where(dn, p, hi), jnp.where(dn, c, c_hi)
            left = jnp.max(jnp.where(active_of(lo, hi, c_lo), 1.0, 0.0))
            return lo, hi, c_lo, c_hi, it + 1, left

        st0 = (lo0, hi0, c_lo0, jnp.zeros((1, tq), F32), jnp.int32(0),
               jnp.max(jnp.where(active_of(lo0, hi0, c_lo0), 1.0, 0.0)))
        thr, _, c_thr, _, _, _ = lax.while_loop(search_cond, search_body, st0)
        j_ref[...] = jnp.full((8, tq), J_ALL, I32)

        @pl.when(jnp.max(c_thr) > topk)
        def _():
            cnt_gt = count(lambda blk, t: blk > thr) + jnp.where(NEG_KEY > thr, extra, 0.0)
            need = topk - cnt_gt

            def idx_step(it, jmax):
                cand = jmax | jnp.left_shift(jnp.int32(1), 14 - it)
                g = count(lambda blk, t: (blk == thr) & (key_pos(t) < cand))
                return jnp.where(g < need, cand, jmax)
            jmax = lax.fori_loop(0, 15, idx_step, jnp.zeros((1, tq), I32))
            j_ref[...] = jnp.broadcast_to(jmax, (8, tq))

        jmax = j_ref[0:1, :]

        def mask_tile(t, edge):
            c0 = pl.multiple_of(t * ta, ta)
            blk = keys_ref[pl.ds(c0, ta), :]
            kpos = key_pos(t)
            ok = (blk > thr) | ((blk == thr) & (kpos <= jmax))
            if edge:
                ok = ok & vis_of(kpos) & (kpos < n_keys)
            keys_ref[pl.ds(c0, ta), :] = pltpu.bitcast(jnp.where(ok, 1.0, 0.0).astype(F32), I32)
        lax.fori_loop(0, n_a - 1, lambda t, _: (mask_tile(t, False), 0)[1], 0)
        mask_tile(n_a - 1, True)

        m_ref[...] = jnp.full(m_ref.shape, -jnp.inf, F32)
        acc_ref[...] = jnp.zeros(acc_ref.shape, F32)

    def attend_fast(jl, boff):
        lo = pl.multiple_of(jl * TKS, TKS)
        c0 = pl.multiple_of(kk * tk + lo, TKS)
        mask = pltpu.bitcast(keys_ref[pl.ds(c0, TKS), :], F32)

        def logits(g):
            kt = k_ref[g, pl.ds(lo, TKS), :]
            return [jnp.dot(kt, qa_ref[g * Q_PER_KV + r], preferred_element_type=F32)
                    for r in range(Q_PER_KV)]

        s_cur = logits(0)
        for g in range(KV_HEADS):
            s_next = logits(g + 1) if g + 1 < KV_HEADS else None
            vt = vT_ref[g, :, pl.ds(lo, TKS)]
            for r in range(Q_PER_KV):
                h = g * Q_PER_KV + r
                s = s_cur[r]
                if boff is not None:
                    s = nb_ref[h, boff:boff + TKS, :] + s
                p = (jnp.exp(s) * mask).astype(BF16)
                acc_ref[h] = acc_ref[h] + jnp.dot(vt, p, preferred_element_type=F32)
            s_cur = s_next

    def attend_slow(jl, boff):
        lo = pl.multiple_of(jl * TKS, TKS)
        c0 = pl.multiple_of(kk * tk + lo, TKS)
        madd = (pltpu.bitcast(keys_ref[pl.ds(c0, TKS), :], F32) - 1.0) * (-NEG)

        def group(g, _):
            kt = k_ref[g, pl.ds(lo, TKS), :]
            vt = vT_ref[g, :, pl.ds(lo, TKS)]
            for r in range(Q_PER_KV):
                h = g * Q_PER_KV + r
                s = jnp.dot(kt, qa_ref[h], preferred_element_type=F32) + madd
                if boff is not None:
                    s = s + nb_ref[h, boff:boff + TKS, :]
                m_prev = m_ref[h]
                m_new = jnp.maximum(m_prev, jnp.max(s, axis=0, keepdims=True))
                alpha = jnp.exp(m_prev - m_new)
                p = jnp.exp(s - m_new[0:1, :])
                acc_ref[h] = alpha[0:1, :] * acc_ref[h] + jnp.dot(
                    vt, p.astype(BF16), preferred_element_type=F32)
                m_ref[h] = m_new
            return 0
        lax.fori_loop(0, KV_HEADS, group, 0)

    n_here = jnp.clip(n_sub - kk * sub_per_tile, 0, sub_per_tile)
    fast = fast_ref[0] == 1

    def sub_loop(attend):
        def sub_body(jl, _):
            u = kk * sub_per_tile + jl

            @pl.when(u < n_sub - 2)
            def _():
                attend(jl, None)

            @pl.when(u == n_sub - 2)
            def _():
                attend(jl, 0)

            @pl.when(u == n_sub - 1)
            def _():
                attend(jl, TKS)
            return 0
        lax.fori_loop(0, n_here, sub_body, 0)

    @pl.when(fast)
    def _():
        sub_loop(attend_fast)

    @pl.when(jnp.logical_not(fast))
    def _():
        sub_loop(attend_slow)

    @pl.when(kk == nk - 1)
    def _():
        outs = []
        for h in range(ATTN_HEADS):
            a = acc_ref[h]
            outs.append(a[0:HEAD_DIM, :] / a[HEAD_DIM:HEAD_DIM + 1, :])
        o = jnp.concatenate(outs, axis=0).T
        o_ref[...] = (o * _silu(ag_ref[...])).astype(o_ref.dtype)


def dsa_t(qT, iqT, iwT, proj, ik, k4, vT, bkT, rel_bias, *, nbatch, tq, ta, tk, start, n_keys, topk):
    t = qT.shape[2]
    s_pad = ik.shape[1]
    nq = t // (nbatch * tq)
    nk = s_pad // tk
    assert tq % LANES == 0 and ta % tq == 0 and tk % TKS == 0 and s_pad % ta == 0 and ta % CNT_ROWS == 0
    assert ta % GM_ROWS == 0 and topk <= GM_ROWS <= start + tq

    kf = k4.astype(F32)
    kmax = jnp.sqrt(jnp.max(jnp.sum(kf * kf, axis=-1), axis=-1))
    real = (jnp.arange(s_pad) < n_keys).astype(BF16)
    k_aug = jnp.concatenate([k4, jnp.broadcast_to(real[None, None, :, None], k4.shape[:3] + (1,)),
                             jnp.zeros(k4.shape[:3] + (AUG_K - HEAD_DIM - 1,), BF16)], axis=-1)
    v_aug = jnp.concatenate([vT, jnp.broadcast_to(real[None, None, None, :], vT.shape[:2] + (1, s_pad)),
                             jnp.zeros(vT.shape[:2] + (V_ROWS - HEAD_DIM - 1, s_pad), BF16)], axis=2)

    def kt_idx(b, i, kk):
        q0 = start + i * tq
        n_cols = jnp.minimum(((q0 + tq - 1) // CHUNK + 1) * CHUNK, n_keys)
        return jnp.minimum(kk, (n_cols + tk - 1) // tk - 1)

    kern = functools.partial(_dsat_kernel, tq=tq, ta=ta, tk=tk, start=start, n_keys=n_keys, topk=topk)
    smem = pl.BlockSpec(memory_space=pltpu.SMEM)
    return pl.pallas_call(
        kern,
        grid=(nbatch, nq, nk),
        in_specs=[smem, smem,
                  pl.BlockSpec((ATTN_HEADS, HEAD_DIM, tq), lambda b, i, kk: (0, 0, b * nq + i)),
                  pl.BlockSpec((IDX_HEADS, IDX_DIM, tq), lambda b, i, kk: (0, 0, b * nq + i)),
                  pl.BlockSpec((IDX_HEADS, tq), lambda b, i, kk: (0, b * nq + i)),
                  pl.BlockSpec((None, s_pad, IDX_DIM), lambda b, i, kk: (b, 0, 0)),
                  pl.BlockSpec((None, KV_HEADS, tk, AUG_K), lambda b, i, kk: (b, 0, kt_idx(b, i, kk), 0)),
                  pl.BlockSpec((None, KV_HEADS, V_ROWS, tk), lambda b, i, kk: (b, 0, 0, kt_idx(b, i, kk))),
                  pl.BlockSpec((NEAR_W, tq), lambda b, i, kk: (0, 0)),
                  pl.BlockSpec((tq, BRANCH_DIM), lambda b, i, kk: (b * nq + i, COL["ag"] // BRANCH_DIM))],
        out_specs=pl.BlockSpec((tq, BRANCH_DIM), lambda b, i, kk: (b * nq + i, 0)),
        out_shape=jax.ShapeDtypeStruct((t, BRANCH_DIM), BF16),
        scratch_shapes=[pltpu.VMEM((s_pad, tq), I32),
                        pltpu.VMEM((ATTN_HEADS, NEAR_W, tq), F32),
                        pltpu.VMEM((ATTN_HEADS, V_ROWS, tq), F32),
                        pltpu.VMEM((ATTN_HEADS, 8, tq), F32),
                        pltpu.VMEM((ATTN_HEADS, AUG_K, tq), BF16),
                        pltpu.VMEM((8, tq), I32),
                        pltpu.VMEM((GM_ROWS, tq), F32),
                        pltpu.SMEM((1,), F32),
                        pltpu.SMEM((1,), I32)],
        compiler_params=_cparams(("arbitrary", "arbitrary", "arbitrary")),
        name="dsa_t",
    )(rel_bias, kmax, qT, iqT, iwT, ik, k_aug, v_aug, bkT, proj)


def t5_bucket(rel):
    half = N_BUCKETS // 2
    max_exact = half // 2
    ret = jnp.where(rel > 0, half, 0)
    n = jnp.abs(rel)
    nf = jnp.maximum(n, max_exact).astype(jnp.float32)
    large = max_exact + (jnp.log(nf / max_exact) / math.log(MAX_DISTANCE / max_exact)
                         * (half - max_exact)).astype(jnp.int32)
    large = jnp.minimum(large, half - 1)
    return ret + jnp.where(n < max_exact, n, large)


def near_buckets(tq, q0, n_keys):
    vis_end = ((q0 + tq - 1) // CHUNK + 1) * CHUNK
    end = -(-min(vis_end, n_keys) // TKS) * TKS
    assert end - NEAR_W <= q0 - MAX_DISTANCE + 1, "near window must cover every non-saturated offset"
    kpos = end - NEAR_W + jnp.arange(NEAR_W, dtype=jnp.int32)[None, :]
    qpos = q0 + jnp.arange(tq, dtype=jnp.int32)[:, None]
    return t5_bucket(kpos - qpos)


def _ssd_kernel(xs_ref, bm_ref, cm_ref, dt_ref, z_ref, hist_ref, st0_ref, cw_ref, cb_ref,
                dtb_ref, alog_ref, dsk_ref, nw_ref, exp_ref, y_ref, st_ref,
                ext_ref, state_ref, *, q, valid):
    c = pl.program_id(1)
    nc = pl.num_programs(1)
    hp = SSD_INNER
    gw = hp // SSD_GROUPS
    hi = lax.Precision.HIGHEST

    @pl.when(c == 0)
    def _():
        ext_ref[0:8, :] = hist_ref[...]
        state_ref[...] = st0_ref[...]

    ext_ref[8:8 + q, 0:hp] = xs_ref[...]
    ext_ref[8:8 + q, hp:hp + 256] = bm_ref[...]
    ext_ref[8:8 + q, hp + 256:hp + 512] = cm_ref[...]
    conv = jnp.zeros((q, CONV_DIM), F32) + cb_ref[...]
    for j in range(CONV_WIDTH):
        conv = conv + ext_ref[5 + j:5 + j + q, :] * cw_ref[j:j + 1, :]
    ext_ref[0:8, :] = ext_ref[q:q + 8, :]
    conv = _silu(conv)
    xs = conv[:, 0:hp]
    bmat = conv[:, hp:hp + 256]
    cmat = conv[:, hp + 256:hp + 512]

    xdt = dt_ref[...] + dtb_ref[...]
    dt = jnp.maximum(xdt, 0.0) + jnp.log1p(jnp.exp(-jnp.abs(xdt)))
    if valid < q:
        rows = lax.broadcasted_iota(I32, (q, LANES), 0)
        dt = jnp.where(rows < valid, dt, 0.0)
    adt = dt * (-jnp.exp(alog_ref[...]))
    rr = lax.broadcasted_iota(I32, (q, q), 0)
    cc = lax.broadcasted_iota(I32, (q, q), 1)
    tri = rr >= cc
    acum = jnp.dot(tri.astype(F32), adt, precision=hi, preferred_element_type=F32)
    acum_t = acum.T
    dt_t = dt.T
    alast = acum[q - 1:q, :]

    expand = exp_ref[...]
    e_acum = jnp.dot(jnp.exp(acum), expand, precision=hi, preferred_element_type=F32)
    e_tail = jnp.dot(jnp.exp(alast - acum) * dt, expand, precision=hi, preferred_element_type=F32)
    e_last = e_acum[q - 1:q, :]

    xw = (xs * e_tail).astype(BF16)
    xb = xs.astype(BF16)
    y_parts = []
    new_state = []
    for g in range(SSD_GROUPS):
        bg = bmat[:, g * SSD_STATE:(g + 1) * SSD_STATE]
        cg = cmat[:, g * SSD_STATE:(g + 1) * SSD_STATE].astype(BF16)
        bg_t = bg.T.astype(BF16)
        cb = jnp.dot(cg, bg_t, preferred_element_type=F32)
        st_g = state_ref[:, g * gw:(g + 1) * gw]
        y_off = jnp.dot(cg, st_g.astype(BF16), preferred_element_type=F32)
        new_state.append(jnp.dot(bg_t, xw[:, g * gw:(g + 1) * gw], preferred_element_type=F32))
        heads = []
        for r in range(SSD_HEADS // SSD_GROUPS):
            h = g * (SSD_HEADS // SSD_GROUPS) + r
            seg = acum[:, h:h + 1] - acum_t[h:h + 1, :]
            decay = jnp.where(tri, jnp.exp(jnp.where(tri, seg, 0.0)), 0.0)
            wmat = (cb * decay * dt_t[h:h + 1, :]).astype(BF16)
            heads.append(jnp.dot(wmat, xb[:, h * SSD_HEAD_DIM:(h + 1) * SSD_HEAD_DIM],
                                 preferred_element_type=F32))
        y_parts.append(jnp.concatenate(heads, axis=1) + y_off * e_acum[:, g * gw:(g + 1) * gw])
    y = jnp.concatenate(y_parts, axis=1)
    state_ref[...] = state_ref[...] * e_last + jnp.concatenate(new_state, axis=1)

    y = (y + dsk_ref[...] * xs) * _silu(z_ref[...])
    ms = jnp.mean(y * y, axis=-1, keepdims=True)
    y_ref[...] = (y * lax.rsqrt(ms + EPS) * nw_ref[...]).astype(y_ref.dtype)

    @pl.when(c == nc - 1)
    def _():
        st_ref[...] = state_ref[...]


def ssd(proj, hist8, state_t, conv_w, conv_b, dt_bias, a_log, d_skip, ssd_norm_w, *, nbatch, q, valid):
    t = proj.shape[0]
    nc = t // (nbatch * q)
    hp = SSD_INNER

    def pad_heads(v, fill):
        return jnp.concatenate([v.astype(F32), jnp.full((LANES - SSD_HEADS,), fill, F32)]).reshape(1, LANES)

    expand = (jnp.arange(LANES)[:, None] == (jnp.arange(hp)[None, :] // SSD_HEAD_DIM)).astype(F32)
    dsk = jnp.repeat(d_skip.astype(F32), SSD_HEAD_DIM).reshape(1, hp)
    kern = functools.partial(_ssd_kernel, q=q, valid=valid)
    const2 = lambda shape: pl.BlockSpec(shape, lambda b, c: (0, 0))
    return pl.pallas_call(
        kern,
        grid=(nbatch, nc),
        in_specs=[pl.BlockSpec((q, hp), lambda b, c: (b * nc + c, COL["xs"] // hp)),
                  pl.BlockSpec((q, 256), lambda b, c: (b * nc + c, COL["bm"] // 256)),
                  pl.BlockSpec((q, 256), lambda b, c: (b * nc + c, COL["cm"] // 256)),
                  pl.BlockSpec((q, LANES), lambda b, c: (b * nc + c, COL["dt"] // LANES)),
                  pl.BlockSpec((q, hp), lambda b, c: (b * nc + c, COL["z"] // hp)),
                  pl.BlockSpec((None, 8, CONV_DIM), lambda b, c: (b, 0, 0)),
                  pl.BlockSpec((None, SSD_STATE, hp), lambda b, c: (b, 0, 0)),
                  const2((CONV_WIDTH, CONV_DIM)), const2((1, CONV_DIM)),
                  const2((1, LANES)), const2((1, LANES)), const2((1, hp)), const2((1, hp)),
                  const2((LANES, hp))],
        out_specs=[pl.BlockSpec((q, hp), lambda b, c: (b * nc + c, 0)),
                   pl.BlockSpec((None, SSD_STATE, hp), lambda b, c: (b, 0, 0))],
        out_shape=[jax.ShapeDtypeStruct((t, hp), BF16),
                   jax.ShapeDtypeStruct((nbatch, SSD_STATE, hp), F32)],
        scratch_shapes=[pltpu.VMEM((q + 8, CONV_DIM), F32),
                        pltpu.VMEM((SSD_STATE, hp), F32)],
        compiler_params=_cparams(("arbitrary", "arbitrary")),
        name="ssd",
    )(proj, proj, proj, proj, proj, hist8, state_t, conv_w, conv_b.reshape(1, CONV_DIM),
      pad_heads(dt_bias, 0.0), pad_heads(a_log, 0.0), dsk, ssd_norm_w.reshape(1, hp), expand)


def _pool_kernel(u_ref, pg_ref, hist_ref, w_ref, b_ref, sc_ref, y_ref, ext_ref, *, r, start):
    c = pl.program_id(1)

    @pl.when(c == 0)
    def _():
        ext_ref[0:16, :] = hist_ref[...]

    ext_ref[16:16 + r, :] = u_ref[...]
    pos = start + c * r + lax.broadcasted_iota(I32, (r, 1), 0)
    outs = []
    for gi, w in enumerate(POOL_WINDOWS):
        lo = gi * POOL_GROUP_DIM
        cur = ext_ref[16:16 + r, lo:lo + POOL_GROUP_DIM]
        win = cur
        for s in range(1, w):
            win = win + ext_ref[16 - s:16 - s + r, lo:lo + POOL_GROUP_DIM]
        cnt = jnp.minimum(w, pos + 1).astype(F32)
        pooled = win / cnt - cur
        mixed = jnp.dot(pooled.astype(BF16), w_ref[gi], preferred_element_type=F32)
        outs.append(mixed + b_ref[gi:gi + 1, :])
    ext_ref[0:16, :] = ext_ref[r:r + 16, :]
    mixed = jnp.concatenate(outs, axis=1) * sc_ref[...]
    y_ref[...] = (mixed * _silu(pg_ref[...])).astype(y_ref.dtype)


def pool(proj, hist16, pool_w, pool_b, pool_scale, *, nbatch, r, start):
    t = proj.shape[0]
    nc = t // (nbatch * r)
    d = BRANCH_DIM
    kern = functools.partial(_pool_kernel, r=r, start=start)
    return pl.pallas_call(
        kern,
        grid=(nbatch, nc),
        in_specs=[pl.BlockSpec((r, d), lambda b, c: (b * nc + c, COL["u"] // d)),
                  pl.BlockSpec((r, d), lambda b, c: (b * nc + c, COL["pg"] // d)),
                  pl.BlockSpec((None, 16, d), lambda b, c: (b, 0, 0)),
                  pl.BlockSpec((POOL_GROUPS, POOL_GROUP_DIM, POOL_GROUP_DIM), lambda b, c: (0, 0, 0)),
                  pl.BlockSpec((POOL_GROUPS, POOL_GROUP_DIM), lambda b, c: (0, 0)),
                  pl.BlockSpec((1, d), lambda b, c: (0, 0))],
        out_specs=pl.BlockSpec((r, d), lambda b, c: (b * nc + c, 0)),
        out_shape=jax.ShapeDtypeStruct((t, d), BF16),
        scratch_shapes=[pltpu.VMEM((r + 16, d), F32)],
        compiler_params=_cparams(("arbitrary", "arbitrary")),
        name="pool",
    )(proj, proj, hist16, pool_w.astype(BF16), pool_b, pool_scale.reshape(1, d))


def _merge_kernel(y0_ref, y1_ref, y2_ref, g0_ref, g1_ref, g2_ref, w_ref, o_ref):
    acc = jax.nn.sigmoid(g0_ref[...]) * jnp.dot(y0_ref[...], w_ref[0], preferred_element_type=F32)
    acc = acc + jax.nn.sigmoid(g1_ref[...]) * jnp.dot(y1_ref[...], w_ref[1], preferred_element_type=F32)
    acc = acc + jax.nn.sigmoid(g2_ref[...]) * jnp.dot(y2_ref[...], w_ref[2], preferred_element_type=F32)
    o_ref[...] = acc.astype(o_ref.dtype)


def merge(y_ssd, y_att, y_pool, proj, w_branch, tm):
    t = y_ssd.shape[0]
    tn = 1024
    nj = D_MODEL // tn
    ysp = pl.BlockSpec((tm, BRANCH_DIM), lambda i, j: (i, 0))

    def gate_spec(bi):
        return pl.BlockSpec((tm, tn), lambda i, j: (i, (COL["mg"] + bi * D_MODEL) // tn + j))

    return pl.pallas_call(
        _merge_kernel,
        grid=(t // tm, nj),
        in_specs=[ysp, ysp, ysp, gate_spec(0), gate_spec(1), gate_spec(2),
                  pl.BlockSpec((N_BRANCH, BRANCH_DIM, tn), lambda i, j: (0, 0, j))],
        out_specs=pl.BlockSpec((tm, tn), lambda i, j: (i, j)),
        out_shape=jax.ShapeDtypeStruct((t, D_MODEL), BF16),
        compiler_params=_cparams(("arbitrary", "arbitrary")),
        name="merge",
    )(y_ssd, y_att, y_pool, proj, proj, proj, w_branch)


def _outproj_kernel(m_ref, w_ref, x_ref, g_ref, o_ref):
    o_ref[...] = x_ref[...] + g_ref[...] * jnp.dot(m_ref[...], w_ref[...], preferred_element_type=F32)


def outproj(merged, w_out, x, gate, tm, rows_per_mod):
    t = x.shape[0]
    tn = 1024
    return pl.pallas_call(
        _outproj_kernel,
        grid=(t // tm, D_MODEL // tn),
        in_specs=[pl.BlockSpec((tm, D_MODEL), lambda i, j: (i, 0)),
                  pl.BlockSpec((D_MODEL, tn), lambda i, j: (0, j)),
                  pl.BlockSpec((tm, tn), lambda i, j: (i, j)),
                  _mod_spec(gate, tm, tn, rows_per_mod, lambda j: j)],
        out_specs=pl.BlockSpec((tm, tn), lambda i, j: (i, j)),
        out_shape=jax.ShapeDtypeStruct((t, D_MODEL), F32),
        compiler_params=_cparams(("arbitrary", "arbitrary")),
        name="outproj",
    )(merged, w_out, x, gate)


def _pad_to(a, axis, size):
    pad = [(0, 0)] * a.ndim
    pad[axis] = (0, size - a.shape[axis])
    return jnp.pad(a, pad)


def trunk_layer(x, mod, k_past, v_past, ik_past, h0, conv_hist, pool_hist, rel_bias, lw, *, per_row_mod):
    (norm_w, w_in, conv_w, conv_b, dt_bias, a_log, d_skip, ssd_norm_w, q_norm_w, k_norm_w,
     pool_w, pool_b, pool_scale, w_branch, w_out) = lw
    bsz, seq, d = x.shape
    t = bsz * seq
    start = k_past.shape[1]
    n_keys = start + seq
    topk = min(TOPK_MAX, n_keys // 4)
    x2 = x.reshape(t, d)
    shift, scale, gate = mod[:, :d], mod[:, d:2 * d], mod[:, 2 * d:]
    tm = min(512, t)
    if per_row_mod:
        expand = lambda m: jnp.broadcast_to(m[:, None, :], (bsz, seq, d)).reshape(t // tm, tm, d)
    else:
        expand = lambda m: m[:, None, :]
    scale3, shift3, gate3 = expand(scale), expand(shift), expand(gate)

    proj = inproj(x2, scale3, shift3, norm_w, w_in, tm, seq)
    qh, k_new, iqh = prep(proj, q_norm_w, k_norm_w, tm)

    kw = KV_HEADS * HEAD_DIM
    v_new = proj[:, COL["v"]:COL["v"] + kw]
    ik_new = proj[:, COL["ik"]:COL["ik"] + IDX_DIM]
    k_new4 = k_new.reshape(bsz, seq, KV_HEADS, HEAD_DIM)
    v_new4 = v_new.reshape(bsz, seq, KV_HEADS, HEAD_DIM)
    ik_new3 = ik_new.reshape(bsz, seq, IDX_DIM)

    tq = min(256, seq)
    if seq >= 2048:
        tk, ta = 2048, 512
    else:
        tk = ta = None
    s_pad = -(-n_keys // TKS) * TKS
    if tk is None:
        tk, ta = s_pad, s_pad // 3 if (s_pad // 3) % TKS == 0 else s_pad
    s_pad = -(-s_pad // tk) * tk
    k_all = jnp.concatenate([k_past.astype(F32), k_new4], axis=1).astype(BF16)
    v_all = jnp.concatenate([v_past.astype(F32), v_new4], axis=1).astype(BF16)
    ik_all = jnp.concatenate([ik_past.astype(F32), ik_new3], axis=1).astype(BF16)
    bk = near_buckets(tq, start, n_keys)
    dsa_args = dict(nbatch=bsz, tq=tq, ta=ta, tk=tk, start=start, n_keys=n_keys, topk=topk)
    if tq % LANES == 0:
        k4 = _pad_to(jnp.transpose(k_all, (0, 2, 1, 3)), 2, s_pad)
        vT = _pad_to(jnp.transpose(v_all, (0, 2, 3, 1)), 3, s_pad)
        ik_p = _pad_to(ik_all, 1, s_pad)
        iwT = proj[:, COL["iw"]:COL["iw"] + IDX_HEADS].T
        y_att = dsa_t(jnp.swapaxes(qh, 1, 2), jnp.swapaxes(iqh, 1, 2), iwT, proj, ik_p, k4, vT, bk.T,
                      rel_bias, **dsa_args)
    else:
        kT = _pad_to(jnp.transpose(k_all, (0, 2, 3, 1)), 3, s_pad)
        v4 = _pad_to(jnp.transpose(v_all, (0, 2, 1, 3)), 2, s_pad)
        ikT = _pad_to(jnp.transpose(ik_all, (0, 2, 1)), 2, s_pad)
        y_att = dsa(qh, iqh, proj, ikT, kT, v4, bk, rel_bias, **dsa_args)

    q = 128
    hist8 = jnp.pad(conv_hist.astype(F32), ((0, 0), (8 - (CONV_WIDTH - 1), 0), (0, 0)))
    state_t = jnp.transpose(h0.astype(F32), (0, 3, 1, 2)).reshape(bsz, SSD_STATE, SSD_INNER)
    if seq < q:
        proj_ssd = _pad_to(proj.reshape(bsz, seq, PROJ_DIM), 1, q).reshape(bsz * q, PROJ_DIM)
        valid = seq
    else:
        proj_ssd, valid = proj, q
    y_ssd, st = ssd(proj_ssd, hist8, state_t, conv_w, conv_b, dt_bias, a_log, d_skip, ssd_norm_w,
                    nbatch=bsz, q=q, valid=valid)
    if seq < q:
        y_ssd = y_ssd.reshape(bsz, q, SSD_INNER)[:, :seq].reshape(t, SSD_INNER)
    h_last = jnp.transpose(st.reshape(bsz, SSD_STATE, SSD_HEADS, SSD_HEAD_DIM), (0, 2, 3, 1))

    hist16 = jnp.pad(pool_hist.astype(F32), ((0, 0), (16 - POOL_STATE, 0), (0, 0)))
    y_pool = pool(proj, hist16, pool_w, pool_b, pool_scale, nbatch=bsz, r=min(512, seq), start=start)

    merged = merge(y_ssd, y_att, y_pool, proj, w_branch, tm)
    x_new = outproj(merged, w_out, x2, gate3, tm, seq).reshape(bsz, seq, d)

    xbc = proj[:, COL["xbc"]:COL["xbc"] + CONV_DIM].reshape(bsz, seq, CONV_DIM)
    u = proj[:, COL["u"]:COL["u"] + BRANCH_DIM].reshape(bsz, seq, BRANCH_DIM)
    conv_new = jnp.concatenate([conv_hist.astype(F32), xbc], axis=1)[:, -(CONV_WIDTH - 1):]
    pool_new = jnp.concatenate([pool_hist.astype(F32), u[:, -min(seq, POOL_STATE):]], axis=1)[:, -POOL_STATE:]
    return x_new, k_new4, v_new4, ik_new3, h_last, conv_new, pool_new


def _reorder_w_in(w_in):
    parts = []
    for name in _NEW_ORDER:
        off, size = _ORIG[name]
        seg = w_in[..., off:off + size]
        padw = -(-size // LANES) * LANES - size
        if padw:
            seg = jnp.pad(seg, ((0, 0), (0, 0), (0, padw)))
        parts.append(seg)
    out = jnp.concatenate(parts, axis=-1)
    return _pad_to(out, 2, PROJ_DIM).astype(BF16)


def kernel(x_prompt, x_sample, cache_k, cache_v, cache_idx_k, state_ssm, state_conv, state_pool,
           c_prompt, c_sample, rel_bias, w_ada, b_ada, norm_w, w_in, conv_w, conv_b, dt_bias,
           a_log, d_skip, ssd_norm_w, q_norm_w, k_norm_w, pool_w, pool_b, pool_scale,
           w_branch, w_out):
    bp = x_prompt.shape[0]
    f32 = F32
    mods = ada_mod(jnp.concatenate([c_prompt, c_sample], axis=0), w_ada, b_ada)
    w_in_r = _reorder_w_in(w_in)
    w_branch_b = w_branch.astype(BF16)
    w_out_b = w_out.astype(BF16)

    empty_kv = jnp.zeros((bp, 0, KV_HEADS, HEAD_DIM), f32)
    empty_ik = jnp.zeros((bp, 0, IDX_DIM), f32)
    zero_ssm = jnp.zeros((bp, SSD_HEADS, SSD_HEAD_DIM, SSD_STATE), f32)
    zero_conv = jnp.zeros((bp, CONV_WIDTH - 1, CONV_DIM), f32)
    zero_pool = jnp.zeros((bp, POOL_STATE, BRANCH_DIM), f32)

    xp, xs = x_prompt, x_sample
    outs_p = [[] for _ in range(6)]
    outs_s = [[] for _ in range(6)]
    for l in range(DEPTH):
        lw = (norm_w[l], w_in_r[l], conv_w[l], conv_b[l], dt_bias[l], a_log[l], d_skip[l], ssd_norm_w[l],
              q_norm_w[l], k_norm_w[l], pool_w[l], pool_b[l], pool_scale[l], w_branch_b[l], w_out_b[l])
        rp = trunk_layer(xp, mods[l, :bp], empty_kv, empty_kv, empty_ik, zero_ssm, zero_conv, zero_pool,
                         rel_bias, lw, per_row_mod=False)
        rs = trunk_layer(xs, mods[l, bp:], cache_k[l], cache_v[l], cache_idx_k[l], state_ssm[l],
                         state_conv[l], state_pool[l], rel_bias, lw, per_row_mod=True)
        xp, xs = rp[0], rs[0]
        for n in range(6):
            outs_p[n].append(rp[n + 1])
            outs_s[n].append(rs[n + 1])
    return (xp, xs, *[jnp.stack(o) for o in outs_p], *[jnp.stack(o) for o in outs_s])
```

```python
import functools
import math

import numpy as np
import jax
import jax.numpy as jnp
from jax import lax
from jax.experimental import pallas as pl
from jax.experimental.pallas import tpu as pltpu

F32 = jnp.float32
BF16 = jnp.bfloat16
I32 = jnp.int32

D_MODEL = 2048
DEPTH = 4
CHUNK = 64
N_BRANCH = 3
BRANCH_DIM = 1024
SSD_INNER = BRANCH_DIM
SSD_HEAD_DIM = 64
SSD_HEADS = SSD_INNER // SSD_HEAD_DIM
SSD_GROUPS = 2
SSD_STATE = 128
CONV_WIDTH = 4
CONV_DIM = SSD_INNER + 2 * SSD_GROUPS * SSD_STATE
ATTN_HEADS = 16
KV_HEADS = 4
HEAD_DIM = BRANCH_DIM // ATTN_HEADS
Q_PER_KV = ATTN_HEADS // KV_HEADS
IDX_HEADS = 8
IDX_DIM = 64
TOPK_MAX = 256
N_BUCKETS = 32
MAX_DISTANCE = 128
POOL_WINDOWS = (2, 4, 8, 16)
POOL_GROUPS = 4
POOL_GROUP_DIM = BRANCH_DIM // POOL_GROUPS
POOL_STATE = 15
EPS = 1e-6
NEG = -1e30

LANES = 128
VMEM_LIMIT = 56 * 1024 * 1024

_ORIG = {}
_off = 0
for _name, _size in (("z", SSD_INNER), ("xbc", CONV_DIM), ("dt", SSD_HEADS), ("q", BRANCH_DIM),
                     ("k", KV_HEADS * HEAD_DIM), ("v", KV_HEADS * HEAD_DIM), ("ag", BRANCH_DIM),
                     ("iq", IDX_HEADS * IDX_DIM), ("ik", IDX_DIM), ("iw", IDX_HEADS),
                     ("u", BRANCH_DIM), ("pg", BRANCH_DIM), ("mg", N_BRANCH * D_MODEL)):
    _ORIG[_name] = (_off, _size)
    _off += _size
IN_DIM = _off

_NEW_ORDER = ("mg", "z", "q", "ag", "u", "pg", "xbc", "iq", "k", "v", "ik", "dt", "iw")
COL = {}
_off = 0
for _name in _NEW_ORDER:
    COL[_name] = _off
    _off += -(-_ORIG[_name][1] // LANES) * LANES
PROJ_DIM = -(-_off // 2048) * 2048
COL["xs"] = COL["xbc"]
COL["bm"] = COL["xbc"] + SSD_INNER
COL["cm"] = COL["bm"] + SSD_GROUPS * SSD_STATE


def _sortable_const(v):
    i = int(np.float32(v).view(np.int32))
    return i ^ ((i >> 31) & 0x7FFFFFFF)


NEG_KEY = _sortable_const(NEG)
INT_MIN = -2 ** 31


def _cparams(sem):
    return pltpu.CompilerParams(dimension_semantics=sem, vmem_limit_bytes=VMEM_LIMIT)


def _silu(x):
    return x * jax.nn.sigmoid(x)


def _ada_kernel(c_ref, w_ref, b_ref, o_ref):
    c = c_ref[...]
    o_ref[...] = jnp.dot(_silu(c).astype(BF16), w_ref[...].astype(BF16),
                         preferred_element_type=F32) + b_ref[...]


def ada_mod(c_all, w_ada, b_ada):
    nb, d = c_all.shape
    n = w_ada.shape[-1]
    tn = 512
    return pl.pallas_call(
        _ada_kernel,
        grid=(DEPTH, n // tn),
        in_specs=[pl.BlockSpec((nb, d), lambda l, j: (0, 0)),
                  pl.BlockSpec((None, d, tn), lambda l, j: (l, 0, j)),
                  pl.BlockSpec((None, 1, tn), lambda l, j: (l, 0, j))],
        out_specs=pl.BlockSpec((None, nb, tn), lambda l, j: (l, 0, j)),
        out_shape=jax.ShapeDtypeStruct((DEPTH, nb, n), F32),
        compiler_params=_cparams(("arbitrary", "arbitrary")),
        name="ada",
    )(c_all, w_ada, b_ada.reshape(DEPTH, 1, n))


def _inproj_kernel(x_ref, sc_ref, sh_ref, nw_ref, w_ref, o_ref, h_ref):
    @pl.when(pl.program_id(1) == 0)
    def _():
        x = x_ref[...]
        ms = jnp.mean(x * x, axis=-1, keepdims=True)
        y = x * lax.rsqrt(ms + EPS) * nw_ref[...]
        h_ref[...] = (y * (1.0 + sc_ref[...]) + sh_ref[...]).astype(BF16)

    o_ref[...] = jnp.dot(h_ref[...], w_ref[...], preferred_element_type=F32)


def _mod_spec(mod, tm, tn, rows_per_mod, col_of_j):
    r = mod.shape[1]
    if r == 1:
        tiles = rows_per_mod // tm
        return pl.BlockSpec((None, 1, tn), lambda i, j: (i // tiles, 0, col_of_j(j)))
    return pl.BlockSpec((None, r, tn), lambda i, j: (i, 0, col_of_j(j)))


def inproj(x, scale, shift, norm_w, w, tm, rows_per_mod):
    t, d = x.shape
    n = w.shape[1]
    tn = 1024
    zero = lambda j: 0
    return pl.pallas_call(
        _inproj_kernel,
        grid=(t // tm, n // tn),
        in_specs=[pl.BlockSpec((tm, d), lambda i, j: (i, 0)),
                  _mod_spec(scale, tm, d, rows_per_mod, zero),
                  _mod_spec(shift, tm, d, rows_per_mod, zero),
                  pl.BlockSpec((1, d), lambda i, j: (0, 0)),
                  pl.BlockSpec((d, tn), lambda i, j: (0, j))],
        out_specs=pl.BlockSpec((tm, tn), lambda i, j: (i, j)),
        out_shape=jax.ShapeDtypeStruct((t, n), F32),
        scratch_shapes=[pltpu.VMEM((tm, d), BF16)],
        compiler_params=_cparams(("arbitrary", "arbitrary")),
        name="inproj",
    )(x, scale, shift, norm_w.reshape(1, d), w)


def _prep_kernel(q_ref, k_ref, iq_ref, qw_ref, kw_ref, qh_ref, ko_ref, iqh_ref):
    def head_norm(xs, w):
        ms = jnp.mean(xs * xs, axis=-1, keepdims=True)
        return xs * lax.rsqrt(ms + EPS) * w

    q = q_ref[...]
    for h in range(ATTN_HEADS):
        qn = head_norm(q[:, h * HEAD_DIM:(h + 1) * HEAD_DIM], qw_ref[...])
        qh_ref[h] = (qn * HEAD_DIM ** -0.5).astype(BF16)
    k = k_ref[...]
    for g in range(KV_HEADS):
        ko_ref[:, g * HEAD_DIM:(g + 1) * HEAD_DIM] = head_norm(k[:, g * HEAD_DIM:(g + 1) * HEAD_DIM], kw_ref[...])
    iq = iq_ref[...]
    for h in range(IDX_HEADS):
        iqh_ref[h] = (iq[:, h * IDX_DIM:(h + 1) * IDX_DIM] * IDX_DIM ** -0.5).astype(BF16)


def prep(proj, q_norm_w, k_norm_w, tm):
    t = proj.shape[0]
    kw = KV_HEADS * HEAD_DIM
    iqw = IDX_HEADS * IDX_DIM
    return pl.pallas_call(
        _prep_kernel,
        grid=(t // tm,),
        in_specs=[pl.BlockSpec((tm, BRANCH_DIM), lambda i: (i, COL["q"] // BRANCH_DIM)),
                  pl.BlockSpec((tm, kw), lambda i: (i, COL["k"] // kw)),
                  pl.BlockSpec((tm, iqw), lambda i: (i, COL["iq"] // iqw)),
                  pl.BlockSpec((1, HEAD_DIM), lambda i: (0, 0)),
                  pl.BlockSpec((1, HEAD_DIM), lambda i: (0, 0))],
        out_specs=[pl.BlockSpec((ATTN_HEADS, tm, HEAD_DIM), lambda i: (0, i, 0)),
                   pl.BlockSpec((tm, kw), lambda i: (i, 0)),
                   pl.BlockSpec((IDX_HEADS, tm, IDX_DIM), lambda i: (0, i, 0))],
        out_shape=[jax.ShapeDtypeStruct((ATTN_HEADS, t, HEAD_DIM), BF16),
                   jax.ShapeDtypeStruct((t, kw), F32),
                   jax.ShapeDtypeStruct((IDX_HEADS, t, IDX_DIM), BF16)],
        compiler_params=_cparams(("arbitrary",)),
        name="prep",
    )(proj, proj, proj, q_norm_w.reshape(1, HEAD_DIM), k_norm_w.reshape(1, HEAD_DIM))


def _prep_t_kernel(q_ref, k_ref, v_ref, iq_ref, ik_ref, iw_ref, qw_ref, kw_ref,
                   qT_ref, iqT_ref, iwT_ref, ko_ref, vo_ref, iko_ref, ka_ref, va_ref, ikb_ref):
    def head_norm(xs, w):
        ms = jnp.mean(xs * xs, axis=-1, keepdims=True)
        return xs * lax.rsqrt(ms + EPS) * w

    tm = q_ref.shape[0]
    q = q_ref[...]
    qn = jnp.concatenate([head_norm(q[:, h * HEAD_DIM:(h + 1) * HEAD_DIM], qw_ref[...])
                          for h in range(ATTN_HEADS)], axis=1)
    qT_ref[...] = (qn * HEAD_DIM ** -0.5).T.reshape(ATTN_HEADS, HEAD_DIM, tm).astype(BF16)
    iqT_ref[...] = (iq_ref[...] * IDX_DIM ** -0.5).T.reshape(IDX_HEADS, IDX_DIM, tm).astype(BF16)
    iwT_ref[...] = iw_ref[...].T[0:IDX_HEADS, :]

    k = k_ref[...]
    lane = lax.broadcasted_iota(I32, (tm, AUG_K), 1)
    for g in range(KV_HEADS):
        kn = head_norm(k[:, g * HEAD_DIM:(g + 1) * HEAD_DIM], kw_ref[...])
        ko_ref[:, g * HEAD_DIM:(g + 1) * HEAD_DIM] = kn
        kpad = jnp.concatenate([kn, jnp.zeros((tm, AUG_K - HEAD_DIM), F32)], axis=1)
        ka_ref[g] = jnp.where(lane == HEAD_DIM, 1.0, kpad).astype(BF16)

    v = v_ref[...]
    vo_ref[...] = v
    vt = v.T
    row = lax.broadcasted_iota(I32, (V_ROWS - HEAD_DIM, tm), 0)
    tail = jnp.where(row == 0, 1.0, 0.0).astype(BF16)
    for g in range(KV_HEADS):
        va_ref[g, 0:HEAD_DIM, :] = vt[g * HEAD_DIM:(g + 1) * HEAD_DIM, :].astype(BF16)
        va_ref[g, HEAD_DIM:V_ROWS, :] = tail

    ik = ik_ref[...][:, 0:IDX_DIM]
    iko_ref[...] = ik
    ikb_ref[...] = ik.astype(BF16)


def prep_t(proj, q_norm_w, k_norm_w, tm, nbatch):
    t = proj.shape[0]
    s = t // nbatch
    tiles = s // tm
    kw = KV_HEADS * HEAD_DIM
    iqw = IDX_HEADS * IDX_DIM
    row = lambda w, name: pl.BlockSpec((tm, w), lambda i: (i, COL[name] // w))
    return pl.pallas_call(
        _prep_t_kernel,
        grid=(t // tm,),
        in_specs=[row(BRANCH_DIM, "q"), row(kw, "k"), row(kw, "v"), row(iqw, "iq"),
                  row(LANES, "ik"), row(LANES, "iw"),
                  pl.BlockSpec((1, HEAD_DIM), lambda i: (0, 0)),
                  pl.BlockSpec((1, HEAD_DIM), lambda i: (0, 0))],
        out_specs=[pl.BlockSpec((ATTN_HEADS, HEAD_DIM, tm), lambda i: (0, 0, i)),
                   pl.BlockSpec((IDX_HEADS, IDX_DIM, tm), lambda i: (0, 0, i)),
                   pl.BlockSpec((IDX_HEADS, tm), lambda i: (0, i)),
                   pl.BlockSpec((tm, kw), lambda i: (i, 0)),
                   pl.BlockSpec((tm, kw), lambda i: (i, 0)),
                   pl.BlockSpec((tm, IDX_DIM), lambda i: (i, 0)),
                   pl.BlockSpec((None, KV_HEADS, tm, AUG_K), lambda i: (i // tiles, 0, i % tiles, 0)),
                   pl.BlockSpec((None, KV_HEADS, V_ROWS, tm), lambda i: (i // tiles, 0, 0, i % tiles)),
                   pl.BlockSpec((None, tm, IDX_DIM), lambda i: (i // tiles, i % tiles, 0))],
        out_shape=[jax.ShapeDtypeStruct((ATTN_HEADS, HEAD_DIM, t), BF16),
                   jax.ShapeDtypeStruct((IDX_HEADS, IDX_DIM, t), BF16),
                   jax.ShapeDtypeStruct((IDX_HEADS, t), F32),
                   jax.ShapeDtypeStruct((t, kw), F32),
                   jax.ShapeDtypeStruct((t, kw), F32),
                   jax.ShapeDtypeStruct((t, IDX_DIM), F32),
                   jax.ShapeDtypeStruct((nbatch, KV_HEADS, s, AUG_K), BF16),
                   jax.ShapeDtypeStruct((nbatch, KV_HEADS, V_ROWS, s), BF16),
                   jax.ShapeDtypeStruct((nbatch, s, IDX_DIM), BF16)],
        compiler_params=_cparams(("arbitrary",)),
        name="prep_t",
    )(proj, proj, proj, proj, proj, proj, q_norm_w.reshape(1, HEAD_DIM), k_norm_w.reshape(1, HEAD_DIM))


TKS = 256
NEAR_W = 2 * TKS
FAR_BUCKET = N_BUCKETS // 2 - 1
J_ALL = 2 ** 30


def _dsa_kernel(tab_ref, qh_ref, iqh_ref, iw_ref, ikT_ref, kT_ref, v_ref, bk_ref, ag_ref,
                o_ref, keys_ref, nb_ref, acc_ref, m_ref, l_ref, j_ref, *,
                tq, ta, tk, start, n_keys, topk):
    b = pl.program_id(0)
    i = pl.program_id(1)
    kk = pl.program_id(2)
    nk = pl.num_programs(2)
    q0 = start + i * tq
    vis_end = ((q0 + tq - 1) // CHUNK + 1) * CHUNK
    n_cols = jnp.minimum(vis_end, n_keys)
    n_a = (n_cols + ta - 1) // ta
    n_sub = (n_cols + TKS - 1) // TKS
    extra = jnp.maximum(n_keys - n_a * ta, 0).astype(F32)
    sub_per_tile = tk // TKS

    def vis_mask(t, width):
        kpos = t * width + lax.broadcasted_iota(I32, (tq, width), 1)
        qpos = q0 + lax.broadcasted_iota(I32, (tq, width), 0)
        vis = (kpos // CHUNK) <= (qpos // CHUNK)
        return vis, kpos

    @pl.when((b == 0) & (i == 0) & (kk == 0))
    def _():
        bk = bk_ref[...]

        def per_head(h, _):
            def per_bucket(n, val):
                return jnp.where(bk == n, tab_ref[n, h], val)
            val = lax.fori_loop(0, N_BUCKETS, per_bucket, jnp.zeros((tq, NEAR_W), F32))
            nb_ref[h] = val - tab_ref[FAR_BUCKET, h]
            return 0
        lax.fori_loop(0, ATTN_HEADS, per_head, 0)

    @pl.when(kk == 0)
    def _():
        iw = iw_ref[...] * IDX_HEADS ** -0.5

        def score_tile(t, _):
            c0 = pl.multiple_of(t * ta, ta)
            ikt = ikT_ref[:, pl.ds(c0, ta)]
            acc = jnp.zeros((tq, ta), F32)
            for h in range(IDX_HEADS):
                s = jnp.dot(iqh_ref[h], ikt, preferred_element_type=F32)
                acc = acc + iw[:, h:h + 1] * jnp.maximum(s, 0.0)
            vis, kpos = vis_mask(t, ta)
            acc = jnp.where(vis, acc, NEG)
            bits = pltpu.bitcast(acc, I32)
            key = bits ^ ((bits >> 31) & 0x7FFFFFFF)
            key = jnp.where(kpos < n_keys, key, INT_MIN)
            keys_ref[:, pl.ds(c0, ta)] = key
            return 0
        lax.fori_loop(0, n_a, score_tile, 0)

        def count(pred):
            def body(t, cnt):
                c0 = pl.multiple_of(t * ta, ta)
                blk = keys_ref[:, pl.ds(c0, ta)]
                c = jnp.where(pred(blk, t), 1.0, 0.0)
                for jj in range(ta // LANES):
                    cnt = cnt + c[:, jj * LANES:(jj + 1) * LANES]
                return cnt
            cnt = lax.fori_loop(0, n_a, body, jnp.zeros((tq, LANES), F32))
            return jnp.sum(cnt, axis=1, keepdims=True)

        def bit_step(it, prefix_u):
            bit = jnp.left_shift(jnp.int32(1), 31 - it)
            cand_u = prefix_u | bit
            cand_s = cand_u ^ INT_MIN
            cnt = count(lambda blk, t: blk >= cand_s) + jnp.where(NEG_KEY >= cand_s, extra, 0.0)
            return jnp.where(cnt >= topk, cand_u, prefix_u)
        prefix_u = lax.fori_loop(0, 32, bit_step, jnp.zeros((tq, 1), I32))
        thr = prefix_u ^ INT_MIN

        cnt_gt = count(lambda blk, t: blk > thr) + jnp.where(NEG_KEY > thr, extra, 0.0)
        cnt_eq = count(lambda blk, t: blk == thr)
        need = topk - cnt_gt
        j_ref[...] = jnp.full((tq, LANES), J_ALL, I32)

        @pl.when(jnp.max(cnt_eq - need) > 0.0)
        def _():
            def idx_step(it, jmax):
                cand = jmax | jnp.left_shift(jnp.int32(1), 14 - it)

                def pred(blk, t):
                    _, kpos = vis_mask(t, ta)
                    return (blk == thr) & (kpos < cand)
                g = count(pred)
                return jnp.where(g < need, cand, jmax)
            jmax = lax.fori_loop(0, 15, idx_step, jnp.zeros((tq, 1), I32))
            j_ref[...] = jnp.broadcast_to(jmax, (tq, LANES))

        jmax = j_ref[:, 0:1]

        def mask_tile(t, _):
            c0 = pl.multiple_of(t * ta, ta)
            blk = keys_ref[:, pl.ds(c0, ta)]
            vis, kpos = vis_mask(t, ta)
            sel = (blk > thr) | ((blk == thr) & (kpos <= jmax))
            ok = sel & vis & (kpos < n_keys)
            madd = jnp.where(ok, 0.0, NEG).astype(F32)
            keys_ref[:, pl.ds(c0, ta)] = pltpu.bitcast(madd, I32)
            return 0
        lax.fori_loop(0, n_a, mask_tile, 0)

        m_ref[...] = jnp.full(m_ref.shape, -jnp.inf, F32)
        l_ref[...] = jnp.zeros(l_ref.shape, F32)
        acc_ref[...] = jnp.zeros(acc_ref.shape, F32)

    def attend(jl, boff):
        lo = pl.multiple_of(jl * TKS, TKS)
        c0 = pl.multiple_of(kk * tk + lo, TKS)
        madd = pltpu.bitcast(keys_ref[:, pl.ds(c0, TKS)], F32)

        def group(g, _):
            kt = kT_ref[g, :, pl.ds(lo, TKS)]
            vv = v_ref[g, pl.ds(lo, TKS), :]
            for r in range(Q_PER_KV):
                h = g * Q_PER_KV + r
                s = jnp.dot(qh_ref[h], kt, preferred_element_type=F32) + madd
                if boff is not None:
                    s = s + nb_ref[h, :, boff:boff + TKS]
                m_prev = m_ref[h]
                m_new = jnp.maximum(m_prev, jnp.max(s, axis=1, keepdims=True))
                alpha = jnp.exp(m_prev - m_new)
                p = jnp.exp(s - m_new[:, 0:1])
                l_ref[h] = alpha * l_ref[h] + jnp.sum(p, axis=1, keepdims=True)
                acc_ref[h] = alpha[:, 0:HEAD_DIM] * acc_ref[h] + jnp.dot(
                    p.astype(BF16), vv, preferred_element_type=F32)
                m_ref[h] = m_new
            return 0
        lax.fori_loop(0, KV_HEADS, group, 0)

    n_here = jnp.clip(n_sub - kk * sub_per_tile, 0, sub_per_tile)

    def sub_body(jl, _):
        u = kk * sub_per_tile + jl

        @pl.when(u < n_sub - 2)
        def _():
            attend(jl, None)

        @pl.when(u == n_sub - 2)
        def _():
            attend(jl, 0)

        @pl.when(u == n_sub - 1)
        def _():
            attend(jl, TKS)
        return 0
    lax.fori_loop(0, n_here, sub_body, 0)

    @pl.when(kk == nk - 1)
    def _():
        ag = ag_ref[...]
        for h in range(ATTN_HEADS):
            o = acc_ref[h] / l_ref[h][:, 0:HEAD_DIM]
            sl = slice(h * HEAD_DIM, (h + 1) * HEAD_DIM)
            o_ref[:, sl] = (o * _silu(ag[:, sl])).astype(o_ref.dtype)


def dsa(qh, iqh, proj, ikT, kT, v4, bk, rel_bias, *, nbatch, tq, ta, tk, start, n_keys, topk):
    t = qh.shape[1]
    s_pad = ikT.shape[-1]
    nq = t // (nbatch * tq)
    nk = s_pad // tk

    def kt_idx(b, i, kk):
        q0 = start + i * tq
        n_cols = jnp.minimum(((q0 + tq - 1) // CHUNK + 1) * CHUNK, n_keys)
        return jnp.minimum(kk, (n_cols + tk - 1) // tk - 1)

    kern = functools.partial(_dsa_kernel, tq=tq, ta=ta, tk=tk, start=start, n_keys=n_keys, topk=topk)
    return pl.pallas_call(
        kern,
        grid=(nbatch, nq, nk),
        in_specs=[pl.BlockSpec(memory_space=pltpu.SMEM),
                  pl.BlockSpec((ATTN_HEADS, tq, HEAD_DIM), lambda b, i, kk: (0, b * nq + i, 0)),
                  pl.BlockSpec((IDX_HEADS, tq, IDX_DIM), lambda b, i, kk: (0, b * nq + i, 0)),
                  pl.BlockSpec((tq, LANES), lambda b, i, kk: (b * nq + i, COL["iw"] // LANES)),
                  pl.BlockSpec((None, IDX_DIM, s_pad), lambda b, i, kk: (b, 0, 0)),
                  pl.BlockSpec((None, KV_HEADS, HEAD_DIM, tk), lambda b, i, kk: (b, 0, 0, kt_idx(b, i, kk))),
                  pl.BlockSpec((None, KV_HEADS, tk, HEAD_DIM), lambda b, i, kk: (b, 0, kt_idx(b, i, kk), 0)),
                  pl.BlockSpec((tq, NEAR_W), lambda b, i, kk: (0, 0)),
                  pl.BlockSpec((tq, BRANCH_DIM), lambda b, i, kk: (b * nq + i, COL["ag"] // BRANCH_DIM))],
        out_specs=pl.BlockSpec((tq, BRANCH_DIM), lambda b, i, kk: (b * nq + i, 0)),
        out_shape=jax.ShapeDtypeStruct((t, BRANCH_DIM), BF16),
        scratch_shapes=[pltpu.VMEM((tq, s_pad), I32),
                        pltpu.VMEM((ATTN_HEADS, tq, NEAR_W), F32),
                        pltpu.VMEM((ATTN_HEADS, tq, HEAD_DIM), F32),
                        pltpu.VMEM((ATTN_HEADS, tq, LANES), F32),
                        pltpu.VMEM((ATTN_HEADS, tq, LANES), F32),
                        pltpu.VMEM((tq, LANES), I32)],
        compiler_params=_cparams(("arbitrary", "arbitrary", "arbitrary")),
        name="dsa",
    )(rel_bias, qh, iqh, proj, ikT, kT, v4, bk, proj)


AUG_K = 128
V_ROWS = 80
CNT_ROWS = 64
FAST_BOUND = 30.0
GM_ROWS = 256
QK_LOOKAHEAD = 8


def _dsat_kernel(tab_ref, kmax_ref, qT_ref, iqT_ref, iw_ref, ik_ref, k_ref, vT_ref, bk_ref, ag_ref,
                 o_ref, keys_ref, nb_ref, acc_ref, m_ref, qa_ref, j_ref, gm_ref, bmax_ref, fast_ref, *,
                 tq, ta, tk, start, n_keys, topk):
    b = pl.program_id(0)
    i = pl.program_id(1)
    kk = pl.program_id(2)
    nk = pl.num_programs(2)
    q0 = start + i * tq
    vis_end = ((q0 + tq - 1) // CHUNK + 1) * CHUNK
    n_cols = jnp.minimum(vis_end, n_keys)
    n_a = (n_cols + ta - 1) // ta
    n_sub = (n_cols + TKS - 1) // TKS
    extra = jnp.maximum(n_keys - n_a * ta, 0).astype(F32)
    sub_per_tile = tk // TKS

    def key_pos(t):
        return t * ta + lax.broadcasted_iota(I32, (ta, tq), 0)

    def vis_of(kpos):
        qpos = q0 + lax.broadcasted_iota(I32, (ta, tq), 1)
        return (kpos >> 6) <= (qpos >> 6)

    def colsum(x):
        return jnp.sum(x.reshape(ta // CNT_ROWS, CNT_ROWS, tq), axis=0)

    @pl.when((b == 0) & (i == 0) & (kk == 0))
    def _():
        bk = bk_ref[...]

        def per_head(h, bmax):
            def per_bucket(n, val):
                return jnp.where(bk == n, tab_ref[n, h], val)
            val = lax.fori_loop(0, N_BUCKETS, per_bucket, jnp.zeros((NEAR_W, tq), F32))
            val = val - tab_ref[FAR_BUCKET, h]
            nb_ref[h] = val
            return jnp.maximum(bmax, jnp.max(val))
        bmax_ref[0] = lax.fori_loop(0, ATTN_HEADS, per_head, jnp.float32(0.0))

    @pl.when(kk == 0)
    def _():
        rows = lax.broadcasted_iota(I32, (AUG_K, tq), 0)
        worst = jnp.zeros((1, tq), F32)
        for h in range(ATTN_HEADS):
            q = qT_ref[h].astype(F32)
            nrm = jnp.sqrt(jnp.sum(q * q, axis=0, keepdims=True))
            bound = nrm * kmax_ref[b, h // Q_PER_KV] + bmax_ref[0]
            worst = jnp.maximum(worst, bound)
            qpad = jnp.concatenate([q, jnp.zeros((AUG_K - HEAD_DIM, tq), F32)], axis=0)
            qa_ref[h] = jnp.where(rows == HEAD_DIM, -bound, qpad).astype(BF16)
        fast_ref[0] = (jnp.max(worst) <= FAST_BOUND).astype(I32)

        iw = iw_ref[...] * IDX_HEADS ** -0.5

        def score_tile(t, edge):
            c0 = pl.multiple_of(t * ta, ta)
            ikt = ik_ref[pl.ds(c0, ta), :]
            acc = jnp.zeros((ta, tq), F32)
            for h in range(IDX_HEADS):
                s = jnp.dot(ikt, iqT_ref[h], preferred_element_type=F32)
                acc = acc + iw[h:h + 1, :] * jnp.maximum(s, 0.0)
            if edge:
                kpos = key_pos(t)
                acc = jnp.where(vis_of(kpos), acc, NEG)
            bits = pltpu.bitcast(acc, I32)
            key = bits ^ ((bits >> 31) & 0x7FFFFFFF)
            if edge:
                key = jnp.where(kpos < n_keys, key, INT_MIN)
                acc = jnp.where(kpos < n_keys, acc, -jnp.inf)
            keys_ref[pl.ds(c0, ta), :] = key
            gm_ref[...] = jnp.maximum(gm_ref[...], jnp.max(acc.reshape(ta // GM_ROWS, GM_ROWS, tq), axis=0))
        gm_ref[...] = jnp.full((GM_ROWS, tq), -jnp.inf, F32)
        lax.fori_loop(0, n_a - 1, lambda t, _: (score_tile(t, False), 0)[1], 0)
        score_tile(n_a - 1, True)

        def count(pred):
            def body(t, cnt):
                c0 = pl.multiple_of(t * ta, ta)
                blk = keys_ref[pl.ds(c0, ta), :]
                return cnt + colsum(jnp.where(pred(blk, t), 1.0, 0.0))
            cnt = lax.fori_loop(0, n_a, body, jnp.zeros((CNT_ROWS, tq), F32))
            return jnp.sum(cnt, axis=0, keepdims=True)

        def count_ge(p):
            return count(lambda blk, t: blk >= p) + jnp.where(NEG_KEY >= p, extra, 0.0)

        def key_of(f):
            bits = pltpu.bitcast(f, I32)
            return bits ^ ((bits >> 31) & 0x7FFFFFFF)

        gm = gm_ref[...]
        lo0 = key_of(jnp.min(gm, axis=0, keepdims=True))
        hi0 = key_of(jnp.max(gm, axis=0, keepdims=True)) + 1
        c_lo0 = jnp.full((1, tq), float(2 ** 24), F32)

        def active_of(lo, hi, c_lo):
            return (c_lo != topk) & (lo + 1 < hi)

        def search_cond(st):
            return st[5] > 0.0

        def search_body(st):
            lo, hi, c_lo, c_hi, it, _ = st
            mid = (lo >> 1) + (hi >> 1) + (lo & hi & 1)
            zero_pivot = (it < 2) & (lo < it) & (it < hi)
            p = jnp.where(zero_pivot, it, mid)
            p = jnp.minimum(jnp.maximum(p, lo + 1), hi - 1)
            c = count_ge(p)
            act = active_of(lo, hi, c_lo)
            up = act & (c >= topk)
            dn = act & (c < topk)
            lo, c_lo = jnp.where(up, p, lo), jnp.where(up, c, c_lo)
            hi, c_hi = jnp.where(dn, p, hi), jnp.where(dn, c, c_hi)
            left = jnp.max(jnp.where(active_of(lo, hi, c_lo), 1.0, 0.0))
            return lo, hi, c_lo, c_hi, it + 1, left

        st0 = (lo0, hi0, c_lo0, jnp.zeros((1, tq), F32), jnp.int32(0),
               jnp.max(jnp.where(active_of(lo0, hi0, c_lo0), 1.0, 0.0)))
        thr, _, c_thr, _, _, _ = lax.while_loop(search_cond, search_body, st0)
        j_ref[...] = jnp.full((8, tq), J_ALL, I32)

        @pl.when(jnp.max(c_thr) > topk)
        def _():
            cnt_gt = count(lambda blk, t: blk > thr) + jnp.where(NEG_KEY > thr, extra, 0.0)
            need = topk - cnt_gt

            def idx_step(it, jmax):
                cand = jmax | jnp.left_shift(jnp.int32(1), 14 - it)
                g = count(lambda blk, t: (blk == thr) & (key_pos(t) < cand))
                return jnp.where(g < need, cand, jmax)
            jmax = lax.fori_loop(0, 15, idx_step, jnp.zeros((1, tq), I32))
            j_ref[...] = jnp.broadcast_to(jmax, (8, tq))

        jmax = j_ref[0:1, :]

        def mask_tile(t, edge):
            c0 = pl.multiple_of(t * ta, ta)
            blk = keys_ref[pl.ds(c0, ta), :]
            kpos = key_pos(t)
            ok = (blk > thr) | ((blk == thr) & (kpos <= jmax))
            if edge:
                ok = ok & vis_of(kpos) & (kpos < n_keys)
            keys_ref[pl.ds(c0, ta), :] = pltpu.bitcast(jnp.where(ok, 1.0, 0.0).astype(F32), I32)
        lax.fori_loop(0, n_a - 1, lambda t, _: (mask_tile(t, False), 0)[1], 0)
        mask_tile(n_a - 1, True)

        m_ref[...] = jnp.full(m_ref.shape, -jnp.inf, F32)
        acc_ref[...] = jnp.zeros(acc_ref.shape, F32)

    def attend_fast(jl, boff):
        lo = pl.multiple_of(jl * TKS, TKS)
        c0 = pl.multiple_of(kk * tk + lo, TKS)
        mask = pltpu.bitcast(keys_ref[pl.ds(c0, TKS), :], F32)

        def logits(h):
            kt = k_ref[h // Q_PER_KV, pl.ds(lo, TKS), :]
            return jnp.dot(kt, qa_ref[h], preferred_element_type=F32)

        pending = [logits(h) for h in range(QK_LOOKAHEAD)]
        for h in range(ATTN_HEADS):
            if h + QK_LOOKAHEAD < ATTN_HEADS:
                pending.append(logits(h + QK_LOOKAHEAD))
            s = pending.pop(0)
            if boff is not None:
                s = nb_ref[h, boff:boff + TKS, :] + s
            p = (jnp.exp(s) * mask).astype(BF16)
            vt = vT_ref[h // Q_PER_KV, :, pl.ds(lo, TKS)]
            acc_ref[h] = acc_ref[h] + jnp.dot(vt, p, preferred_element_type=F32)

    def attend_slow(jl, boff):
        lo = pl.multiple_of(jl * TKS, TKS)
        c0 = pl.multiple_of(kk * tk + lo, TKS)
        madd = (pltpu.bitcast(keys_ref[pl.ds(c0, TKS), :], F32) - 1.0) * (-NEG)

        def group(g, _):
            kt = k_ref[g, pl.ds(lo, TKS), :]
            vt = vT_ref[g, :, pl.ds(lo, TKS)]
            for r in range(Q_PER_KV):
                h = g * Q_PER_KV + r
                s = jnp.dot(kt, qa_ref[h], preferred_element_type=F32) + madd
                if boff is not None:
                    s = s + nb_ref[h, boff:boff + TKS, :]
                m_prev = m_ref[h]
                m_new = jnp.maximum(m_prev, jnp.max(s, axis=0, keepdims=True))
                alpha = jnp.exp(m_prev - m_new)
                p = jnp.exp(s - m_new[0:1, :])
                acc_ref[h] = alpha[0:1, :] * acc_ref[h] + jnp.dot(
                    vt, p.astype(BF16), preferred_element_type=F32)
                m_ref[h] = m_new
            return 0
        lax.fori_loop(0, KV_HEADS, group, 0)

    n_here = jnp.clip(n_sub - kk * sub_per_tile, 0, sub_per_tile)
    fast = fast_ref[0] == 1

    def sub_loop(attend):
        def sub_body(jl, _):
            u = kk * sub_per_tile + jl

            @pl.when(u < n_sub - 2)
            def _():
                attend(jl, None)

            @pl.when(u == n_sub - 2)
            def _():
                attend(jl, 0)

            @pl.when(u == n_sub - 1)
            def _():
                attend(jl, TKS)
            return 0
        lax.fori_loop(0, n_here, sub_body, 0)

    @pl.when(fast)
    def _():
        sub_loop(attend_fast)

    @pl.when(jnp.logical_not(fast))
    def _():
        sub_loop(attend_slow)

    @pl.when(kk == nk - 1)
    def _():
        outs = []
        for h in range(ATTN_HEADS):
            a = acc_ref[h]
            outs.append(a[0:HEAD_DIM, :] / a[HEAD_DIM:HEAD_DIM + 1, :])
        o = jnp.concatenate(outs, axis=0).T
        o_ref[...] = (o * _silu(ag_ref[...])).astype(o_ref.dtype)


def augment_kv(k4, vT, n_keys):
    s_pad = k4.shape[2]
    real = (jnp.arange(s_pad) < n_keys).astype(BF16)
    k_aug = jnp.concatenate([k4, jnp.broadcast_to(real[None, None, :, None], k4.shape[:3] + (1,)),
                             jnp.zeros(k4.shape[:3] + (AUG_K - HEAD_DIM - 1,), BF16)], axis=-1)
    v_aug = jnp.concatenate([vT, jnp.broadcast_to(real[None, None, None, :], vT.shape[:2] + (1, s_pad)),
                             jnp.zeros(vT.shape[:2] + (V_ROWS - HEAD_DIM - 1, s_pad), BF16)], axis=2)
    return k_aug, v_aug


def dsa_t(qT, iqT, iwT, proj, ik, k_aug, v_aug, bkT, rel_bias, *, nbatch, tq, ta, tk, start, n_keys, topk):
    t = qT.shape[2]
    s_pad = ik.shape[1]
    nq = t // (nbatch * tq)
    nk = s_pad // tk
    assert tq % LANES == 0 and ta % tq == 0 and tk % TKS == 0 and s_pad % ta == 0 and ta % CNT_ROWS == 0
    assert ta % GM_ROWS == 0 and topk <= GM_ROWS <= start + tq

    kf = k_aug[..., :HEAD_DIM].astype(F32)
    kmax = jnp.sqrt(jnp.max(jnp.sum(kf * kf, axis=-1), axis=-1))

    def kt_idx(b, i, kk):
        q0 = start + i * tq
        n_cols = jnp.minimum(((q0 + tq - 1) // CHUNK + 1) * CHUNK, n_keys)
        return jnp.minimum(kk, (n_cols + tk - 1) // tk - 1)

    kern = functools.partial(_dsat_kernel, tq=tq, ta=ta, tk=tk, start=start, n_keys=n_keys, topk=topk)
    smem = pl.BlockSpec(memory_space=pltpu.SMEM)
    return pl.pallas_call(
        kern,
        grid=(nbatch, nq, nk),
        in_specs=[smem, smem,
                  pl.BlockSpec((ATTN_HEADS, HEAD_DIM, tq), lambda b, i, kk: (0, 0, b * nq + i)),
                  pl.BlockSpec((IDX_HEADS, IDX_DIM, tq), lambda b, i, kk: (0, 0, b * nq + i)),
                  pl.BlockSpec((IDX_HEADS, tq), lambda b, i, kk: (0, b * nq + i)),
                  pl.BlockSpec((None, s_pad, IDX_DIM), lambda b, i, kk: (b, 0, 0)),
                  pl.BlockSpec((None, KV_HEADS, tk, AUG_K), lambda b, i, kk: (b, 0, kt_idx(b, i, kk), 0)),
                  pl.BlockSpec((None, KV_HEADS, V_ROWS, tk), lambda b, i, kk: (b, 0, 0, kt_idx(b, i, kk))),
                  pl.BlockSpec((NEAR_W, tq), lambda b, i, kk: (0, 0)),
                  pl.BlockSpec((tq, BRANCH_DIM), lambda b, i, kk: (b * nq + i, COL["ag"] // BRANCH_DIM))],
        out_specs=pl.BlockSpec((tq, BRANCH_DIM), lambda b, i, kk: (b * nq + i, 0)),
        out_shape=jax.ShapeDtypeStruct((t, BRANCH_DIM), BF16),
        scratch_shapes=[pltpu.VMEM((s_pad, tq), I32),
                        pltpu.VMEM((ATTN_HEADS, NEAR_W, tq), F32),
                        pltpu.VMEM((ATTN_HEADS, V_ROWS, tq), F32),
                        pltpu.VMEM((ATTN_HEADS, 8, tq), F32),
                        pltpu.VMEM((ATTN_HEADS, AUG_K, tq), BF16),
                        pltpu.VMEM((8, tq), I32),
                        pltpu.VMEM((GM_ROWS, tq), F32),
                        pltpu.SMEM((1,), F32),
                        pltpu.SMEM((1,), I32)],
        compiler_params=_cparams(("arbitrary", "arbitrary", "arbitrary")),
        name="dsa_t",
    )(rel_bias, kmax, qT, iqT, iwT, ik, k_aug, v_aug, bkT, proj)


def t5_bucket(rel):
    half = N_BUCKETS // 2
    max_exact = half // 2
    ret = jnp.where(rel > 0, half, 0)
    n = jnp.abs(rel)
    nf = jnp.maximum(n, max_exact).astype(jnp.float32)
    large = max_exact + (jnp.log(nf / max_exact) / math.log(MAX_DISTANCE / max_exact)
                         * (half - max_exact)).astype(jnp.int32)
    large = jnp.minimum(large, half - 1)
    return ret + jnp.where(n < max_exact, n, large)


def near_buckets(tq, q0, n_keys):
    vis_end = ((q0 + tq - 1) // CHUNK + 1) * CHUNK
    end = -(-min(vis_end, n_keys) // TKS) * TKS
    assert end - NEAR_W <= q0 - MAX_DISTANCE + 1, "near window must cover every non-saturated offset"
    kpos = end - NEAR_W + jnp.arange(NEAR_W, dtype=jnp.int32)[None, :]
    qpos = q0 + jnp.arange(tq, dtype=jnp.int32)[:, None]
    return t5_bucket(kpos - qpos)


def _ssd_kernel(xs_ref, bm_ref, cm_ref, dt_ref, z_ref, hist_ref, st0_ref, cw_ref, cb_ref,
                dtb_ref, alog_ref, dsk_ref, nw_ref, exp_ref, y_ref, st_ref,
                ext_ref, state_ref, *, q, valid):
    c = pl.program_id(1)
    nc = pl.num_programs(1)
    hp = SSD_INNER
    gw = hp // SSD_GROUPS
    hi = lax.Precision.HIGHEST

    @pl.when(c == 0)
    def _():
        ext_ref[0:8, :] = hist_ref[...]
        state_ref[...] = st0_ref[...]

    ext_ref[8:8 + q, 0:hp] = xs_ref[...]
    ext_ref[8:8 + q, hp:hp + 256] = bm_ref[...]
    ext_ref[8:8 + q, hp + 256:hp + 512] = cm_ref[...]
    conv = jnp.zeros((q, CONV_DIM), F32) + cb_ref[...]
    for j in range(CONV_WIDTH):
        conv = conv + ext_ref[5 + j:5 + j + q, :] * cw_ref[j:j + 1, :]
    ext_ref[0:8, :] = ext_ref[q:q + 8, :]
    conv = _silu(conv)
    xs = conv[:, 0:hp]
    bmat = conv[:, hp:hp + 256]
    cmat = conv[:, hp + 256:hp + 512]

    xdt = dt_ref[...] + dtb_ref[...]
    dt = jnp.maximum(xdt, 0.0) + jnp.log1p(jnp.exp(-jnp.abs(xdt)))
    if valid < q:
        rows = lax.broadcasted_iota(I32, (q, LANES), 0)
        dt = jnp.where(rows < valid, dt, 0.0)
    adt = dt * (-jnp.exp(alog_ref[...]))
    rr = lax.broadcasted_iota(I32, (q, q), 0)
    cc = lax.broadcasted_iota(I32, (q, q), 1)
    tri = rr >= cc
    acum = jnp.dot(tri.astype(F32), adt, precision=hi, preferred_element_type=F32)
    acum_t = acum.T
    dt_t = dt.T
    alast = acum[q - 1:q, :]

    expand = exp_ref[...]
    e_acum = jnp.dot(jnp.exp(acum), expand, precision=hi, preferred_element_type=F32)
    e_tail = jnp.dot(jnp.exp(alast - acum) * dt, expand, precision=hi, preferred_element_type=F32)
    e_last = e_acum[q - 1:q, :]

    xw = (xs * e_tail).astype(BF16)
    xb = xs.astype(BF16)
    y_parts = []
    new_state = []
    for g in range(SSD_GROUPS):
        bg = bmat[:, g * SSD_STATE:(g + 1) * SSD_STATE]
        cg = cmat[:, g * SSD_STATE:(g + 1) * SSD_STATE].astype(BF16)
        bg_t = bg.T.astype(BF16)
        cb = jnp.dot(cg, bg_t, preferred_element_type=F32)
        st_g = state_ref[:, g * gw:(g + 1) * gw]
        y_off = jnp.dot(cg, st_g.astype(BF16), preferred_element_type=F32)
        new_state.append(jnp.dot(bg_t, xw[:, g * gw:(g + 1) * gw], preferred_element_type=F32))
        heads = []
        for r in range(SSD_HEADS // SSD_GROUPS):
            h = g * (SSD_HEADS // SSD_GROUPS) + r
            seg = acum[:, h:h + 1] - acum_t[h:h + 1, :]
            decay = jnp.where(tri, jnp.exp(jnp.where(tri, seg, 0.0)), 0.0)
            wmat = (cb * decay * dt_t[h:h + 1, :]).astype(BF16)
            heads.append(jnp.dot(wmat, xb[:, h * SSD_HEAD_DIM:(h + 1) * SSD_HEAD_DIM],
                                 preferred_element_type=F32))
        y_parts.append(jnp.concatenate(heads, axis=1) + y_off * e_acum[:, g * gw:(g + 1) * gw])
    y = jnp.concatenate(y_parts, axis=1)
    state_ref[...] = state_ref[...] * e_last + jnp.concatenate(new_state, axis=1)

    y = (y + dsk_ref[...] * xs) * _silu(z_ref[...])
    ms = jnp.mean(y * y, axis=-1, keepdims=True)
    y_ref[...] = (y * lax.rsqrt(ms + EPS) * nw_ref[...]).astype(y_ref.dtype)

    @pl.when(c == nc - 1)
    def _():
        st_ref[...] = state_ref[...]


def ssd(proj, hist8, state_t, conv_w, conv_b, dt_bias, a_log, d_skip, ssd_norm_w, *, nbatch, q, valid):
    t = proj.shape[0]
    nc = t // (nbatch * q)
    hp = SSD_INNER

    def pad_heads(v, fill):
        return jnp.concatenate([v.astype(F32), jnp.full((LANES - SSD_HEADS,), fill, F32)]).reshape(1, LANES)

    expand = (jnp.arange(LANES)[:, None] == (jnp.arange(hp)[None, :] // SSD_HEAD_DIM)).astype(F32)
    dsk = jnp.repeat(d_skip.astype(F32), SSD_HEAD_DIM).reshape(1, hp)
    kern = functools.partial(_ssd_kernel, q=q, valid=valid)
    const2 = lambda shape: pl.BlockSpec(shape, lambda b, c: (0, 0))
    return pl.pallas_call(
        kern,
        grid=(nbatch, nc),
        in_specs=[pl.BlockSpec((q, hp), lambda b, c: (b * nc + c, COL["xs"] // hp)),
                  pl.BlockSpec((q, 256), lambda b, c: (b * nc + c, COL["bm"] // 256)),
                  pl.BlockSpec((q, 256), lambda b, c: (b * nc + c, COL["cm"] // 256)),
                  pl.BlockSpec((q, LANES), lambda b, c: (b * nc + c, COL["dt"] // LANES)),
                  pl.BlockSpec((q, hp), lambda b, c: (b * nc + c, COL["z"] // hp)),
                  pl.BlockSpec((None, 8, CONV_DIM), lambda b, c: (b, 0, 0)),
                  pl.BlockSpec((None, SSD_STATE, hp), lambda b, c: (b, 0, 0)),
                  const2((CONV_WIDTH, CONV_DIM)), const2((1, CONV_DIM)),
                  const2((1, LANES)), const2((1, LANES)), const2((1, hp)), const2((1, hp)),
                  const2((LANES, hp))],
        out_specs=[pl.BlockSpec((q, hp), lambda b, c: (b * nc + c, 0)),
                   pl.BlockSpec((None, SSD_STATE, hp), lambda b, c: (b, 0, 0))],
        out_shape=[jax.ShapeDtypeStruct((t, hp), BF16),
                   jax.ShapeDtypeStruct((nbatch, SSD_STATE, hp), F32)],
        scratch_shapes=[pltpu.VMEM((q + 8, CONV_DIM), F32),
                        pltpu.VMEM((SSD_STATE, hp), F32)],
        compiler_params=_cparams(("arbitrary", "arbitrary")),
        name="ssd",
    )(proj, proj, proj, proj, proj, hist8, state_t, conv_w, conv_b.reshape(1, CONV_DIM),
      pad_heads(dt_bias, 0.0), pad_heads(a_log, 0.0), dsk, ssd_norm_w.reshape(1, hp), expand)


def _pool_kernel(u_ref, pg_ref, hist_ref, w_ref, b_ref, sc_ref, y_ref, ext_ref, *, r, start):
    c = pl.program_id(1)

    @pl.when(c == 0)
    def _():
        ext_ref[0:16, :] = hist_ref[...]

    ext_ref[16:16 + r, :] = u_ref[...]
    pos = start + c * r + lax.broadcasted_iota(I32, (r, 1), 0)
    outs = []
    for gi, w in enumerate(POOL_WINDOWS):
        lo = gi * POOL_GROUP_DIM
        cur = ext_ref[16:16 + r, lo:lo + POOL_GROUP_DIM]
        win = cur
        for s in range(1, w):
            win = win + ext_ref[16 - s:16 - s + r, lo:lo + POOL_GROUP_DIM]
        cnt = jnp.minimum(w, pos + 1).astype(F32)
        pooled = win / cnt - cur
        mixed = jnp.dot(pooled.astype(BF16), w_ref[gi], preferred_element_type=F32)
        outs.append(mixed + b_ref[gi:gi + 1, :])
    ext_ref[0:16, :] = ext_ref[r:r + 16, :]
    mixed = jnp.concatenate(outs, axis=1) * sc_ref[...]
    y_ref[...] = (mixed * _silu(pg_ref[...])).astype(y_ref.dtype)


def pool(proj, hist16, pool_w, pool_b, pool_scale, *, nbatch, r, start):
    t = proj.shape[0]
    nc = t // (nbatch * r)
    d = BRANCH_DIM
    kern = functools.partial(_pool_kernel, r=r, start=start)
    return pl.pallas_call(
        kern,
        grid=(nbatch, nc),
        in_specs=[pl.BlockSpec((r, d), lambda b, c: (b * nc + c, COL["u"] // d)),
                  pl.BlockSpec((r, d), lambda b, c: (b * nc + c, COL["pg"] // d)),
                  pl.BlockSpec((None, 16, d), lambda b, c: (b, 0, 0)),
                  pl.BlockSpec((POOL_GROUPS, POOL_GROUP_DIM, POOL_GROUP_DIM), lambda b, c: (0, 0, 0)),
                  pl.BlockSpec((POOL_GROUPS, POOL_GROUP_DIM), lambda b, c: (0, 0)),
                  pl.BlockSpec((1, d), lambda b, c: (0, 0))],
        out_specs=pl.BlockSpec((r, d), lambda b, c: (b * nc + c, 0)),
        out_shape=jax.ShapeDtypeStruct((t, d), BF16),
        scratch_shapes=[pltpu.VMEM((r + 16, d), F32)],
        compiler_params=_cparams(("arbitrary", "arbitrary")),
        name="pool",
    )(proj, proj, hist16, pool_w.astype(BF16), pool_b, pool_scale.reshape(1, d))


def _merge_kernel(y0_ref, y1_ref, y2_ref, g0_ref, g1_ref, g2_ref, w_ref, o_ref):
    acc = jax.nn.sigmoid(g0_ref[...]) * jnp.dot(y0_ref[...], w_ref[0], preferred_element_type=F32)
    acc = acc + jax.nn.sigmoid(g1_ref[...]) * jnp.dot(y1_ref[...], w_ref[1], preferred_element_type=F32)
    acc = acc + jax.nn.sigmoid(g2_ref[...]) * jnp.dot(y2_ref[...], w_ref[2], preferred_element_type=F32)
    o_ref[...] = acc.astype(o_ref.dtype)


def merge(y_ssd, y_att, y_pool, proj, w_branch, tm):
    t = y_ssd.shape[0]
    tn = 1024
    nj = D_MODEL // tn
    ysp = pl.BlockSpec((tm, BRANCH_DIM), lambda i, j: (i, 0))

    def gate_spec(bi):
        return pl.BlockSpec((tm, tn), lambda i, j: (i, (COL["mg"] + bi * D_MODEL) // tn + j))

    return pl.pallas_call(
        _merge_kernel,
        grid=(t // tm, nj),
        in_specs=[ysp, ysp, ysp, gate_spec(0), gate_spec(1), gate_spec(2),
                  pl.BlockSpec((N_BRANCH, BRANCH_DIM, tn), lambda i, j: (0, 0, j))],
        out_specs=pl.BlockSpec((tm, tn), lambda i, j: (i, j)),
        out_shape=jax.ShapeDtypeStruct((t, D_MODEL), BF16),
        compiler_params=_cparams(("arbitrary", "arbitrary")),
        name="merge",
    )(y_ssd, y_att, y_pool, proj, proj, proj, w_branch)


def _outproj_kernel(m_ref, w_ref, x_ref, g_ref, o_ref):
    o_ref[...] = x_ref[...] + g_ref[...] * jnp.dot(m_ref[...], w_ref[...], preferred_element_type=F32)


def outproj(merged, w_out, x, gate, tm, rows_per_mod):
    t = x.shape[0]
    tn = 1024
    return pl.pallas_call(
        _outproj_kernel,
        grid=(t // tm, D_MODEL // tn),
        in_specs=[pl.BlockSpec((tm, D_MODEL), lambda i, j: (i, 0)),
                  pl.BlockSpec((D_MODEL, tn), lambda i, j: (0, j)),
                  pl.BlockSpec((tm, tn), lambda i, j: (i, j)),
                  _mod_spec(gate, tm, tn, rows_per_mod, lambda j: j)],
        out_specs=pl.BlockSpec((tm, tn), lambda i, j: (i, j)),
        out_shape=jax.ShapeDtypeStruct((t, D_MODEL), F32),
        compiler_params=_cparams(("arbitrary", "arbitrary")),
        name="outproj",
    )(merged, w_out, x, gate)


def _pad_to(a, axis, size):
    pad = [(0, 0)] * a.ndim
    pad[axis] = (0, size - a.shape[axis])
    return jnp.pad(a, pad)


def trunk_layer(x, mod, k_past, v_past, ik_past, h0, conv_hist, pool_hist, rel_bias, lw, *, per_row_mod):
    (norm_w, w_in, conv_w, conv_b, dt_bias, a_log, d_skip, ssd_norm_w, q_norm_w, k_norm_w,
     pool_w, pool_b, pool_scale, w_branch, w_out) = lw
    bsz, seq, d = x.shape
    t = bsz * seq
    start = k_past.shape[1]
    n_keys = start + seq
    topk = min(TOPK_MAX, n_keys // 4)
    x2 = x.reshape(t, d)
    shift, scale, gate = mod[:, :d], mod[:, d:2 * d], mod[:, 2 * d:]
    tm = min(512, t)
    if per_row_mod:
        expand = lambda m: jnp.broadcast_to(m[:, None, :], (bsz, seq, d)).reshape(t // tm, tm, d)
    else:
        expand = lambda m: m[:, None, :]
    scale3, shift3, gate3 = expand(scale), expand(shift), expand(gate)

    proj = inproj(x2, scale3, shift3, norm_w, w_in, tm, seq)

    kw = KV_HEADS * HEAD_DIM
    tq = min(256, seq)
    if seq >= 2048:
        tk, ta = 2048, 512
    else:
        tk = ta = None
    s_pad = -(-n_keys // TKS) * TKS
    if tk is None:
        tk, ta = s_pad, s_pad // 3 if (s_pad // 3) % TKS == 0 else s_pad
    s_pad = -(-s_pad // tk) * tk
    bk = near_buckets(tq, start, n_keys)
    dsa_args = dict(nbatch=bsz, tq=tq, ta=ta, tk=tk, start=start, n_keys=n_keys, topk=topk)
    lanes_layout = tq % LANES == 0
    if lanes_layout and start == 0 and s_pad == seq:
        qT, iqT, iwT, k_new, v_new, ik_new, k_aug, v_aug, ik_b = prep_t(proj, q_norm_w, k_norm_w, tm, bsz)
        y_att = dsa_t(qT, iqT, iwT, proj, ik_b, k_aug, v_aug, bk.T, rel_bias, **dsa_args)
    else:
        qh, k_new, iqh = prep(proj, q_norm_w, k_norm_w, tm)
        v_new = proj[:, COL["v"]:COL["v"] + kw]
        ik_new = proj[:, COL["ik"]:COL["ik"] + IDX_DIM]
    k_new4 = k_new.reshape(bsz, seq, KV_HEADS, HEAD_DIM)
    v_new4 = v_new.reshape(bsz, seq, KV_HEADS, HEAD_DIM)
    ik_new3 = ik_new.reshape(bsz, seq, IDX_DIM)
    if not (lanes_layout and start == 0 and s_pad == seq):
        k_all = jnp.concatenate([k_past.astype(F32), k_new4], axis=1).astype(BF16)
        v_all = jnp.concatenate([v_past.astype(F32), v_new4], axis=1).astype(BF16)
        ik_all = jnp.concatenate([ik_past.astype(F32), ik_new3], axis=1).astype(BF16)
        if lanes_layout:
            k4 = _pad_to(jnp.transpose(k_all, (0, 2, 1, 3)), 2, s_pad)
            vT = _pad_to(jnp.transpose(v_all, (0, 2, 3, 1)), 3, s_pad)
            k_aug, v_aug = augment_kv(k4, vT, n_keys)
            ik_p = _pad_to(ik_all, 1, s_pad)
            iwT = proj[:, COL["iw"]:COL["iw"] + IDX_HEADS].T
            y_att = dsa_t(jnp.swapaxes(qh, 1, 2), jnp.swapaxes(iqh, 1, 2), iwT, proj, ik_p, k_aug, v_aug,
                          bk.T, rel_bias, **dsa_args)
        else:
            kT = _pad_to(jnp.transpose(k_all, (0, 2, 3, 1)), 3, s_pad)
            v4 = _pad_to(jnp.transpose(v_all, (0, 2, 1, 3)), 2, s_pad)
            ikT = _pad_to(jnp.transpose(ik_all, (0, 2, 1)), 2, s_pad)
            y_att = dsa(qh, iqh, proj, ikT, kT, v4, bk, rel_bias, **dsa_args)

    q = 128
    hist8 = jnp.pad(conv_hist.astype(F32), ((0, 0), (8 - (CONV_WIDTH - 1), 0), (0, 0)))
    state_t = jnp.transpose(h0.astype(F32), (0, 3, 1, 2)).reshape(bsz, SSD_STATE, SSD_INNER)
    if seq < q:
        proj_ssd = _pad_to(proj.reshape(bsz, seq, PROJ_DIM), 1, q).reshape(bsz * q, PROJ_DIM)
        valid = seq
    else:
        proj_ssd, valid = proj, q
    y_ssd, st = ssd(proj_ssd, hist8, state_t, conv_w, conv_b, dt_bias, a_log, d_skip, ssd_norm_w,
                    nbatch=bsz, q=q, valid=valid)
    if seq < q:
        y_ssd = y_ssd.reshape(bsz, q, SSD_INNER)[:, :seq].reshape(t, SSD_INNER)
    h_last = jnp.transpose(st.reshape(bsz, SSD_STATE, SSD_HEADS, SSD_HEAD_DIM), (0, 2, 3, 1))

    hist16 = jnp.pad(pool_hist.astype(F32), ((0, 0), (16 - POOL_STATE, 0), (0, 0)))
    y_pool = pool(proj, hist16, pool_w, pool_b, pool_scale, nbatch=bsz, r=min(512, seq), start=start)

    merged = merge(y_ssd, y_att, y_pool, proj, w_branch, tm)
    x_new = outproj(merged, w_out, x2, gate3, tm, seq).reshape(bsz, seq, d)

    xbc = proj[:, COL["xbc"]:COL["xbc"] + CONV_DIM].reshape(bsz, seq, CONV_DIM)
    u = proj[:, COL["u"]:COL["u"] + BRANCH_DIM].reshape(bsz, seq, BRANCH_DIM)
    conv_new = jnp.concatenate([conv_hist.astype(F32), xbc], axis=1)[:, -(CONV_WIDTH - 1):]
    pool_new = jnp.concatenate([pool_hist.astype(F32), u[:, -min(seq, POOL_STATE):]], axis=1)[:, -POOL_STATE:]
    return x_new, k_new4, v_new4, ik_new3, h_last, conv_new, pool_new


def _reorder_w_in(w_in):
    parts = []
    for name in _NEW_ORDER:
        off, size = _ORIG[name]
        seg = w_in[..., off:off + size]
        padw = -(-size // LANES) * LANES - size
        if padw:
            seg = jnp.pad(seg, ((0, 0), (0, 0), (0, padw)))
        parts.append(seg)
    out = jnp.concatenate(parts, axis=-1)
    return _pad_to(out, 2, PROJ_DIM).astype(BF16)


def kernel(x_prompt, x_sample, cache_k, cache_v, cache_idx_k, state_ssm, state_conv, state_pool,
           c_prompt, c_sample, rel_bias, w_ada, b_ada, norm_w, w_in, conv_w, conv_b, dt_bias,
           a_log, d_skip, ssd_norm_w, q_norm_w, k_norm_w, pool_w, pool_b, pool_scale,
           w_branch, w_out):
    bp = x_prompt.shape[0]
    f32 = F32
    mods = ada_mod(jnp.concatenate([c_prompt, c_sample], axis=0), w_ada, b_ada)
    w_in_r = _reorder_w_in(w_in)
    w_branch_b = w_branch.astype(BF16)
    w_out_b = w_out.astype(BF16)

    empty_kv = jnp.zeros((bp, 0, KV_HEADS, HEAD_DIM), f32)
    empty_ik = jnp.zeros((bp, 0, IDX_DIM), f32)
    zero_ssm = jnp.zeros((bp, SSD_HEADS, SSD_HEAD_DIM, SSD_STATE), f32)
    zero_conv = jnp.zeros((bp, CONV_WIDTH - 1, CONV_DIM), f32)
    zero_pool = jnp.zeros((bp, POOL_STATE, BRANCH_DIM), f32)

    xp, xs = x_prompt, x_sample
    outs_p = [[] for _ in range(6)]
    outs_s = [[] for _ in range(6)]
    for l in range(DEPTH):
        lw = (norm_w[l], w_in_r[l], conv_w[l], conv_b[l], dt_bias[l], a_log[l], d_skip[l], ssd_norm_w[l],
              q_norm_w[l], k_norm_w[l], pool_w[l], pool_b[l], pool_scale[l], w_branch_b[l], w_out_b[l])
        rp = trunk_layer(xp, mods[l, :bp], empty_kv, empty_kv, empty_ik, zero_ssm, zero_conv, zero_pool,
                         rel_bias, lw, per_row_mod=False)
        rs = trunk_layer(xs, mods[l, bp:], cache_k[l], cache_v[l], cache_idx_k[l], state_ssm[l],
                         state_conv[l], state_pool[l], rel_bias, lw, per_row_mod=True)
        xp, xs = rp[0], rs[0]
        for n in range(6):
            outs_p[n].append(rp[n + 1])
            outs_s[n].append(rs[n + 1])
    return (xp, xs, *[jnp.stack(o) for o in outs_p], *[jnp.stack(o) for o in outs_s])
```

```python
import functools
import math

import numpy as np
import jax
import jax.numpy as jnp
from jax import lax
from jax.experimental import pallas as pl
from jax.experimental.pallas import tpu as pltpu

F32 = jnp.float32
BF16 = jnp.bfloat16
I32 = jnp.int32

D_MODEL = 2048
DEPTH = 4
CHUNK = 64
N_BRANCH = 3
BRANCH_DIM = 1024
SSD_INNER = BRANCH_DIM
SSD_HEAD_DIM = 64
SSD_HEADS = SSD_INNER // SSD_HEAD_DIM
SSD_GROUPS = 2
SSD_STATE = 128
CONV_WIDTH = 4
CONV_DIM = SSD_INNER + 2 * SSD_GROUPS * SSD_STATE
ATTN_HEADS = 16
KV_HEADS = 4
HEAD_DIM = BRANCH_DIM // ATTN_HEADS
Q_PER_KV = ATTN_HEADS // KV_HEADS
IDX_HEADS = 8
IDX_DIM = 64
TOPK_MAX = 256
N_BUCKETS = 32
MAX_DISTANCE = 128
POOL_WINDOWS = (2, 4, 8, 16)
POOL_GROUPS = 4
POOL_GROUP_DIM = BRANCH_DIM // POOL_GROUPS
POOL_STATE = 15
EPS = 1e-6
NEG = -1e30

LANES = 128
VMEM_LIMIT = 56 * 1024 * 1024

_ORIG = {}
_off = 0
for _name, _size in (("z", SSD_INNER), ("xbc", CONV_DIM), ("dt", SSD_HEADS), ("q", BRANCH_DIM),
                     ("k", KV_HEADS * HEAD_DIM), ("v", KV_HEADS * HEAD_DIM), ("ag", BRANCH_DIM),
                     ("iq", IDX_HEADS * IDX_DIM), ("ik", IDX_DIM), ("iw", IDX_HEADS),
                     ("u", BRANCH_DIM), ("pg", BRANCH_DIM), ("mg", N_BRANCH * D_MODEL)):
    _ORIG[_name] = (_off, _size)
    _off += _size
IN_DIM = _off

_NEW_ORDER = ("mg", "z", "q", "ag", "u", "pg", "xbc", "iq", "k", "v", "ik", "dt", "iw")
COL = {}
_off = 0
for _name in _NEW_ORDER:
    COL[_name] = _off
    _off += -(-_ORIG[_name][1] // LANES) * LANES
PROJ_DIM = -(-_off // 2048) * 2048
COL["xs"] = COL["xbc"]
COL["bm"] = COL["xbc"] + SSD_INNER
COL["cm"] = COL["bm"] + SSD_GROUPS * SSD_STATE


def _sortable_const(v):
    i = int(np.float32(v).view(np.int32))
    return i ^ ((i >> 31) & 0x7FFFFFFF)


NEG_KEY = _sortable_const(NEG)
INT_MIN = -2 ** 31


def _cparams(sem):
    return pltpu.CompilerParams(dimension_semantics=sem, vmem_limit_bytes=VMEM_LIMIT)


def _silu(x):
    return x * jax.nn.sigmoid(x)


def _ada_kernel(c_ref, w_ref, b_ref, o_ref):
    c = c_ref[...]
    o_ref[...] = jnp.dot(_silu(c).astype(BF16), w_ref[...].astype(BF16),
                         preferred_element_type=F32) + b_ref[...]


def ada_mod(c_all, w_ada, b_ada):
    nb, d = c_all.shape
    n = w_ada.shape[-1]
    tn = 512
    return pl.pallas_call(
        _ada_kernel,
        grid=(DEPTH, n // tn),
        in_specs=[pl.BlockSpec((nb, d), lambda l, j: (0, 0)),
                  pl.BlockSpec((None, d, tn), lambda l, j: (l, 0, j)),
                  pl.BlockSpec((None, 1, tn), lambda l, j: (l, 0, j))],
        out_specs=pl.BlockSpec((None, nb, tn), lambda l, j: (l, 0, j)),
        out_shape=jax.ShapeDtypeStruct((DEPTH, nb, n), F32),
        compiler_params=_cparams(("arbitrary", "arbitrary")),
        name="ada",
    )(c_all, w_ada, b_ada.reshape(DEPTH, 1, n))


def _inproj_kernel(x_ref, sc_ref, sh_ref, nw_ref, w_ref, o_ref, h_ref):
    @pl.when(pl.program_id(1) == 0)
    def _():
        x = x_ref[...]
        ms = jnp.mean(x * x, axis=-1, keepdims=True)
        y = x * lax.rsqrt(ms + EPS) * nw_ref[...]
        h_ref[...] = (y * (1.0 + sc_ref[...]) + sh_ref[...]).astype(BF16)

    o_ref[...] = jnp.dot(h_ref[...], w_ref[...], preferred_element_type=F32)


def _mod_spec(mod, tm, tn, rows_per_mod, col_of_j):
    r = mod.shape[1]
    if r == 1:
        tiles = rows_per_mod // tm
        return pl.BlockSpec((None, 1, tn), lambda i, j: (i // tiles, 0, col_of_j(j)))
    return pl.BlockSpec((None, r, tn), lambda i, j: (i, 0, col_of_j(j)))


def inproj(x, scale, shift, norm_w, w, tm, rows_per_mod):
    t, d = x.shape
    n = w.shape[1]
    tn = 1024
    zero = lambda j: 0
    return pl.pallas_call(
        _inproj_kernel,
        grid=(t // tm, n // tn),
        in_specs=[pl.BlockSpec((tm, d), lambda i, j: (i, 0)),
                  _mod_spec(scale, tm, d, rows_per_mod, zero),
                  _mod_spec(shift, tm, d, rows_per_mod, zero),
                  pl.BlockSpec((1, d), lambda i, j: (0, 0)),
                  pl.BlockSpec((d, tn), lambda i, j: (0, j))],
        out_specs=pl.BlockSpec((tm, tn), lambda i, j: (i, j)),
        out_shape=jax.ShapeDtypeStruct((t, n), F32),
        scratch_shapes=[pltpu.VMEM((tm, d), BF16)],
        compiler_params=_cparams(("arbitrary", "arbitrary")),
        name="inproj",
    )(x, scale, shift, norm_w.reshape(1, d), w)


def _prep_kernel(q_ref, k_ref, iq_ref, qw_ref, kw_ref, qh_ref, ko_ref, iqh_ref):
    def head_norm(xs, w):
        ms = jnp.mean(xs * xs, axis=-1, keepdims=True)
        return xs * lax.rsqrt(ms + EPS) * w

    q = q_ref[...]
    for h in range(ATTN_HEADS):
        qn = head_norm(q[:, h * HEAD_DIM:(h + 1) * HEAD_DIM], qw_ref[...])
        qh_ref[h] = (qn * HEAD_DIM ** -0.5).astype(BF16)
    k = k_ref[...]
    for g in range(KV_HEADS):
        ko_ref[:, g * HEAD_DIM:(g + 1) * HEAD_DIM] = head_norm(k[:, g * HEAD_DIM:(g + 1) * HEAD_DIM], kw_ref[...])
    iq = iq_ref[...]
    for h in range(IDX_HEADS):
        iqh_ref[h] = (iq[:, h * IDX_DIM:(h + 1) * IDX_DIM] * IDX_DIM ** -0.5).astype(BF16)


def prep(proj, q_norm_w, k_norm_w, tm):
    t = proj.shape[0]
    kw = KV_HEADS * HEAD_DIM
    iqw = IDX_HEADS * IDX_DIM
    return pl.pallas_call(
        _prep_kernel,
        grid=(t // tm,),
        in_specs=[pl.BlockSpec((tm, BRANCH_DIM), lambda i: (i, COL["q"] // BRANCH_DIM)),
                  pl.BlockSpec((tm, kw), lambda i: (i, COL["k"] // kw)),
                  pl.BlockSpec((tm, iqw), lambda i: (i, COL["iq"] // iqw)),
                  pl.BlockSpec((1, HEAD_DIM), lambda i: (0, 0)),
                  pl.BlockSpec((1, HEAD_DIM), lambda i: (0, 0))],
        out_specs=[pl.BlockSpec((ATTN_HEADS, tm, HEAD_DIM), lambda i: (0, i, 0)),
                   pl.BlockSpec((tm, kw), lambda i: (i, 0)),
                   pl.BlockSpec((IDX_HEADS, tm, IDX_DIM), lambda i: (0, i, 0))],
        out_shape=[jax.ShapeDtypeStruct((ATTN_HEADS, t, HEAD_DIM), BF16),
                   jax.ShapeDtypeStruct((t, kw), F32),
                   jax.ShapeDtypeStruct((IDX_HEADS, t, IDX_DIM), BF16)],
        compiler_params=_cparams(("arbitrary",)),
        name="prep",
    )(proj, proj, proj, q_norm_w.reshape(1, HEAD_DIM), k_norm_w.reshape(1, HEAD_DIM))


def _prep_t_kernel(q_ref, k_ref, v_ref, iq_ref, ik_ref, iw_ref, qw_ref, kw_ref,
                   qT_ref, iqT_ref, iwT_ref, ko_ref, vo_ref, iko_ref, ka_ref, va_ref, ikb_ref):
    def head_norm(xs, w):
        ms = jnp.mean(xs * xs, axis=-1, keepdims=True)
        return xs * lax.rsqrt(ms + EPS) * w

    tm = q_ref.shape[0]
    q = q_ref[...]
    qn = jnp.concatenate([head_norm(q[:, h * HEAD_DIM:(h + 1) * HEAD_DIM], qw_ref[...])
                          for h in range(ATTN_HEADS)], axis=1)
    qT_ref[...] = (qn * HEAD_DIM ** -0.5).T.reshape(ATTN_HEADS, HEAD_DIM, tm).astype(BF16)
    iqT_ref[...] = (iq_ref[...] * IDX_DIM ** -0.5).T.reshape(IDX_HEADS, IDX_DIM, tm).astype(BF16)
    iwT_ref[...] = iw_ref[...].T[0:IDX_HEADS, :]

    k = k_ref[...]
    lane = lax.broadcasted_iota(I32, (tm, AUG_K), 1)
    for g in range(KV_HEADS):
        kn = head_norm(k[:, g * HEAD_DIM:(g + 1) * HEAD_DIM], kw_ref[...])
        ko_ref[:, g * HEAD_DIM:(g + 1) * HEAD_DIM] = kn
        kpad = jnp.concatenate([kn, jnp.zeros((tm, AUG_K - HEAD_DIM), F32)], axis=1)
        ka_ref[g] = jnp.where(lane == HEAD_DIM, 1.0, kpad).astype(BF16)

    v = v_ref[...]
    vo_ref[...] = v
    vt = v.T
    row = lax.broadcasted_iota(I32, (V_ROWS - HEAD_DIM, tm), 0)
    tail = jnp.where(row == 0, 1.0, 0.0).astype(BF16)
    for g in range(KV_HEADS):
        va_ref[g, 0:HEAD_DIM, :] = vt[g * HEAD_DIM:(g + 1) * HEAD_DIM, :].astype(BF16)
        va_ref[g, HEAD_DIM:V_ROWS, :] = tail

    ik = ik_ref[...][:, 0:IDX_DIM]
    iko_ref[...] = ik
    ikb_ref[...] = ik.astype(BF16)


def prep_t(proj, q_norm_w, k_norm_w, tm, nbatch):
    t = proj.shape[0]
    s = t // nbatch
    tiles = s // tm
    kw = KV_HEADS * HEAD_DIM
    iqw = IDX_HEADS * IDX_DIM
    row = lambda w, name: pl.BlockSpec((tm, w), lambda i: (i, COL[name] // w))
    return pl.pallas_call(
        _prep_t_kernel,
        grid=(t // tm,),
        in_specs=[row(BRANCH_DIM, "q"), row(kw, "k"), row(kw, "v"), row(iqw, "iq"),
                  row(LANES, "ik"), row(LANES, "iw"),
                  pl.BlockSpec((1, HEAD_DIM), lambda i: (0, 0)),
                  pl.BlockSpec((1, HEAD_DIM), lambda i: (0, 0))],
        out_specs=[pl.BlockSpec((ATTN_HEADS, HEAD_DIM, tm), lambda i: (0, 0, i)),
                   pl.BlockSpec((IDX_HEADS, IDX_DIM, tm), lambda i: (0, 0, i)),
                   pl.BlockSpec((IDX_HEADS, tm), lambda i: (0, i)),
                   pl.BlockSpec((tm, kw), lambda i: (i, 0)),
                   pl.BlockSpec((tm, kw), lambda i: (i, 0)),
                   pl.BlockSpec((tm, IDX_DIM), lambda i: (i, 0)),
                   pl.BlockSpec((None, KV_HEADS, tm, AUG_K), lambda i: (i // tiles, 0, i % tiles, 0)),
                   pl.BlockSpec((None, KV_HEADS, V_ROWS, tm), lambda i: (i // tiles, 0, 0, i % tiles)),
                   pl.BlockSpec((None, tm, IDX_DIM), lambda i: (i // tiles, i % tiles, 0))],
        out_shape=[jax.ShapeDtypeStruct((ATTN_HEADS, HEAD_DIM, t), BF16),
                   jax.ShapeDtypeStruct((IDX_HEADS, IDX_DIM, t), BF16),
                   jax.ShapeDtypeStruct((IDX_HEADS, t), F32),
                   jax.ShapeDtypeStruct((t, kw), F32),
                   jax.ShapeDtypeStruct((t, kw), F32),
                   jax.ShapeDtypeStruct((t, IDX_DIM), F32),
                   jax.ShapeDtypeStruct((nbatch, KV_HEADS, s, AUG_K), BF16),
                   jax.ShapeDtypeStruct((nbatch, KV_HEADS, V_ROWS, s), BF16),
                   jax.ShapeDtypeStruct((nbatch, s, IDX_DIM), BF16)],
        compiler_params=_cparams(("arbitrary",)),
        name="prep_t",
    )(proj, proj, proj, proj, proj, proj, q_norm_w.reshape(1, HEAD_DIM), k_norm_w.reshape(1, HEAD_DIM))


TKS = 256
NEAR_W = 2 * TKS
FAR_BUCKET = N_BUCKETS // 2 - 1
J_ALL = 2 ** 30


def _dsa_kernel(tab_ref, qh_ref, iqh_ref, iw_ref, ikT_ref, kT_ref, v_ref, bk_ref, ag_ref,
                o_ref, keys_ref, nb_ref, acc_ref, m_ref, l_ref, j_ref, *,
                tq, ta, tk, start, n_keys, topk):
    b = pl.program_id(0)
    i = pl.program_id(1)
    kk = pl.program_id(2)
    nk = pl.num_programs(2)
    q0 = start + i * tq
    vis_end = ((q0 + tq - 1) // CHUNK + 1) * CHUNK
    n_cols = jnp.minimum(vis_end, n_keys)
    n_a = (n_cols + ta - 1) // ta
    n_sub = (n_cols + TKS - 1) // TKS
    extra = jnp.maximum(n_keys - n_a * ta, 0).astype(F32)
    sub_per_tile = tk // TKS

    def vis_mask(t, width):
        kpos = t * width + lax.broadcasted_iota(I32, (tq, width), 1)
        qpos = q0 + lax.broadcasted_iota(I32, (tq, width), 0)
        vis = (kpos // CHUNK) <= (qpos // CHUNK)
        return vis, kpos

    @pl.when((b == 0) & (i == 0) & (kk == 0))
    def _():
        bk = bk_ref[...]

        def per_head(h, _):
            def per_bucket(n, val):
                return jnp.where(bk == n, tab_ref[n, h], val)
            val = lax.fori_loop(0, N_BUCKETS, per_bucket, jnp.zeros((tq, NEAR_W), F32))
            nb_ref[h] = val - tab_ref[FAR_BUCKET, h]
            return 0
        lax.fori_loop(0, ATTN_HEADS, per_head, 0)

    @pl.when(kk == 0)
    def _():
        iw = iw_ref[...] * IDX_HEADS ** -0.5

        def score_tile(t, _):
            c0 = pl.multiple_of(t * ta, ta)
            ikt = ikT_ref[:, pl.ds(c0, ta)]
            acc = jnp.zeros((tq, ta), F32)
            for h in range(IDX_HEADS):
                s = jnp.dot(iqh_ref[h], ikt, preferred_element_type=F32)
                acc = acc + iw[:, h:h + 1] * jnp.maximum(s, 0.0)
            vis, kpos = vis_mask(t, ta)
            acc = jnp.where(vis, acc, NEG)
            bits = pltpu.bitcast(acc, I32)
            key = bits ^ ((bits >> 31) & 0x7FFFFFFF)
            key = jnp.where(kpos < n_keys, key, INT_MIN)
            keys_ref[:, pl.ds(c0, ta)] = key
            return 0
        lax.fori_loop(0, n_a, score_tile, 0)

        def count(pred):
            def body(t, cnt):
                c0 = pl.multiple_of(t * ta, ta)
                blk = keys_ref[:, pl.ds(c0, ta)]
                c = jnp.where(pred(blk, t), 1.0, 0.0)
                for jj in range(ta // LANES):
                    cnt = cnt + c[:, jj * LANES:(jj + 1) * LANES]
                return cnt
            cnt = lax.fori_loop(0, n_a, body, jnp.zeros((tq, LANES), F32))
            return jnp.sum(cnt, axis=1, keepdims=True)

        def bit_step(it, prefix_u):
            bit = jnp.left_shift(jnp.int32(1), 31 - it)
            cand_u = prefix_u | bit
            cand_s = cand_u ^ INT_MIN
            cnt = count(lambda blk, t: blk >= cand_s) + jnp.where(NEG_KEY >= cand_s, extra, 0.0)
            return jnp.where(cnt >= topk, cand_u, prefix_u)
        prefix_u = lax.fori_loop(0, 32, bit_step, jnp.zeros((tq, 1), I32))
        thr = prefix_u ^ INT_MIN

        cnt_gt = count(lambda blk, t: blk > thr) + jnp.where(NEG_KEY > thr, extra, 0.0)
        cnt_eq = count(lambda blk, t: blk == thr)
        need = topk - cnt_gt
        j_ref[...] = jnp.full((tq, LANES), J_ALL, I32)

        @pl.when(jnp.max(cnt_eq - need) > 0.0)
        def _():
            def idx_step(it, jmax):
                cand = jmax | jnp.left_shift(jnp.int32(1), 14 - it)

                def pred(blk, t):
                    _, kpos = vis_mask(t, ta)
                    return (blk == thr) & (kpos < cand)
                g = count(pred)
                return jnp.where(g < need, cand, jmax)
            jmax = lax.fori_loop(0, 15, idx_step, jnp.zeros((tq, 1), I32))
            j_ref[...] = jnp.broadcast_to(jmax, (tq, LANES))

        jmax = j_ref[:, 0:1]

        def mask_tile(t, _):
            c0 = pl.multiple_of(t * ta, ta)
            blk = keys_ref[:, pl.ds(c0, ta)]
            vis, kpos = vis_mask(t, ta)
            sel = (blk > thr) | ((blk == thr) & (kpos <= jmax))
            ok = sel & vis & (kpos < n_keys)
            madd = jnp.where(ok, 0.0, NEG).astype(F32)
            keys_ref[:, pl.ds(c0, ta)] = pltpu.bitcast(madd, I32)
            return 0
        lax.fori_loop(0, n_a, mask_tile, 0)

        m_ref[...] = jnp.full(m_ref.shape, -jnp.inf, F32)
        l_ref[...] = jnp.zeros(l_ref.shape, F32)
        acc_ref[...] = jnp.zeros(acc_ref.shape, F32)

    def attend(jl, boff):
        lo = pl.multiple_of(jl * TKS, TKS)
        c0 = pl.multiple_of(kk * tk + lo, TKS)
        madd = pltpu.bitcast(keys_ref[:, pl.ds(c0, TKS)], F32)

        logits = [madd + jnp.dot(qh_ref[h], kT_ref[h // Q_PER_KV, :, pl.ds(lo, TKS)],
                                 preferred_element_type=F32) for h in range(ATTN_HEADS)]
        for h in range(ATTN_HEADS):
            s = logits[h]
            if boff is not None:
                s = s + nb_ref[h, :, boff:boff + TKS]
            m_prev = m_ref[h]
            m_new = jnp.maximum(m_prev, jnp.max(s, axis=1, keepdims=True))
            alpha = jnp.exp(m_prev - m_new)
            p = jnp.exp(s - m_new[:, 0:1])
            l_ref[h] = alpha * l_ref[h] + jnp.sum(p, axis=1, keepdims=True)
            acc_ref[h] = alpha[:, 0:HEAD_DIM] * acc_ref[h] + jnp.dot(
                p.astype(BF16), v_ref[h // Q_PER_KV, pl.ds(lo, TKS), :], preferred_element_type=F32)
            m_ref[h] = m_new

    n_here = jnp.clip(n_sub - kk * sub_per_tile, 0, sub_per_tile)

    def sub_body(jl, _):
        u = kk * sub_per_tile + jl

        @pl.when(u < n_sub - 2)
        def _():
            attend(jl, None)

        @pl.when(u == n_sub - 2)
        def _():
            attend(jl, 0)

        @pl.when(u == n_sub - 1)
        def _():
            attend(jl, TKS)
        return 0
    lax.fori_loop(0, n_here, sub_body, 0)

    @pl.when(kk == nk - 1)
    def _():
        ag = ag_ref[...]
        for h in range(ATTN_HEADS):
            o = acc_ref[h] / l_ref[h][:, 0:HEAD_DIM]
            sl = slice(h * HEAD_DIM, (h + 1) * HEAD_DIM)
            o_ref[:, sl] = (o * _silu(ag[:, sl])).astype(o_ref.dtype)


def dsa(qh, iqh, proj, ikT, kT, v4, bk, rel_bias, *, nbatch, tq, ta, tk, start, n_keys, topk):
    t = qh.shape[1]
    s_pad = ikT.shape[-1]
    nq = t // (nbatch * tq)
    nk = s_pad // tk

    def kt_idx(b, i, kk):
        q0 = start + i * tq
        n_cols = jnp.minimum(((q0 + tq - 1) // CHUNK + 1) * CHUNK, n_keys)
        return jnp.minimum(kk, (n_cols + tk - 1) // tk - 1)

    kern = functools.partial(_dsa_kernel, tq=tq, ta=ta, tk=tk, start=start, n_keys=n_keys, topk=topk)
    return pl.pallas_call(
        kern,
        grid=(nbatch, nq, nk),
        in_specs=[pl.BlockSpec(memory_space=pltpu.SMEM),
                  pl.BlockSpec((ATTN_HEADS, tq, HEAD_DIM), lambda b, i, kk: (0, b * nq + i, 0)),
                  pl.BlockSpec((IDX_HEADS, tq, IDX_DIM), lambda b, i, kk: (0, b * nq + i, 0)),
                  pl.BlockSpec((tq, LANES), lambda b, i, kk: (b * nq + i, COL["iw"] // LANES)),
                  pl.BlockSpec((None, IDX_DIM, s_pad), lambda b, i, kk: (b, 0, 0)),
                  pl.BlockSpec((None, KV_HEADS, HEAD_DIM, tk), lambda b, i, kk: (b, 0, 0, kt_idx(b, i, kk))),
                  pl.BlockSpec((None, KV_HEADS, tk, HEAD_DIM), lambda b, i, kk: (b, 0, kt_idx(b, i, kk), 0)),
                  pl.BlockSpec((tq, NEAR_W), lambda b, i, kk: (0, 0)),
                  pl.BlockSpec((tq, BRANCH_DIM), lambda b, i, kk: (b * nq + i, COL["ag"] // BRANCH_DIM))],
        out_specs=pl.BlockSpec((tq, BRANCH_DIM), lambda b, i, kk: (b * nq + i, 0)),
        out_shape=jax.ShapeDtypeStruct((t, BRANCH_DIM), BF16),
        scratch_shapes=[pltpu.VMEM((tq, s_pad), I32),
                        pltpu.VMEM((ATTN_HEADS, tq, NEAR_W), F32),
                        pltpu.VMEM((ATTN_HEADS, tq, HEAD_DIM), F32),
                        pltpu.VMEM((ATTN_HEADS, tq, LANES), F32),
                        pltpu.VMEM((ATTN_HEADS, tq, LANES), F32),
                        pltpu.VMEM((tq, LANES), I32)],
        compiler_params=_cparams(("arbitrary", "arbitrary", "arbitrary")),
        name="dsa",
    )(rel_bias, qh, iqh, proj, ikT, kT, v4, bk, proj)


AUG_K = 128
V_ROWS = 80
CNT_ROWS = 64
FAST_BOUND = 30.0
GM_ROWS = 256
QK_LOOKAHEAD = 8


def _dsat_kernel(tab_ref, kmax_ref, qT_ref, iqT_ref, iw_ref, ik_ref, k_ref, vT_ref, bk_ref, ag_ref,
                 o_ref, keys_ref, nb_ref, acc_ref, m_ref, qa_ref, j_ref, gm_ref, bmax_ref, fast_ref, *,
                 tq, ta, tk, start, n_keys, topk):
    b = pl.program_id(0)
    i = pl.program_id(1)
    kk = pl.program_id(2)
    nk = pl.num_programs(2)
    q0 = start + i * tq
    vis_end = ((q0 + tq - 1) // CHUNK + 1) * CHUNK
    n_cols = jnp.minimum(vis_end, n_keys)
    n_a = (n_cols + ta - 1) // ta
    n_sub = (n_cols + TKS - 1) // TKS
    extra = jnp.maximum(n_keys - n_a * ta, 0).astype(F32)
    sub_per_tile = tk // TKS

    def key_pos(t):
        return t * ta + lax.broadcasted_iota(I32, (ta, tq), 0)

    def vis_of(kpos):
        qpos = q0 + lax.broadcasted_iota(I32, (ta, tq), 1)
        return (kpos >> 6) <= (qpos >> 6)

    def colsum(x):
        return jnp.sum(x.reshape(ta // CNT_ROWS, CNT_ROWS, tq), axis=0)

    @pl.when((b == 0) & (i == 0) & (kk == 0))
    def _():
        bk = bk_ref[...]

        def per_head(h, bmax):
            def per_bucket(n, val):
                return jnp.where(bk == n, tab_ref[n, h], val)
            val = lax.fori_loop(0, N_BUCKETS, per_bucket, jnp.zeros((NEAR_W, tq), F32))
            val = val - tab_ref[FAR_BUCKET, h]
            nb_ref[h] = val
            return jnp.maximum(bmax, jnp.max(val))
        bmax_ref[0] = lax.fori_loop(0, ATTN_HEADS, per_head, jnp.float32(0.0))

    @pl.when(kk == 0)
    def _():
        rows = lax.broadcasted_iota(I32, (AUG_K, tq), 0)
        worst = jnp.zeros((1, tq), F32)
        for h in range(ATTN_HEADS):
            q = qT_ref[h].astype(F32)
            nrm = jnp.sqrt(jnp.sum(q * q, axis=0, keepdims=True))
            bound = nrm * kmax_ref[b, h // Q_PER_KV] + bmax_ref[0]
            worst = jnp.maximum(worst, bound)
            qpad = jnp.concatenate([q, jnp.zeros((AUG_K - HEAD_DIM, tq), F32)], axis=0)
            qa_ref[h] = jnp.where(rows == HEAD_DIM, -bound, qpad).astype(BF16)
        fast_ref[0] = (jnp.max(worst) <= FAST_BOUND).astype(I32)

        iw = iw_ref[...] * IDX_HEADS ** -0.5

        def score_tile(t, edge):
            c0 = pl.multiple_of(t * ta, ta)
            ikt = ik_ref[pl.ds(c0, ta), :]
            acc = jnp.zeros((ta, tq), F32)
            for h in range(IDX_HEADS):
                s = jnp.dot(ikt, iqT_ref[h], preferred_element_type=F32)
                acc = acc + iw[h:h + 1, :] * jnp.maximum(s, 0.0)
            if edge:
                kpos = key_pos(t)
                acc = jnp.where(vis_of(kpos), acc, NEG)
            bits = pltpu.bitcast(acc, I32)
            key = bits ^ ((bits >> 31) & 0x7FFFFFFF)
            if edge:
                key = jnp.where(kpos < n_keys, key, INT_MIN)
                acc = jnp.where(kpos < n_keys, acc, -jnp.inf)
            keys_ref[pl.ds(c0, ta), :] = key
            gm_ref[...] = jnp.maximum(gm_ref[...], jnp.max(acc.reshape(ta // GM_ROWS, GM_ROWS, tq), axis=0))
        gm_ref[...] = jnp.full((GM_ROWS, tq), -jnp.inf, F32)
        lax.fori_loop(0, n_a - 1, lambda t, _: (score_tile(t, False), 0)[1], 0)
        score_tile(n_a - 1, True)

        def count(pred):
            def body(t, cnt):
                c0 = pl.multiple_of(t * ta, ta)
                blk = keys_ref[pl.ds(c0, ta), :]
                return cnt + colsum(jnp.where(pred(blk, t), 1.0, 0.0))
            cnt = lax.fori_loop(0, n_a, body, jnp.zeros((CNT_ROWS, tq), F32))
            return jnp.sum(cnt, axis=0, keepdims=True)

        def count_ge(p):
            return count(lambda blk, t: blk >= p) + jnp.where(NEG_KEY >= p, extra, 0.0)

        def key_of(f):
            bits = pltpu.bitcast(f, I32)
            return bits ^ ((bits >> 31) & 0x7FFFFFFF)

        gm = gm_ref[...]
        lo0 = key_of(jnp.min(gm, axis=0, keepdims=True))
        hi0 = key_of(jnp.max(gm, axis=0, keepdims=True)) + 1
        c_lo0 = jnp.full((1, tq), float(2 ** 24), F32)

        def active_of(lo, hi, c_lo):
            return (c_lo != topk) & (lo + 1 < hi)

        def search_cond(st):
            return st[5] > 0.0

        def search_body(st):
            lo, hi, c_lo, c_hi, it, _ = st
            mid = (lo >> 1) + (hi >> 1) + (lo & hi & 1)
            zero_pivot = (it < 2) & (lo < it) & (it < hi)
            p = jnp.where(zero_pivot, it, mid)
            p = jnp.minimum(jnp.maximum(p, lo + 1), hi - 1)
            c = count_ge(p)
            act = active_of(lo, hi, c_lo)
            up = act & (c >= topk)
            dn = act & (c < topk)
            lo, c_lo = jnp.where(up, p, lo), jnp.where(up, c, c_lo)
            hi, c_hi = jnp.where(dn, p, hi), jnp.where(dn, c, c_hi)
            left = jnp.max(jnp.where(active_of(lo, hi, c_lo), 1.0, 0.0))
            return lo, hi, c_lo, c_hi, it + 1, left

        st0 = (lo0, hi0, c_lo0, jnp.zeros((1, tq), F32), jnp.int32(0),
               jnp.max(jnp.where(active_of(lo0, hi0, c_lo0), 1.0, 0.0)))
        thr, _, c_thr, _, _, _ = lax.while_loop(search_cond, search_body, st0)
        j_ref[...] = jnp.full((8, tq), J_ALL, I32)

        @pl.when(jnp.max(c_thr) > topk)
        def _():
            cnt_gt = count(lambda blk, t: blk > thr) + jnp.where(NEG_KEY > thr, extra, 0.0)
            need = topk - cnt_gt

            def idx_step(it, jmax):
                cand = jmax | jnp.left_shift(jnp.int32(1), 14 - it)
                g = count(lambda blk, t: (blk == thr) & (key_pos(t) < cand))
                return jnp.where(g < need, cand, jmax)
            jmax = lax.fori_loop(0, 15, idx_step, jnp.zeros((1, tq), I32))
            j_ref[...] = jnp.broadcast_to(jmax, (8, tq))

        jmax = j_ref[0:1, :]

        def mask_tile(t, edge):
            c0 = pl.multiple_of(t * ta, ta)
            blk = keys_ref[pl.ds(c0, ta), :]
            kpos = key_pos(t)
            ok = (blk > thr) | ((blk == thr) & (kpos <= jmax))
            if edge:
                ok = ok & vis_of(kpos) & (kpos < n_keys)
            keys_ref[pl.ds(c0, ta), :] = pltpu.bitcast(jnp.where(ok, 1.0, 0.0).astype(F32), I32)
        lax.fori_loop(0, n_a - 1, lambda t, _: (mask_tile(t, False), 0)[1], 0)
        mask_tile(n_a - 1, True)

        m_ref[...] = jnp.full(m_ref.shape, -jnp.inf, F32)
        acc_ref[...] = jnp.zeros(acc_ref.shape, F32)

    def attend_fast(jl, boff):
        lo = pl.multiple_of(jl * TKS, TKS)
        c0 = pl.multiple_of(kk * tk + lo, TKS)
        mask = pltpu.bitcast(keys_ref[pl.ds(c0, TKS), :], F32)

        def logits(h):
            kt = k_ref[h // Q_PER_KV, pl.ds(lo, TKS), :]
            return jnp.dot(kt, qa_ref[h], preferred_element_type=F32)

        pending = [logits(h) for h in range(QK_LOOKAHEAD)]
        for h in range(ATTN_HEADS):
            if h + QK_LOOKAHEAD < ATTN_HEADS:
                pending.append(logits(h + QK_LOOKAHEAD))
            s = pending.pop(0)
            if boff is not None:
                s = nb_ref[h, boff:boff + TKS, :] + s
            p = (jnp.exp(s) * mask).astype(BF16)
            vt = vT_ref[h // Q_PER_KV, :, pl.ds(lo, TKS)]
            acc_ref[h] = acc_ref[h] + jnp.dot(vt, p, preferred_element_type=F32)

    def attend_slow(jl, boff):
        lo = pl.multiple_of(jl * TKS, TKS)
        c0 = pl.multiple_of(kk * tk + lo, TKS)
        madd = (pltpu.bitcast(keys_ref[pl.ds(c0, TKS), :], F32) - 1.0) * (-NEG)

        def group(g, _):
            kt = k_ref[g, pl.ds(lo, TKS), :]
            vt = vT_ref[g, :, pl.ds(lo, TKS)]
            for r in range(Q_PER_KV):
                h = g * Q_PER_KV + r
                s = jnp.dot(kt, qa_ref[h], preferred_element_type=F32) + madd
                if boff is not None:
                    s = s + nb_ref[h, boff:boff + TKS, :]
                m_prev = m_ref[h]
                m_new = jnp.maximum(m_prev, jnp.max(s, axis=0, keepdims=True))
                alpha = jnp.exp(m_prev - m_new)
                p = jnp.exp(s - m_new[0:1, :])
                acc_ref[h] = alpha[0:1, :] * acc_ref[h] + jnp.dot(
                    vt, p.astype(BF16), preferred_element_type=F32)
                m_ref[h] = m_new
            return 0
        lax.fori_loop(0, KV_HEADS, group, 0)

    n_here = jnp.clip(n_sub - kk * sub_per_tile, 0, sub_per_tile)
    fast = fast_ref[0] == 1

    def sub_loop(attend):
        def sub_body(jl, _):
            u = kk * sub_per_tile + jl

            @pl.when(u < n_sub - 2)
            def _():
                attend(jl, None)

            @pl.when(u == n_sub - 2)
            def _():
                attend(jl, 0)

            @pl.when(u == n_sub - 1)
            def _():
                attend(jl, TKS)
            return 0
        lax.fori_loop(0, n_here, sub_body, 0)

    @pl.when(fast)
    def _():
        sub_loop(attend_fast)

    @pl.when(jnp.logical_not(fast))
    def _():
        sub_loop(attend_slow)

    @pl.when(kk == nk - 1)
    def _():
        outs = []
        for h in range(ATTN_HEADS):
            a = acc_ref[h]
            outs.append(a[0:HEAD_DIM, :] / a[HEAD_DIM:HEAD_DIM + 1, :])
        o = jnp.concatenate(outs, axis=0).T
        o_ref[...] = (o * _silu(ag_ref[...])).astype(o_ref.dtype)


def augment_kv(k4, vT, n_keys):
    s_pad = k4.shape[2]
    real = (jnp.arange(s_pad) < n_keys).astype(BF16)
    k_aug = jnp.concatenate([k4, jnp.broadcast_to(real[None, None, :, None], k4.shape[:3] + (1,)),
                             jnp.zeros(k4.shape[:3] + (AUG_K - HEAD_DIM - 1,), BF16)], axis=-1)
    v_aug = jnp.concatenate([vT, jnp.broadcast_to(real[None, None, None, :], vT.shape[:2] + (1, s_pad)),
                             jnp.zeros(vT.shape[:2] + (V_ROWS - HEAD_DIM - 1, s_pad), BF16)], axis=2)
    return k_aug, v_aug


def dsa_t(qT, iqT, iwT, proj, ik, k_aug, v_aug, bkT, rel_bias, *, nbatch, tq, ta, tk, start, n_keys, topk):
    t = qT.shape[2]
    s_pad = ik.shape[1]
    nq = t // (nbatch * tq)
    nk = s_pad // tk
    assert tq % LANES == 0 and ta % tq == 0 and tk % TKS == 0 and s_pad % ta == 0 and ta % CNT_ROWS == 0
    assert ta % GM_ROWS == 0 and topk <= GM_ROWS <= start + tq

    kf = k_aug[..., :HEAD_DIM].astype(F32)
    kmax = jnp.sqrt(jnp.max(jnp.sum(kf * kf, axis=-1), axis=-1))

    def kt_idx(b, i, kk):
        q0 = start + i * tq
        n_cols = jnp.minimum(((q0 + tq - 1) // CHUNK + 1) * CHUNK, n_keys)
        return jnp.minimum(kk, (n_cols + tk - 1) // tk - 1)

    kern = functools.partial(_dsat_kernel, tq=tq, ta=ta, tk=tk, start=start, n_keys=n_keys, topk=topk)
    smem = pl.BlockSpec(memory_space=pltpu.SMEM)
    return pl.pallas_call(
        kern,
        grid=(nbatch, nq, nk),
        in_specs=[smem, smem,
                  pl.BlockSpec((ATTN_HEADS, HEAD_DIM, tq), lambda b, i, kk: (0, 0, b * nq + i)),
                  pl.BlockSpec((IDX_HEADS, IDX_DIM, tq), lambda b, i, kk: (0, 0, b * nq + i)),
                  pl.BlockSpec((IDX_HEADS, tq), lambda b, i, kk: (0, b * nq + i)),
                  pl.BlockSpec((None, s_pad, IDX_DIM), lambda b, i, kk: (b, 0, 0)),
                  pl.BlockSpec((None, KV_HEADS, tk, AUG_K), lambda b, i, kk: (b, 0, kt_idx(b, i, kk), 0)),
                  pl.BlockSpec((None, KV_HEADS, V_ROWS, tk), lambda b, i, kk: (b, 0, 0, kt_idx(b, i, kk))),
                  pl.BlockSpec((NEAR_W, tq), lambda b, i, kk: (0, 0)),
                  pl.BlockSpec((tq, BRANCH_DIM), lambda b, i, kk: (b * nq + i, COL["ag"] // BRANCH_DIM))],
        out_specs=pl.BlockSpec((tq, BRANCH_DIM), lambda b, i, kk: (b * nq + i, 0)),
        out_shape=jax.ShapeDtypeStruct((t, BRANCH_DIM), BF16),
        scratch_shapes=[pltpu.VMEM((s_pad, tq), I32),
                        pltpu.VMEM((ATTN_HEADS, NEAR_W, tq), F32),
                        pltpu.VMEM((ATTN_HEADS, V_ROWS, tq), F32),
                        pltpu.VMEM((ATTN_HEADS, 8, tq), F32),
                        pltpu.VMEM((ATTN_HEADS, AUG_K, tq), BF16),
                        pltpu.VMEM((8, tq), I32),
                        pltpu.VMEM((GM_ROWS, tq), F32),
                        pltpu.SMEM((1,), F32),
                        pltpu.SMEM((1,), I32)],
        compiler_params=_cparams(("arbitrary", "arbitrary", "arbitrary")),
        name="dsa_t",
    )(rel_bias, kmax, qT, iqT, iwT, ik, k_aug, v_aug, bkT, proj)


def t5_bucket(rel):
    half = N_BUCKETS // 2
    max_exact = half // 2
    ret = jnp.where(rel > 0, half, 0)
    n = jnp.abs(rel)
    nf = jnp.maximum(n, max_exact).astype(jnp.float32)
    large = max_exact + (jnp.log(nf / max_exact) / math.log(MAX_DISTANCE / max_exact)
                         * (half - max_exact)).astype(jnp.int32)
    large = jnp.minimum(large, half - 1)
    return ret + jnp.where(n < max_exact, n, large)


def near_buckets(tq, q0, n_keys):
    vis_end = ((q0 + tq - 1) // CHUNK + 1) * CHUNK
    end = -(-min(vis_end, n_keys) // TKS) * TKS
    assert end - NEAR_W <= q0 - MAX_DISTANCE + 1, "near window must cover every non-saturated offset"
    kpos = end - NEAR_W + jnp.arange(NEAR_W, dtype=jnp.int32)[None, :]
    qpos = q0 + jnp.arange(tq, dtype=jnp.int32)[:, None]
    return t5_bucket(kpos - qpos)


def _ssd_kernel(xs_ref, bm_ref, cm_ref, dt_ref, z_ref, hist_ref, st0_ref, cw_ref, cb_ref,
                dtb_ref, alog_ref, dsk_ref, nw_ref, exp_ref, y_ref, st_ref,
                ext_ref, state_ref, *, q, valid):
    c = pl.program_id(1)
    nc = pl.num_programs(1)
    hp = SSD_INNER
    gw = hp // SSD_GROUPS
    hi = lax.Precision.HIGHEST

    @pl.when(c == 0)
    def _():
        ext_ref[0:8, :] = hist_ref[...]
        state_ref[...] = st0_ref[...]

    ext_ref[8:8 + q, 0:hp] = xs_ref[...]
    ext_ref[8:8 + q, hp:hp + 256] = bm_ref[...]
    ext_ref[8:8 + q, hp + 256:hp + 512] = cm_ref[...]
    conv = jnp.zeros((q, CONV_DIM), F32) + cb_ref[...]
    for j in range(CONV_WIDTH):
        conv = conv + ext_ref[5 + j:5 + j + q, :] * cw_ref[j:j + 1, :]
    ext_ref[0:8, :] = ext_ref[q:q + 8, :]
    conv = _silu(conv)
    xs = conv[:, 0:hp]
    bmat = conv[:, hp:hp + 256]
    cmat = conv[:, hp + 256:hp + 512]

    xdt = dt_ref[...] + dtb_ref[...]
    dt = jnp.maximum(xdt, 0.0) + jnp.log1p(jnp.exp(-jnp.abs(xdt)))
    if valid < q:
        rows = lax.broadcasted_iota(I32, (q, LANES), 0)
        dt = jnp.where(rows < valid, dt, 0.0)
    adt = dt * (-jnp.exp(alog_ref[...]))
    rr = lax.broadcasted_iota(I32, (q, q), 0)
    cc = lax.broadcasted_iota(I32, (q, q), 1)
    tri = rr >= cc
    acum = jnp.dot(tri.astype(F32), adt, precision=hi, preferred_element_type=F32)
    acum_t = acum.T
    dt_t = dt.T
    alast = acum[q - 1:q, :]

    expand = exp_ref[...]
    e_acum = jnp.dot(jnp.exp(acum), expand, precision=hi, preferred_element_type=F32)
    e_tail = jnp.dot(jnp.exp(alast - acum) * dt, expand, precision=hi, preferred_element_type=F32)
    e_last = e_acum[q - 1:q, :]

    xw = (xs * e_tail).astype(BF16)
    xb = xs.astype(BF16)
    y_parts = []
    new_state = []
    for g in range(SSD_GROUPS):
        bg = bmat[:, g * SSD_STATE:(g + 1) * SSD_STATE]
        cg = cmat[:, g * SSD_STATE:(g + 1) * SSD_STATE].astype(BF16)
        bg_t = bg.T.astype(BF16)
        cb = jnp.dot(cg, bg_t, preferred_element_type=F32)
        st_g = state_ref[:, g * gw:(g + 1) * gw]
        y_off = jnp.dot(cg, st_g.astype(BF16), preferred_element_type=F32)
        new_state.append(jnp.dot(bg_t, xw[:, g * gw:(g + 1) * gw], preferred_element_type=F32))
        heads = []
        for r in range(SSD_HEADS // SSD_GROUPS):
            h = g * (SSD_HEADS // SSD_GROUPS) + r
            seg = acum[:, h:h + 1] - acum_t[h:h + 1, :]
            decay = jnp.where(tri, jnp.exp(jnp.where(tri, seg, 0.0)), 0.0)
            wmat = (cb * decay * dt_t[h:h + 1, :]).astype(BF16)
            heads.append(jnp.dot(wmat, xb[:, h * SSD_HEAD_DIM:(h + 1) * SSD_HEAD_DIM],
                                 preferred_element_type=F32))
        y_parts.append(jnp.concatenate(heads, axis=1) + y_off * e_acum[:, g * gw:(g + 1) * gw])
    y = jnp.concatenate(y_parts, axis=1)
    state_ref[...] = state_ref[...] * e_last + jnp.concatenate(new_state, axis=1)

    y = (y + dsk_ref[...] * xs) * _silu(z_ref[...])
    ms = jnp.mean(y * y, axis=-1, keepdims=True)
    y_ref[...] = (y * lax.rsqrt(ms + EPS) * nw_ref[...]).astype(y_ref.dtype)

    @pl.when(c == nc - 1)
    def _():
        st_ref[...] = state_ref[...]


def ssd(proj, hist8, state_t, conv_w, conv_b, dt_bias, a_log, d_skip, ssd_norm_w, *, nbatch, q, valid):
    t = proj.shape[0]
    nc = t // (nbatch * q)
    hp = SSD_INNER

    def pad_heads(v, fill):
        return jnp.concatenate([v.astype(F32), jnp.full((LANES - SSD_HEADS,), fill, F32)]).reshape(1, LANES)

    expand = (jnp.arange(LANES)[:, None] == (jnp.arange(hp)[None, :] // SSD_HEAD_DIM)).astype(F32)
    dsk = jnp.repeat(d_skip.astype(F32), SSD_HEAD_DIM).reshape(1, hp)
    kern = functools.partial(_ssd_kernel, q=q, valid=valid)
    const2 = lambda shape: pl.BlockSpec(shape, lambda b, c: (0, 0))
    return pl.pallas_call(
        kern,
        grid=(nbatch, nc),
        in_specs=[pl.BlockSpec((q, hp), lambda b, c: (b * nc + c, COL["xs"] // hp)),
                  pl.BlockSpec((q, 256), lambda b, c: (b * nc + c, COL["bm"] // 256)),
                  pl.BlockSpec((q, 256), lambda b, c: (b * nc + c, COL["cm"] // 256)),
                  pl.BlockSpec((q, LANES), lambda b, c: (b * nc + c, COL["dt"] // LANES)),
                  pl.BlockSpec((q, hp), lambda b, c: (b * nc + c, COL["z"] // hp)),
                  pl.BlockSpec((None, 8, CONV_DIM), lambda b, c: (b, 0, 0)),
                  pl.BlockSpec((None, SSD_STATE, hp), lambda b, c: (b, 0, 0)),
                  const2((CONV_WIDTH, CONV_DIM)), const2((1, CONV_DIM)),
                  const2((1, LANES)), const2((1, LANES)), const2((1, hp)), const2((1, hp)),
                  const2((LANES, hp))],
        out_specs=[pl.BlockSpec((q, hp), lambda b, c: (b * nc + c, 0)),
                   pl.BlockSpec((None, SSD_STATE, hp), lambda b, c: (b, 0, 0))],
        out_shape=[jax.ShapeDtypeStruct((t, hp), BF16),
                   jax.ShapeDtypeStruct((nbatch, SSD_STATE, hp), F32)],
        scratch_shapes=[pltpu.VMEM((q + 8, CONV_DIM), F32),
                        pltpu.VMEM((SSD_STATE, hp), F32)],
        compiler_params=_cparams(("arbitrary", "arbitrary")),
        name="ssd",
    )(proj, proj, proj, proj, proj, hist8, state_t, conv_w, conv_b.reshape(1, CONV_DIM),
      pad_heads(dt_bias, 0.0), pad_heads(a_log, 0.0), dsk, ssd_norm_w.reshape(1, hp), expand)


def _pool_kernel(u_ref, pg_ref, hist_ref, w_ref, b_ref, sc_ref, y_ref, ext_ref, *, r, start):
    c = pl.program_id(1)

    @pl.when(c == 0)
    def _():
        ext_ref[0:16, :] = hist_ref[...]

    ext_ref[16:16 + r, :] = u_ref[...]
    pos = start + c * r + lax.broadcasted_iota(I32, (r, 1), 0)
    outs = []
    for gi, w in enumerate(POOL_WINDOWS):
        lo = gi * POOL_GROUP_DIM
        cur = ext_ref[16:16 + r, lo:lo + POOL_GROUP_DIM]
        win = cur
        for s in range(1, w):
            win = win + ext_ref[16 - s:16 - s + r, lo:lo + POOL_GROUP_DIM]
        cnt = jnp.minimum(w, pos + 1).astype(F32)
        pooled = win / cnt - cur
        mixed = jnp.dot(pooled.astype(BF16), w_ref[gi], preferred_element_type=F32)
        outs.append(mixed + b_ref[gi:gi + 1, :])
    ext_ref[0:16, :] = ext_ref[r:r + 16, :]
    mixed = jnp.concatenate(outs, axis=1) * sc_ref[...]
    y_ref[...] = (mixed * _silu(pg_ref[...])).astype(y_ref.dtype)


def pool(proj, hist16, pool_w, pool_b, pool_scale, *, nbatch, r, start):
    t = proj.shape[0]
    nc = t // (nbatch * r)
    d = BRANCH_DIM
    kern = functools.partial(_pool_kernel, r=r, start=start)
    return pl.pallas_call(
        kern,
        grid=(nbatch, nc),
        in_specs=[pl.BlockSpec((r, d), lambda b, c: (b * nc + c, COL["u"] // d)),
                  pl.BlockSpec((r, d), lambda b, c: (b * nc + c, COL["pg"] // d)),
                  pl.BlockSpec((None, 16, d), lambda b, c: (b, 0, 0)),
                  pl.BlockSpec((POOL_GROUPS, POOL_GROUP_DIM, POOL_GROUP_DIM), lambda b, c: (0, 0, 0)),
                  pl.BlockSpec((POOL_GROUPS, POOL_GROUP_DIM), lambda b, c: (0, 0)),
                  pl.BlockSpec((1, d), lambda b, c: (0, 0))],
        out_specs=pl.BlockSpec((r, d), lambda b, c: (b * nc + c, 0)),
        out_shape=jax.ShapeDtypeStruct((t, d), BF16),
        scratch_shapes=[pltpu.VMEM((r + 16, d), F32)],
        compiler_params=_cparams(("arbitrary", "arbitrary")),
        name="pool",
    )(proj, proj, hist16, pool_w.astype(BF16), pool_b, pool_scale.reshape(1, d))


def _merge_kernel(y0_ref, y1_ref, y2_ref, g0_ref, g1_ref, g2_ref, w_ref, o_ref):
    acc = jax.nn.sigmoid(g0_ref[...]) * jnp.dot(y0_ref[...], w_ref[0], preferred_element_type=F32)
    acc = acc + jax.nn.sigmoid(g1_ref[...]) * jnp.dot(y1_ref[...], w_ref[1], preferred_element_type=F32)
    acc = acc + jax.nn.sigmoid(g2_ref[...]) * jnp.dot(y2_ref[...], w_ref[2], preferred_element_type=F32)
    o_ref[...] = acc.astype(o_ref.dtype)


def merge(y_ssd, y_att, y_pool, proj, w_branch, tm):
    t = y_ssd.shape[0]
    tn = 1024
    nj = D_MODEL // tn
    ysp = pl.BlockSpec((tm, BRANCH_DIM), lambda i, j: (i, 0))

    def gate_spec(bi):
        return pl.BlockSpec((tm, tn), lambda i, j: (i, (COL["mg"] + bi * D_MODEL) // tn + j))

    return pl.pallas_call(
        _merge_kernel,
        grid=(t // tm, nj),
        in_specs=[ysp, ysp, ysp, gate_spec(0), gate_spec(1), gate_spec(2),
                  pl.BlockSpec((N_BRANCH, BRANCH_DIM, tn), lambda i, j: (0, 0, j))],
        out_specs=pl.BlockSpec((tm, tn), lambda i, j: (i, j)),
        out_shape=jax.ShapeDtypeStruct((t, D_MODEL), BF16),
        compiler_params=_cparams(("arbitrary", "arbitrary")),
        name="merge",
    )(y_ssd, y_att, y_pool, proj, proj, proj, w_branch)


def _outproj_kernel(m_ref, w_ref, x_ref, g_ref, o_ref):
    o_ref[...] = x_ref[...] + g_ref[...] * jnp.dot(m_ref[...], w_ref[...], preferred_element_type=F32)


def outproj(merged, w_out, x, gate, tm, rows_per_mod):
    t = x.shape[0]
    tn = 1024
    return pl.pallas_call(
        _outproj_kernel,
        grid=(t // tm, D_MODEL // tn),
        in_specs=[pl.BlockSpec((tm, D_MODEL), lambda i, j: (i, 0)),
                  pl.BlockSpec((D_MODEL, tn), lambda i, j: (0, j)),
                  pl.BlockSpec((tm, tn), lambda i, j: (i, j)),
                  _mod_spec(gate, tm, tn, rows_per_mod, lambda j: j)],
        out_specs=pl.BlockSpec((tm, tn), lambda i, j: (i, j)),
        out_shape=jax.ShapeDtypeStruct((t, D_MODEL), F32),
        compiler_params=_cparams(("arbitrary", "arbitrary")),
        name="outproj",
    )(merged, w_out, x, gate)


def _pad_to(a, axis, size):
    pad = [(0, 0)] * a.ndim
    pad[axis] = (0, size - a.shape[axis])
    return jnp.pad(a, pad)


def trunk_layer(x, mod, k_past, v_past, ik_past, h0, conv_hist, pool_hist, rel_bias, lw, *, per_row_mod):
    (norm_w, w_in, conv_w, conv_b, dt_bias, a_log, d_skip, ssd_norm_w, q_norm_w, k_norm_w,
     pool_w, pool_b, pool_scale, w_branch, w_out) = lw
    bsz, seq, d = x.shape
    t = bsz * seq
    start = k_past.shape[1]
    n_keys = start + seq
    topk = min(TOPK_MAX, n_keys // 4)
    x2 = x.reshape(t, d)
    shift, scale, gate = mod[:, :d], mod[:, d:2 * d], mod[:, 2 * d:]
    tm = min(512, t)
    if per_row_mod:
        expand = lambda m: jnp.broadcast_to(m[:, None, :], (bsz, seq, d)).reshape(t // tm, tm, d)
    else:
        expand = lambda m: m[:, None, :]
    scale3, shift3, gate3 = expand(scale), expand(shift), expand(gate)

    proj = inproj(x2, scale3, shift3, norm_w, w_in, tm, seq)

    kw = KV_HEADS * HEAD_DIM
    tq = min(256, seq)
    if seq >= 2048:
        tk, ta = 2048, 512
    else:
        tk = ta = None
    s_pad = -(-n_keys // TKS) * TKS
    if tk is None:
        tk, ta = s_pad, s_pad // 3 if (s_pad // 3) % TKS == 0 else s_pad
    s_pad = -(-s_pad // tk) * tk
    bk = near_buckets(tq, start, n_keys)
    dsa_args = dict(nbatch=bsz, tq=tq, ta=ta, tk=tk, start=start, n_keys=n_keys, topk=topk)
    lanes_layout = tq % LANES == 0
    if lanes_layout and start == 0 and s_pad == seq:
        qT, iqT, iwT, k_new, v_new, ik_new, k_aug, v_aug, ik_b = prep_t(proj, q_norm_w, k_norm_w, tm, bsz)
        y_att = dsa_t(qT, iqT, iwT, proj, ik_b, k_aug, v_aug, bk.T, rel_bias, **dsa_args)
    else:
        qh, k_new, iqh = prep(proj, q_norm_w, k_norm_w, tm)
        v_new = proj[:, COL["v"]:COL["v"] + kw]
        ik_new = proj[:, COL["ik"]:COL["ik"] + IDX_DIM]
    k_new4 = k_new.reshape(bsz, seq, KV_HEADS, HEAD_DIM)
    v_new4 = v_new.reshape(bsz, seq, KV_HEADS, HEAD_DIM)
    ik_new3 = ik_new.reshape(bsz, seq, IDX_DIM)
    if not (lanes_layout and start == 0 and s_pad == seq):
        k_all = jnp.concatenate([k_past.astype(F32), k_new4], axis=1).astype(BF16)
        v_all = jnp.concatenate([v_past.astype(F32), v_new4], axis=1).astype(BF16)
        ik_all = jnp.concatenate([ik_past.astype(F32), ik_new3], axis=1).astype(BF16)
        if lanes_layout:
            k4 = _pad_to(jnp.transpose(k_all, (0, 2, 1, 3)), 2, s_pad)
            vT = _pad_to(jnp.transpose(v_all, (0, 2, 3, 1)), 3, s_pad)
            k_aug, v_aug = augment_kv(k4, vT, n_keys)
            ik_p = _pad_to(ik_all, 1, s_pad)
            iwT = proj[:, COL["iw"]:COL["iw"] + IDX_HEADS].T
            y_att = dsa_t(jnp.swapaxes(qh, 1, 2), jnp.swapaxes(iqh, 1, 2), iwT, proj, ik_p, k_aug, v_aug,
                          bk.T, rel_bias, **dsa_args)
        else:
            kT = _pad_to(jnp.transpose(k_all, (0, 2, 3, 1)), 3, s_pad)
            v4 = _pad_to(jnp.transpose(v_all, (0, 2, 1, 3)), 2, s_pad)
            ikT = _pad_to(jnp.transpose(ik_all, (0, 2, 1)), 2, s_pad)
            y_att = dsa(qh, iqh, proj, ikT, kT, v4, bk, rel_bias, **dsa_args)

    q = 128
    hist8 = jnp.pad(conv_hist.astype(F32), ((0, 0), (8 - (CONV_WIDTH - 1), 0), (0, 0)))
    state_t = jnp.transpose(h0.astype(F32), (0, 3, 1, 2)).reshape(bsz, SSD_STATE, SSD_INNER)
    if seq < q:
        proj_ssd = _pad_to(proj.reshape(bsz, seq, PROJ_DIM), 1, q).reshape(bsz * q, PROJ_DIM)
        valid = seq
    else:
        proj_ssd, valid = proj, q
    y_ssd, st = ssd(proj_ssd, hist8, state_t, conv_w, conv_b, dt_bias, a_log, d_skip, ssd_norm_w,
                    nbatch=bsz, q=q, valid=valid)
    if seq < q:
        y_ssd = y_ssd.reshape(bsz, q, SSD_INNER)[:, :seq].reshape(t, SSD_INNER)
    h_last = jnp.transpose(st.reshape(bsz, SSD_STATE, SSD_HEADS, SSD_HEAD_DIM), (0, 2, 3, 1))

    hist16 = jnp.pad(pool_hist.astype(F32), ((0, 0), (16 - POOL_STATE, 0), (0, 0)))
    y_pool = pool(proj, hist16, pool_w, pool_b, pool_scale, nbatch=bsz, r=min(512, seq), start=start)

    merged = merge(y_ssd, y_att, y_pool, proj, w_branch, tm)
    x_new = outproj(merged, w_out, x2, gate3, tm, seq).reshape(bsz, seq, d)

    xbc = proj[:, COL["xbc"]:COL["xbc"] + CONV_DIM].reshape(bsz, seq, CONV_DIM)
    u = proj[:, COL["u"]:COL["u"] + BRANCH_DIM].reshape(bsz, seq, BRANCH_DIM)
    conv_new = jnp.concatenate([conv_hist.astype(F32), xbc], axis=1)[:, -(CONV_WIDTH - 1):]
    pool_new = jnp.concatenate([pool_hist.astype(F32), u[:, -min(seq, POOL_STATE):]], axis=1)[:, -POOL_STATE:]
    return x_new, k_new4, v_new4, ik_new3, h_last, conv_new, pool_new


def _reorder_w_in(w_in):
    parts = []
    for name in _NEW_ORDER:
        off, size = _ORIG[name]
        seg = w_in[..., off:off + size]
        padw = -(-size // LANES) * LANES - size
        if padw:
            seg = jnp.pad(seg, ((0, 0), (0, 0), (0, padw)))
        parts.append(seg)
    out = jnp.concatenate(parts, axis=-1)
    return _pad_to(out, 2, PROJ_DIM).astype(BF16)


def kernel(x_prompt, x_sample, cache_k, cache_v, cache_idx_k, state_ssm, state_conv, state_pool,
           c_prompt, c_sample, rel_bias, w_ada, b_ada, norm_w, w_in, conv_w, conv_b, dt_bias,
           a_log, d_skip, ssd_norm_w, q_norm_w, k_norm_w, pool_w, pool_b, pool_scale,
           w_branch, w_out):
    bp = x_prompt.shape[0]
    f32 = F32
    mods = ada_mod(jnp.concatenate([c_prompt, c_sample], axis=0), w_ada, b_ada)
    w_in_r = _reorder_w_in(w_in)
    w_branch_b = w_branch.astype(BF16)
    w_out_b = w_out.astype(BF16)

    empty_kv = jnp.zeros((bp, 0, KV_HEADS, HEAD_DIM), f32)
    empty_ik = jnp.zeros((bp, 0, IDX_DIM), f32)
    zero_ssm = jnp.zeros((bp, SSD_HEADS, SSD_HEAD_DIM, SSD_STATE), f32)
    zero_conv = jnp.zeros((bp, CONV_WIDTH - 1, CONV_DIM), f32)
    zero_pool = jnp.zeros((bp, POOL_STATE, BRANCH_DIM), f32)

    xp, xs = x_prompt, x_sample
    outs_p = [[] for _ in range(6)]
    outs_s = [[] for _ in range(6)]
    for l in range(DEPTH):
        lw = (norm_w[l], w_in_r[l], conv_w[l], conv_b[l], dt_bias[l], a_log[l], d_skip[l], ssd_norm_w[l],
              q_norm_w[l], k_norm_w[l], pool_w[l], pool_b[l], pool_scale[l], w_branch_b[l], w_out_b[l])
        rp = trunk_layer(xp, mods[l, :bp], empty_kv, empty_kv, empty_ik, zero_ssm, zero_conv, zero_pool,
                         rel_bias, lw, per_row_mod=False)
        rs = trunk_layer(xs, mods[l, bp:], cache_k[l], cache_v[l], cache_idx_k[l], state_ssm[l],
                         state_conv[l], state_pool[l], rel_bias, lw, per_row_mod=True)
        xp, xs = rp[0], rs[0]
        for n in range(6):
            outs_p[n].append(rp[n + 1])
            outs_s[n].append(rs[n + 1])
    return (xp, xs, *[jnp.stack(o) for o in outs_p], *[jnp.stack(o) for o in outs_s])
```

```python
import functools
import math

import numpy as np
import jax
import jax.numpy as jnp
from jax import lax
from jax.experimental import pallas as pl
from jax.experimental.pallas import tpu as pltpu

F32 = jnp.float32
BF16 = jnp.bfloat16
I32 = jnp.int32

D_MODEL = 2048
DEPTH = 4
CHUNK = 64
N_BRANCH = 3
BRANCH_DIM = 1024
SSD_INNER = BRANCH_DIM
SSD_HEAD_DIM = 64
SSD_HEADS = SSD_INNER // SSD_HEAD_DIM
SSD_GROUPS = 2
SSD_STATE = 128
CONV_WIDTH = 4
CONV_DIM = SSD_INNER + 2 * SSD_GROUPS * SSD_STATE
ATTN_HEADS = 16
KV_HEADS = 4
HEAD_DIM = BRANCH_DIM // ATTN_HEADS
Q_PER_KV = ATTN_HEADS // KV_HEADS
IDX_HEADS = 8
IDX_DIM = 64
TOPK_MAX = 256
N_BUCKETS = 32
MAX_DISTANCE = 128
POOL_WINDOWS = (2, 4, 8, 16)
POOL_GROUPS = 4
POOL_GROUP_DIM = BRANCH_DIM // POOL_GROUPS
POOL_STATE = 15
EPS = 1e-6
NEG = -1e30

LANES = 128
VMEM_LIMIT = 56 * 1024 * 1024

_ORIG = {}
_off = 0
for _name, _size in (("z", SSD_INNER), ("xbc", CONV_DIM), ("dt", SSD_HEADS), ("q", BRANCH_DIM),
                     ("k", KV_HEADS * HEAD_DIM), ("v", KV_HEADS * HEAD_DIM), ("ag", BRANCH_DIM),
                     ("iq", IDX_HEADS * IDX_DIM), ("ik", IDX_DIM), ("iw", IDX_HEADS),
                     ("u", BRANCH_DIM), ("pg", BRANCH_DIM), ("mg", N_BRANCH * D_MODEL)):
    _ORIG[_name] = (_off, _size)
    _off += _size
IN_DIM = _off

_NEW_ORDER = ("mg", "z", "q", "ag", "u", "pg", "xbc", "iq", "k", "v", "ik", "dt", "iw")
COL = {}
_off = 0
for _name in _NEW_ORDER:
    COL[_name] = _off
    _off += -(-_ORIG[_name][1] // LANES) * LANES
PROJ_DIM = -(-_off // 2048) * 2048
COL["xs"] = COL["xbc"]
COL["bm"] = COL["xbc"] + SSD_INNER
COL["cm"] = COL["bm"] + SSD_GROUPS * SSD_STATE


def _sortable_const(v):
    i = int(np.float32(v).view(np.int32))
    return i ^ ((i >> 31) & 0x7FFFFFFF)


NEG_KEY = _sortable_const(NEG)
INT_MIN = -2 ** 31


def _cparams(sem):
    return pltpu.CompilerParams(dimension_semantics=sem, vmem_limit_bytes=VMEM_LIMIT)


def _silu(x):
    return x * jax.nn.sigmoid(x)


def _ada_kernel(c_ref, w_ref, b_ref, o_ref):
    c = c_ref[...]
    o_ref[...] = jnp.dot(_silu(c).astype(BF16), w_ref[...].astype(BF16),
                         preferred_element_type=F32) + b_ref[...]


def ada_mod(c_all, w_ada, b_ada):
    nb, d = c_all.shape
    n = w_ada.shape[-1]
    tn = 512
    return pl.pallas_call(
        _ada_kernel,
        grid=(DEPTH, n // tn),
        in_specs=[pl.BlockSpec((nb, d), lambda l, j: (0, 0)),
                  pl.BlockSpec((None, d, tn), lambda l, j: (l, 0, j)),
                  pl.BlockSpec((None, 1, tn), lambda l, j: (l, 0, j))],
        out_specs=pl.BlockSpec((None, nb, tn), lambda l, j: (l, 0, j)),
        out_shape=jax.ShapeDtypeStruct((DEPTH, nb, n), F32),
        compiler_params=_cparams(("arbitrary", "arbitrary")),
        name="ada",
    )(c_all, w_ada, b_ada.reshape(DEPTH, 1, n))


def _inproj_kernel(x_ref, sc_ref, sh_ref, nw_ref, w_ref, o_ref, h_ref):
    @pl.when(pl.program_id(1) == 0)
    def _():
        x = x_ref[...]
        ms = jnp.mean(x * x, axis=-1, keepdims=True)
        y = x * lax.rsqrt(ms + EPS) * nw_ref[...]
        h_ref[...] = (y * (1.0 + sc_ref[...]) + sh_ref[...]).astype(BF16)

    o_ref[...] = jnp.dot(h_ref[...], w_ref[...], preferred_element_type=F32)


def _mod_spec(mod, tm, tn, rows_per_mod, col_of_j):
    r = mod.shape[1]
    if r == 1:
        tiles = rows_per_mod // tm
        return pl.BlockSpec((None, 1, tn), lambda i, j: (i // tiles, 0, col_of_j(j)))
    return pl.BlockSpec((None, r, tn), lambda i, j: (i, 0, col_of_j(j)))


def inproj(x, scale, shift, norm_w, w, tm, rows_per_mod):
    t, d = x.shape
    n = w.shape[1]
    tn = 1024
    zero = lambda j: 0
    return pl.pallas_call(
        _inproj_kernel,
        grid=(t // tm, n // tn),
        in_specs=[pl.BlockSpec((tm, d), lambda i, j: (i, 0)),
                  _mod_spec(scale, tm, d, rows_per_mod, zero),
                  _mod_spec(shift, tm, d, rows_per_mod, zero),
                  pl.BlockSpec((1, d), lambda i, j: (0, 0)),
                  pl.BlockSpec((d, tn), lambda i, j: (0, j))],
        out_specs=pl.BlockSpec((tm, tn), lambda i, j: (i, j)),
        out_shape=jax.ShapeDtypeStruct((t, n), F32),
        scratch_shapes=[pltpu.VMEM((tm, d), BF16)],
        compiler_params=_cparams(("arbitrary", "arbitrary")),
        name="inproj",
    )(x, scale, shift, norm_w.reshape(1, d), w)


def _prep_kernel(q_ref, k_ref, iq_ref, qw_ref, kw_ref, qh_ref, ko_ref, iqh_ref):
    def head_norm(xs, w):
        ms = jnp.mean(xs * xs, axis=-1, keepdims=True)
        return xs * lax.rsqrt(ms + EPS) * w

    q = q_ref[...]
    for h in range(ATTN_HEADS):
        qn = head_norm(q[:, h * HEAD_DIM:(h + 1) * HEAD_DIM], qw_ref[...])
        qh_ref[h] = (qn * HEAD_DIM ** -0.5).astype(BF16)
    k = k_ref[...]
    for g in range(KV_HEADS):
        ko_ref[:, g * HEAD_DIM:(g + 1) * HEAD_DIM] = head_norm(k[:, g * HEAD_DIM:(g + 1) * HEAD_DIM], kw_ref[...])
    iq = iq_ref[...]
    for h in range(IDX_HEADS):
        iqh_ref[h] = (iq[:, h * IDX_DIM:(h + 1) * IDX_DIM] * IDX_DIM ** -0.5).astype(BF16)


def prep(proj, q_norm_w, k_norm_w, tm):
    t = proj.shape[0]
    kw = KV_HEADS * HEAD_DIM
    iqw = IDX_HEADS * IDX_DIM
    return pl.pallas_call(
        _prep_kernel,
        grid=(t // tm,),
        in_specs=[pl.BlockSpec((tm, BRANCH_DIM), lambda i: (i, COL["q"] // BRANCH_DIM)),
                  pl.BlockSpec((tm, kw), lambda i: (i, COL["k"] // kw)),
                  pl.BlockSpec((tm, iqw), lambda i: (i, COL["iq"] // iqw)),
                  pl.BlockSpec((1, HEAD_DIM), lambda i: (0, 0)),
                  pl.BlockSpec((1, HEAD_DIM), lambda i: (0, 0))],
        out_specs=[pl.BlockSpec((ATTN_HEADS, tm, HEAD_DIM), lambda i: (0, i, 0)),
                   pl.BlockSpec((tm, kw), lambda i: (i, 0)),
                   pl.BlockSpec((IDX_HEADS, tm, IDX_DIM), lambda i: (0, i, 0))],
        out_shape=[jax.ShapeDtypeStruct((ATTN_HEADS, t, HEAD_DIM), BF16),
                   jax.ShapeDtypeStruct((t, kw), F32),
                   jax.ShapeDtypeStruct((IDX_HEADS, t, IDX_DIM), BF16)],
        compiler_params=_cparams(("arbitrary",)),
        name="prep",
    )(proj, proj, proj, q_norm_w.reshape(1, HEAD_DIM), k_norm_w.reshape(1, HEAD_DIM))


def _prep_t_kernel(q_ref, k_ref, v_ref, iq_ref, ik_ref, iw_ref, qw_ref, kw_ref,
                   qT_ref, iqT_ref, iwT_ref, ko_ref, vo_ref, iko_ref, ka_ref, va_ref, ikb_ref):
    def head_norm(xs, w):
        ms = jnp.mean(xs * xs, axis=-1, keepdims=True)
        return xs * lax.rsqrt(ms + EPS) * w

    tm = q_ref.shape[0]
    q = q_ref[...]
    qn = jnp.concatenate([head_norm(q[:, h * HEAD_DIM:(h + 1) * HEAD_DIM], qw_ref[...])
                          for h in range(ATTN_HEADS)], axis=1)
    qT_ref[...] = (qn * HEAD_DIM ** -0.5).T.reshape(ATTN_HEADS, HEAD_DIM, tm).astype(BF16)
    iqT_ref[...] = (iq_ref[...] * IDX_DIM ** -0.5).T.reshape(IDX_HEADS, IDX_DIM, tm).astype(BF16)
    iwT_ref[...] = iw_ref[...].T[0:IDX_HEADS, :]

    k = k_ref[...]
    lane = lax.broadcasted_iota(I32, (tm, AUG_K), 1)
    for g in range(KV_HEADS):
        kn = head_norm(k[:, g * HEAD_DIM:(g + 1) * HEAD_DIM], kw_ref[...])
        ko_ref[:, g * HEAD_DIM:(g + 1) * HEAD_DIM] = kn
        kpad = jnp.concatenate([kn, jnp.zeros((tm, AUG_K - HEAD_DIM), F32)], axis=1)
        ka_ref[g] = jnp.where(lane == HEAD_DIM, 1.0, kpad).astype(BF16)

    v = v_ref[...]
    vo_ref[...] = v
    vt = v.T
    row = lax.broadcasted_iota(I32, (V_ROWS - HEAD_DIM, tm), 0)
    tail = jnp.where(row == 0, 1.0, 0.0).astype(BF16)
    for g in range(KV_HEADS):
        va_ref[g, 0:HEAD_DIM, :] = vt[g * HEAD_DIM:(g + 1) * HEAD_DIM, :].astype(BF16)
        va_ref[g, HEAD_DIM:V_ROWS, :] = tail

    ik = ik_ref[...][:, 0:IDX_DIM]
    iko_ref[...] = ik
    ikb_ref[...] = ik.astype(BF16)


def prep_t(proj, q_norm_w, k_norm_w, tm, nbatch):
    t = proj.shape[0]
    s = t // nbatch
    tiles = s // tm
    kw = KV_HEADS * HEAD_DIM
    iqw = IDX_HEADS * IDX_DIM
    row = lambda w, name: pl.BlockSpec((tm, w), lambda i: (i, COL[name] // w))
    return pl.pallas_call(
        _prep_t_kernel,
        grid=(t // tm,),
        in_specs=[row(BRANCH_DIM, "q"), row(kw, "k"), row(kw, "v"), row(iqw, "iq"),
                  row(LANES, "ik"), row(LANES, "iw"),
                  pl.BlockSpec((1, HEAD_DIM), lambda i: (0, 0)),
                  pl.BlockSpec((1, HEAD_DIM), lambda i: (0, 0))],
        out_specs=[pl.BlockSpec((ATTN_HEADS, HEAD_DIM, tm), lambda i: (0, 0, i)),
                   pl.BlockSpec((IDX_HEADS, IDX_DIM, tm), lambda i: (0, 0, i)),
                   pl.BlockSpec((IDX_HEADS, tm), lambda i: (0, i)),
                   pl.BlockSpec((tm, kw), lambda i: (i, 0)),
                   pl.BlockSpec((tm, kw), lambda i: (i, 0)),
                   pl.BlockSpec((tm, IDX_DIM), lambda i: (i, 0)),
                   pl.BlockSpec((None, KV_HEADS, tm, AUG_K), lambda i: (i // tiles, 0, i % tiles, 0)),
                   pl.BlockSpec((None, KV_HEADS, V_ROWS, tm), lambda i: (i // tiles, 0, 0, i % tiles)),
                   pl.BlockSpec((None, tm, IDX_DIM), lambda i: (i // tiles, i % tiles, 0))],
        out_shape=[jax.ShapeDtypeStruct((ATTN_HEADS, HEAD_DIM, t), BF16),
                   jax.ShapeDtypeStruct((IDX_HEADS, IDX_DIM, t), BF16),
                   jax.ShapeDtypeStruct((IDX_HEADS, t), F32),
                   jax.ShapeDtypeStruct((t, kw), F32),
                   jax.ShapeDtypeStruct((t, kw), F32),
                   jax.ShapeDtypeStruct((t, IDX_DIM), F32),
                   jax.ShapeDtypeStruct((nbatch, KV_HEADS, s, AUG_K), BF16),
                   jax.ShapeDtypeStruct((nbatch, KV_HEADS, V_ROWS, s), BF16),
                   jax.ShapeDtypeStruct((nbatch, s, IDX_DIM), BF16)],
        compiler_params=_cparams(("arbitrary",)),
        name="prep_t",
    )(proj, proj, proj, proj, proj, proj, q_norm_w.reshape(1, HEAD_DIM), k_norm_w.reshape(1, HEAD_DIM))


TKS = 256
NEAR_W = 2 * TKS
FAR_BUCKET = N_BUCKETS // 2 - 1
J_ALL = 2 ** 30


def _dsa_kernel(tab_ref, qh_ref, iqh_ref, iw_ref, ikT_ref, kT_ref, v_ref, bk_ref, ag_ref,
                o_ref, keys_ref, nb_ref, acc_ref, m_ref, l_ref, j_ref, *,
                tq, ta, tk, start, n_keys, topk):
    b = pl.program_id(0)
    i = pl.program_id(1)
    kk = pl.program_id(2)
    nk = pl.num_programs(2)
    q0 = start + i * tq
    vis_end = ((q0 + tq - 1) // CHUNK + 1) * CHUNK
    n_cols = jnp.minimum(vis_end, n_keys)
    n_a = (n_cols + ta - 1) // ta
    n_sub = (n_cols + TKS - 1) // TKS
    extra = jnp.maximum(n_keys - n_a * ta, 0).astype(F32)
    sub_per_tile = tk // TKS

    def vis_mask(t, width):
        kpos = t * width + lax.broadcasted_iota(I32, (tq, width), 1)
        qpos = q0 + lax.broadcasted_iota(I32, (tq, width), 0)
        vis = (kpos // CHUNK) <= (qpos // CHUNK)
        return vis, kpos

    @pl.when((b == 0) & (i == 0) & (kk == 0))
    def _():
        bk = bk_ref[...]

        def per_head(h, _):
            def per_bucket(n, val):
                return jnp.where(bk == n, tab_ref[n, h], val)
            val = lax.fori_loop(0, N_BUCKETS, per_bucket, jnp.zeros((tq, NEAR_W), F32))
            nb_ref[h] = val - tab_ref[FAR_BUCKET, h]
            return 0
        lax.fori_loop(0, ATTN_HEADS, per_head, 0)

    @pl.when(kk == 0)
    def _():
        iw = iw_ref[...] * IDX_HEADS ** -0.5

        def score_tile(t, _):
            c0 = pl.multiple_of(t * ta, ta)
            ikt = ikT_ref[:, pl.ds(c0, ta)]
            acc = jnp.zeros((tq, ta), F32)
            for h in range(IDX_HEADS):
                s = jnp.dot(iqh_ref[h], ikt, preferred_element_type=F32)
                acc = acc + iw[:, h:h + 1] * jnp.maximum(s, 0.0)
            vis, kpos = vis_mask(t, ta)
            acc = jnp.where(vis, acc, NEG)
            bits = pltpu.bitcast(acc, I32)
            key = bits ^ ((bits >> 31) & 0x7FFFFFFF)
            key = jnp.where(kpos < n_keys, key, INT_MIN)
            keys_ref[:, pl.ds(c0, ta)] = key
            return 0
        lax.fori_loop(0, n_a, score_tile, 0)

        def count(pred):
            def body(t, cnt):
                c0 = pl.multiple_of(t * ta, ta)
                blk = keys_ref[:, pl.ds(c0, ta)]
                c = jnp.where(pred(blk, t), 1.0, 0.0)
                for jj in range(ta // LANES):
                    cnt = cnt + c[:, jj * LANES:(jj + 1) * LANES]
                return cnt
            cnt = lax.fori_loop(0, n_a, body, jnp.zeros((tq, LANES), F32))
            return jnp.sum(cnt, axis=1, keepdims=True)

        def bit_step(it, prefix_u):
            bit = jnp.left_shift(jnp.int32(1), 31 - it)
            cand_u = prefix_u | bit
            cand_s = cand_u ^ INT_MIN
            cnt = count(lambda blk, t: blk >= cand_s) + jnp.where(NEG_KEY >= cand_s, extra, 0.0)
            return jnp.where(cnt >= topk, cand_u, prefix_u)
        prefix_u = lax.fori_loop(0, 32, bit_step, jnp.zeros((tq, 1), I32))
        thr = prefix_u ^ INT_MIN

        cnt_gt = count(lambda blk, t: blk > thr) + jnp.where(NEG_KEY > thr, extra, 0.0)
        cnt_eq = count(lambda blk, t: blk == thr)
        need = topk - cnt_gt
        j_ref[...] = jnp.full((tq, LANES), J_ALL, I32)

        @pl.when(jnp.max(cnt_eq - need) > 0.0)
        def _():
            def idx_step(it, jmax):
                cand = jmax | jnp.left_shift(jnp.int32(1), 14 - it)

                def pred(blk, t):
                    _, kpos = vis_mask(t, ta)
                    return (blk == thr) & (kpos < cand)
                g = count(pred)
                return jnp.where(g < need, cand, jmax)
            jmax = lax.fori_loop(0, 15, idx_step, jnp.zeros((tq, 1), I32))
            j_ref[...] = jnp.broadcast_to(jmax, (tq, LANES))

        jmax = j_ref[:, 0:1]

        def mask_tile(t, _):
            c0 = pl.multiple_of(t * ta, ta)
            blk = keys_ref[:, pl.ds(c0, ta)]
            vis, kpos = vis_mask(t, ta)
            sel = (blk > thr) | ((blk == thr) & (kpos <= jmax))
            ok = sel & vis & (kpos < n_keys)
            madd = jnp.where(ok, 0.0, NEG).astype(F32)
            keys_ref[:, pl.ds(c0, ta)] = pltpu.bitcast(madd, I32)
            return 0
        lax.fori_loop(0, n_a, mask_tile, 0)

        m_ref[...] = jnp.full(m_ref.shape, -jnp.inf, F32)
        l_ref[...] = jnp.zeros(l_ref.shape, F32)
        acc_ref[...] = jnp.zeros(acc_ref.shape, F32)

    def attend(jl, boff):
        lo = pl.multiple_of(jl * TKS, TKS)
        c0 = pl.multiple_of(kk * tk + lo, TKS)
        madd = pltpu.bitcast(keys_ref[:, pl.ds(c0, TKS)], F32)

        logits = [madd + jnp.dot(qh_ref[h], kT_ref[h // Q_PER_KV, :, pl.ds(lo, TKS)],
                                 preferred_element_type=F32) for h in range(ATTN_HEADS)]
        for h in range(ATTN_HEADS):
            s = logits[h]
            if boff is not None:
                s = s + nb_ref[h, :, boff:boff + TKS]
            m_prev = m_ref[h]
            m_new = jnp.maximum(m_prev, jnp.max(s, axis=1, keepdims=True))
            alpha = jnp.exp(m_prev - m_new)
            p = jnp.exp(s - m_new[:, 0:1])
            l_ref[h] = alpha * l_ref[h] + jnp.sum(p, axis=1, keepdims=True)
            acc_ref[h] = alpha[:, 0:HEAD_DIM] * acc_ref[h] + jnp.dot(
                p.astype(BF16), v_ref[h // Q_PER_KV, pl.ds(lo, TKS), :], preferred_element_type=F32)
            m_ref[h] = m_new

    n_here = jnp.clip(n_sub - kk * sub_per_tile, 0, sub_per_tile)

    def sub_body(jl, _):
        u = kk * sub_per_tile + jl

        @pl.when(u < n_sub - 2)
        def _():
            attend(jl, None)

        @pl.when(u == n_sub - 2)
        def _():
            attend(jl, 0)

        @pl.when(u == n_sub - 1)
        def _():
            attend(jl, TKS)
        return 0
    lax.fori_loop(0, n_here, sub_body, 0)

    @pl.when(kk == nk - 1)
    def _():
        ag = ag_ref[...]
        for h in range(ATTN_HEADS):
            o = acc_ref[h] / l_ref[h][:, 0:HEAD_DIM]
            sl = slice(h * HEAD_DIM, (h + 1) * HEAD_DIM)
            o_ref[:, sl] = (o * _silu(ag[:, sl])).astype(o_ref.dtype)


def dsa(qh, iqh, proj, ikT, kT, v4, bk, rel_bias, *, nbatch, tq, ta, tk, start, n_keys, topk):
    t = qh.shape[1]
    s_pad = ikT.shape[-1]
    nq = t // (nbatch * tq)
    nk = s_pad // tk

    def kt_idx(b, i, kk):
        q0 = start + i * tq
        n_cols = jnp.minimum(((q0 + tq - 1) // CHUNK + 1) * CHUNK, n_keys)
        return jnp.minimum(kk, (n_cols + tk - 1) // tk - 1)

    kern = functools.partial(_dsa_kernel, tq=tq, ta=ta, tk=tk, start=start, n_keys=n_keys, topk=topk)
    return pl.pallas_call(
        kern,
        grid=(nbatch, nq, nk),
        in_specs=[pl.BlockSpec(memory_space=pltpu.SMEM),
                  pl.BlockSpec((ATTN_HEADS, tq, HEAD_DIM), lambda b, i, kk: (0, b * nq + i, 0)),
                  pl.BlockSpec((IDX_HEADS, tq, IDX_DIM), lambda b, i, kk: (0, b * nq + i, 0)),
                  pl.BlockSpec((tq, LANES), lambda b, i, kk: (b * nq + i, COL["iw"] // LANES)),
                  pl.BlockSpec((None, IDX_DIM, s_pad), lambda b, i, kk: (b, 0, 0)),
                  pl.BlockSpec((None, KV_HEADS, HEAD_DIM, tk), lambda b, i, kk: (b, 0, 0, kt_idx(b, i, kk))),
                  pl.BlockSpec((None, KV_HEADS, tk, HEAD_DIM), lambda b, i, kk: (b, 0, kt_idx(b, i, kk), 0)),
                  pl.BlockSpec((tq, NEAR_W), lambda b, i, kk: (0, 0)),
                  pl.BlockSpec((tq, BRANCH_DIM), lambda b, i, kk: (b * nq + i, COL["ag"] // BRANCH_DIM))],
        out_specs=pl.BlockSpec((tq, BRANCH_DIM), lambda b, i, kk: (b * nq + i, 0)),
        out_shape=jax.ShapeDtypeStruct((t, BRANCH_DIM), BF16),
        scratch_shapes=[pltpu.VMEM((tq, s_pad), I32),
                        pltpu.VMEM((ATTN_HEADS, tq, NEAR_W), F32),
                        pltpu.VMEM((ATTN_HEADS, tq, HEAD_DIM), F32),
                        pltpu.VMEM((ATTN_HEADS, tq, LANES), F32),
                        pltpu.VMEM((ATTN_HEADS, tq, LANES), F32),
                        pltpu.VMEM((tq, LANES), I32)],
        compiler_params=_cparams(("arbitrary", "arbitrary", "arbitrary")),
        name="dsa",
    )(rel_bias, qh, iqh, proj, ikT, kT, v4, bk, proj)


AUG_K = 128
V_ROWS = 80
CNT_ROWS = 64
FAST_BOUND = 30.0
GM_ROWS = 256
QK_LOOKAHEAD = 8


def _dsat_kernel(tab_ref, kmax_ref, qT_ref, iqT_ref, iw_ref, ik_ref, k_ref, vT_ref, bk_ref, ag_ref,
                 o_ref, keys_ref, nb_ref, acc_ref, m_ref, qa_ref, j_ref, gm_ref, bmax_ref, fast_ref, *,
                 tq, ta, tk, start, n_keys, topk):
    b = pl.program_id(0)
    i = pl.program_id(1)
    kk = pl.program_id(2)
    nk = pl.num_programs(2)
    q0 = start + i * tq
    vis_end = ((q0 + tq - 1) // CHUNK + 1) * CHUNK
    n_cols = jnp.minimum(vis_end, n_keys)
    n_a = (n_cols + ta - 1) // ta
    n_sub = (n_cols + TKS - 1) // TKS
    extra = jnp.maximum(n_keys - n_a * ta, 0).astype(F32)
    sub_per_tile = tk // TKS

    def key_pos(t):
        return t * ta + lax.broadcasted_iota(I32, (ta, tq), 0)

    def vis_of(kpos):
        qpos = q0 + lax.broadcasted_iota(I32, (ta, tq), 1)
        return (kpos >> 6) <= (qpos >> 6)

    def colsum(x):
        return jnp.sum(x.reshape(ta // CNT_ROWS, CNT_ROWS, tq), axis=0)

    @pl.when((b == 0) & (i == 0) & (kk == 0))
    def _():
        bk = bk_ref[...]

        def per_head(h, bmax):
            def per_bucket(n, val):
                return jnp.where(bk == n, tab_ref[n, h], val)
            val = lax.fori_loop(0, N_BUCKETS, per_bucket, jnp.zeros((NEAR_W, tq), F32))
            val = val - tab_ref[FAR_BUCKET, h]
            nb_ref[h] = val
            return jnp.maximum(bmax, jnp.max(val))
        bmax_ref[0] = lax.fori_loop(0, ATTN_HEADS, per_head, jnp.float32(0.0))

    @pl.when(kk == 0)
    def _():
        rows = lax.broadcasted_iota(I32, (AUG_K, tq), 0)
        worst = jnp.zeros((1, tq), F32)
        for h in range(ATTN_HEADS):
            q = qT_ref[h].astype(F32)
            nrm = jnp.sqrt(jnp.sum(q * q, axis=0, keepdims=True))
            bound = nrm * kmax_ref[b, h // Q_PER_KV] + bmax_ref[0]
            worst = jnp.maximum(worst, bound)
            qpad = jnp.concatenate([q, jnp.zeros((AUG_K - HEAD_DIM, tq), F32)], axis=0)
            qa_ref[h] = jnp.where(rows == HEAD_DIM, -bound, qpad).astype(BF16)
        fast_ref[0] = (jnp.max(worst) <= FAST_BOUND).astype(I32)

        iw = iw_ref[...] * IDX_HEADS ** -0.5

        def score_tile(t, edge):
            c0 = pl.multiple_of(t * ta, ta)
            ikt = ik_ref[pl.ds(c0, ta), :]
            acc = jnp.zeros((ta, tq), F32)
            for h in range(IDX_HEADS):
                s = jnp.dot(ikt, iqT_ref[h], preferred_element_type=F32)
                acc = acc + iw[h:h + 1, :] * jnp.maximum(s, 0.0)
            if edge:
                kpos = key_pos(t)
                acc = jnp.where(vis_of(kpos), acc, NEG)
            bits = pltpu.bitcast(acc, I32)
            key = bits ^ ((bits >> 31) & 0x7FFFFFFF)
            if edge:
                key = jnp.where(kpos < n_keys, key, INT_MIN)
                acc = jnp.where(kpos < n_keys, acc, -jnp.inf)
            keys_ref[pl.ds(c0, ta), :] = key
            gm_ref[...] = jnp.maximum(gm_ref[...], jnp.max(acc.reshape(ta // GM_ROWS, GM_ROWS, tq), axis=0))
        gm_ref[...] = jnp.full((GM_ROWS, tq), -jnp.inf, F32)
        lax.fori_loop(0, n_a - 1, lambda t, _: (score_tile(t, False), 0)[1], 0)
        score_tile(n_a - 1, True)

        def count(pred):
            def body(t, cnt):
                c0 = pl.multiple_of(t * ta, ta)
                blk = keys_ref[pl.ds(c0, ta), :]
                return cnt + colsum(jnp.where(pred(blk, t), 1.0, 0.0))
            cnt = lax.fori_loop(0, n_a, body, jnp.zeros((CNT_ROWS, tq), F32))
            return jnp.sum(cnt, axis=0, keepdims=True)

        def count_ge(p):
            return count(lambda blk, t: blk >= p) + jnp.where(NEG_KEY >= p, extra, 0.0)

        def key_of(f):
            bits = pltpu.bitcast(f, I32)
            return bits ^ ((bits >> 31) & 0x7FFFFFFF)

        gm = gm_ref[...]
        lo0 = key_of(jnp.min(gm, axis=0, keepdims=True))
        hi0 = key_of(jnp.max(gm, axis=0, keepdims=True)) + 1
        c_lo0 = jnp.full((1, tq), float(2 ** 24), F32)

        def active_of(lo, hi, c_lo):
            return (c_lo != topk) & (lo + 1 < hi)

        def search_cond(st):
            return st[5] > 0.0

        def search_body(st):
            lo, hi, c_lo, c_hi, it, _ = st
            mid = (lo >> 1) + (hi >> 1) + (lo & hi & 1)
            zero_pivot = (it < 2) & (lo < it) & (it < hi)
            p = jnp.where(zero_pivot, it, mid)
            p = jnp.minimum(jnp.maximum(p, lo + 1), hi - 1)
            c = count_ge(p)
            act = active_of(lo, hi, c_lo)
            up = act & (c >= topk)
            dn = act & (c < topk)
            lo, c_lo = jnp.where(up, p, lo), jnp.where(up, c, c_lo)
            hi, c_hi = jnp.where(dn, p, hi), jnp.where(dn, c, c_hi)
            left = jnp.max(jnp.where(active_of(lo, hi, c_lo), 1.0, 0.0))
            return lo, hi, c_lo, c_hi, it + 1, left

        st0 = (lo0, hi0, c_lo0, jnp.zeros((1, tq), F32), jnp.int32(0),
               jnp.max(jnp.where(active_of(lo0, hi0, c_lo0), 1.0, 0.0)))
        thr, _, c_thr, _, _, _ = lax.while_loop(search_cond, search_body, st0)
        j_ref[...] = jnp.full((8, tq), J_ALL, I32)

        @pl.when(jnp.max(c_thr) > topk)
        def _():
            cnt_gt = count(lambda blk, t: blk > thr) + jnp.where(NEG_KEY > thr, extra, 0.0)
            need = topk - cnt_gt

            def idx_step(it, jmax):
                cand = jmax | jnp.left_shift(jnp.int32(1), 14 - it)
                g = count(lambda blk, t: (blk == thr) & (key_pos(t) < cand))
                return jnp.where(g < need, cand, jmax)
            jmax = lax.fori_loop(0, 15, idx_step, jnp.zeros((1, tq), I32))
            j_ref[...] = jnp.broadcast_to(jmax, (8, tq))

        jmax = j_ref[0:1, :]

        def mask_tile(t, edge):
            c0 = pl.multiple_of(t * ta, ta)
            blk = keys_ref[pl.ds(c0, ta), :]
            kpos = key_pos(t)
            ok = (blk > thr) | ((blk == thr) & (kpos <= jmax))
            if edge:
                ok = ok & vis_of(kpos) & (kpos < n_keys)
            keys_ref[pl.ds(c0, ta), :] = pltpu.bitcast(jnp.where(ok, 1.0, 0.0).astype(F32), I32)
        lax.fori_loop(0, n_a - 1, lambda t, _: (mask_tile(t, False), 0)[1], 0)
        mask_tile(n_a - 1, True)

        m_ref[...] = jnp.full(m_ref.shape, -jnp.inf, F32)
        acc_ref[...] = jnp.zeros(acc_ref.shape, F32)

    def attend_fast(jl, boff):
        lo = pl.multiple_of(jl * TKS, TKS)
        c0 = pl.multiple_of(kk * tk + lo, TKS)
        mask = pltpu.bitcast(keys_ref[pl.ds(c0, TKS), :], F32)

        def logits(h):
            kt = k_ref[h // Q_PER_KV, pl.ds(lo, TKS), :]
            return jnp.dot(kt, qa_ref[h], preferred_element_type=F32)

        pending = [logits(h) for h in range(QK_LOOKAHEAD)]
        for h in range(ATTN_HEADS):
            if h + QK_LOOKAHEAD < ATTN_HEADS:
                pending.append(logits(h + QK_LOOKAHEAD))
            s = pending.pop(0)
            if boff is not None:
                s = nb_ref[h, boff:boff + TKS, :] + s
            p = (jnp.exp(s) * mask).astype(BF16)
            vt = vT_ref[h // Q_PER_KV, :, pl.ds(lo, TKS)]
            acc_ref[h] = acc_ref[h] + jnp.dot(vt, p, preferred_element_type=F32)

    def attend_slow(jl, boff):
        lo = pl.multiple_of(jl * TKS, TKS)
        c0 = pl.multiple_of(kk * tk + lo, TKS)
        madd = (pltpu.bitcast(keys_ref[pl.ds(c0, TKS), :], F32) - 1.0) * (-NEG)

        def group(g, _):
            kt = k_ref[g, pl.ds(lo, TKS), :]
            vt = vT_ref[g, :, pl.ds(lo, TKS)]
            for r in range(Q_PER_KV):
                h = g * Q_PER_KV + r
                s = jnp.dot(kt, qa_ref[h], preferred_element_type=F32) + madd
                if boff is not None:
                    s = s + nb_ref[h, boff:boff + TKS, :]
                m_prev = m_ref[h]
                m_new = jnp.maximum(m_prev, jnp.max(s, axis=0, keepdims=True))
                alpha = jnp.exp(m_prev - m_new)
                p = jnp.exp(s - m_new[0:1, :])
                acc_ref[h] = alpha[0:1, :] * acc_ref[h] + jnp.dot(
                    vt, p.astype(BF16), preferred_element_type=F32)
                m_ref[h] = m_new
            return 0
        lax.fori_loop(0, KV_HEADS, group, 0)

    n_here = jnp.clip(n_sub - kk * sub_per_tile, 0, sub_per_tile)
    fast = fast_ref[0] == 1

    def sub_loop(attend):
        def sub_body(jl, _):
            u = kk * sub_per_tile + jl

            @pl.when(u < n_sub - 2)
            def _():
                attend(jl, None)

            @pl.when(u == n_sub - 2)
            def _():
                attend(jl, 0)

            @pl.when(u == n_sub - 1)
            def _():
                attend(jl, TKS)
            return 0
        lax.fori_loop(0, n_here, sub_body, 0)

    @pl.when(fast)
    def _():
        sub_loop(attend_fast)

    @pl.when(jnp.logical_not(fast))
    def _():
        sub_loop(attend_slow)

    @pl.when(kk == nk - 1)
    def _():
        outs = []
        for h in range(ATTN_HEADS):
            a = acc_ref[h]
            outs.append(a[0:HEAD_DIM, :] / a[HEAD_DIM:HEAD_DIM + 1, :])
        o = jnp.concatenate(outs, axis=0).T
        o_ref[...] = (o * _silu(ag_ref[...])).astype(o_ref.dtype)


def augment_kv(k4, vT, n_keys):
    s_pad = k4.shape[2]
    real = (jnp.arange(s_pad) < n_keys).astype(BF16)
    k_aug = jnp.concatenate([k4, jnp.broadcast_to(real[None, None, :, None], k4.shape[:3] + (1,)),
                             jnp.zeros(k4.shape[:3] + (AUG_K - HEAD_DIM - 1,), BF16)], axis=-1)
    v_aug = jnp.concatenate([vT, jnp.broadcast_to(real[None, None, None, :], vT.shape[:2] + (1, s_pad)),
                             jnp.zeros(vT.shape[:2] + (V_ROWS - HEAD_DIM - 1, s_pad), BF16)], axis=2)
    return k_aug, v_aug


def dsa_t(qT, iqT, iwT, proj, ik, k_aug, v_aug, bkT, rel_bias, *, nbatch, tq, ta, tk, start, n_keys, topk):
    t = qT.shape[2]
    s_pad = ik.shape[1]
    nq = t // (nbatch * tq)
    nk = s_pad // tk
    assert tq % LANES == 0 and ta % tq == 0 and tk % TKS == 0 and s_pad % ta == 0 and ta % CNT_ROWS == 0
    assert ta % GM_ROWS == 0 and topk <= GM_ROWS <= start + tq

    kf = k_aug[..., :HEAD_DIM].astype(F32)
    kmax = jnp.sqrt(jnp.max(jnp.sum(kf * kf, axis=-1), axis=-1))

    def kt_idx(b, i, kk):
        q0 = start + i * tq
        n_cols = jnp.minimum(((q0 + tq - 1) // CHUNK + 1) * CHUNK, n_keys)
        return jnp.minimum(kk, (n_cols + tk - 1) // tk - 1)

    kern = functools.partial(_dsat_kernel, tq=tq, ta=ta, tk=tk, start=start, n_keys=n_keys, topk=topk)
    smem = pl.BlockSpec(memory_space=pltpu.SMEM)
    return pl.pallas_call(
        kern,
        grid=(nbatch, nq, nk),
        in_specs=[smem, smem,
                  pl.BlockSpec((ATTN_HEADS, HEAD_DIM, tq), lambda b, i, kk: (0, 0, b * nq + i)),
                  pl.BlockSpec((IDX_HEADS, IDX_DIM, tq), lambda b, i, kk: (0, 0, b * nq + i)),
                  pl.BlockSpec((IDX_HEADS, tq), lambda b, i, kk: (0, b * nq + i)),
                  pl.BlockSpec((None, s_pad, IDX_DIM), lambda b, i, kk: (b, 0, 0)),
                  pl.BlockSpec((None, KV_HEADS, tk, AUG_K), lambda b, i, kk: (b, 0, kt_idx(b, i, kk), 0)),
                  pl.BlockSpec((None, KV_HEADS, V_ROWS, tk), lambda b, i, kk: (b, 0, 0, kt_idx(b, i, kk))),
                  pl.BlockSpec((NEAR_W, tq), lambda b, i, kk: (0, 0)),
                  pl.BlockSpec((tq, BRANCH_DIM), lambda b, i, kk: (b * nq + i, COL["ag"] // BRANCH_DIM))],
        out_specs=pl.BlockSpec((tq, BRANCH_DIM), lambda b, i, kk: (b * nq + i, 0)),
        out_shape=jax.ShapeDtypeStruct((t, BRANCH_DIM), BF16),
        scratch_shapes=[pltpu.VMEM((s_pad, tq), I32),
                        pltpu.VMEM((ATTN_HEADS, NEAR_W, tq), F32),
                        pltpu.VMEM((ATTN_HEADS, V_ROWS, tq), F32),
                        pltpu.VMEM((ATTN_HEADS, 8, tq), F32),
                        pltpu.VMEM((ATTN_HEADS, AUG_K, tq), BF16),
                        pltpu.VMEM((8, tq), I32),
                        pltpu.VMEM((GM_ROWS, tq), F32),
                        pltpu.SMEM((1,), F32),
                        pltpu.SMEM((1,), I32)],
        compiler_params=_cparams(("arbitrary", "arbitrary", "arbitrary")),
        name="dsa_t",
    )(rel_bias, kmax, qT, iqT, iwT, ik, k_aug, v_aug, bkT, proj)


def t5_bucket(rel):
    half = N_BUCKETS // 2
    max_exact = half // 2
    ret = jnp.where(rel > 0, half, 0)
    n = jnp.abs(rel)
    nf = jnp.maximum(n, max_exact).astype(jnp.float32)
    large = max_exact + (jnp.log(nf / max_exact) / math.log(MAX_DISTANCE / max_exact)
                         * (half - max_exact)).astype(jnp.int32)
    large = jnp.minimum(large, half - 1)
    return ret + jnp.where(n < max_exact, n, large)


def near_buckets(tq, q0, n_keys):
    vis_end = ((q0 + tq - 1) // CHUNK + 1) * CHUNK
    end = -(-min(vis_end, n_keys) // TKS) * TKS
    assert end - NEAR_W <= q0 - MAX_DISTANCE + 1, "near window must cover every non-saturated offset"
    kpos = end - NEAR_W + jnp.arange(NEAR_W, dtype=jnp.int32)[None, :]
    qpos = q0 + jnp.arange(tq, dtype=jnp.int32)[:, None]
    return t5_bucket(kpos - qpos)


def _ssd_kernel(xs_ref, bm_ref, cm_ref, dt_ref, z_ref, hist_ref, st0_ref, cw_ref, cb_ref,
                dtb_ref, alog_ref, dsk_ref, nw_ref, exp_ref, y_ref, st_ref,
                ext_ref, state_ref, *, q, valid):
    c = pl.program_id(1)
    nc = pl.num_programs(1)
    hp = SSD_INNER
    gw = hp // SSD_GROUPS
    hi = lax.Precision.HIGHEST

    @pl.when(c == 0)
    def _():
        ext_ref[0:8, :] = hist_ref[...]
        state_ref[...] = st0_ref[...]

    ext_ref[8:8 + q, 0:hp] = xs_ref[...]
    ext_ref[8:8 + q, hp:hp + 256] = bm_ref[...]
    ext_ref[8:8 + q, hp + 256:hp + 512] = cm_ref[...]
    conv = jnp.zeros((q, CONV_DIM), F32) + cb_ref[...]
    for j in range(CONV_WIDTH):
        conv = conv + ext_ref[5 + j:5 + j + q, :] * cw_ref[j:j + 1, :]
    ext_ref[0:8, :] = ext_ref[q:q + 8, :]
    conv = _silu(conv)
    xs = conv[:, 0:hp]
    bmat = conv[:, hp:hp + 256]
    cmat = conv[:, hp + 256:hp + 512]

    xdt = dt_ref[...] + dtb_ref[...]
    dt = jnp.maximum(xdt, 0.0) + jnp.log1p(jnp.exp(-jnp.abs(xdt)))
    if valid < q:
        rows = lax.broadcasted_iota(I32, (q, LANES), 0)
        dt = jnp.where(rows < valid, dt, 0.0)
    adt = dt * (-jnp.exp(alog_ref[...]))
    rr = lax.broadcasted_iota(I32, (q, q), 0)
    cc = lax.broadcasted_iota(I32, (q, q), 1)
    tri = rr >= cc
    acum = jnp.dot(tri.astype(F32), adt, precision=hi, preferred_element_type=F32)
    acum_t = acum.T
    dt_t = dt.T
    alast = acum[q - 1:q, :]

    expand = exp_ref[...]
    e_acum = jnp.dot(jnp.exp(acum), expand, precision=hi, preferred_element_type=F32)
    e_tail = jnp.dot(jnp.exp(alast - acum) * dt, expand, precision=hi, preferred_element_type=F32)
    e_last = e_acum[q - 1:q, :]

    xw = (xs * e_tail).astype(BF16)
    xb = xs.astype(BF16)
    y_parts = []
    new_state = []
    for g in range(SSD_GROUPS):
        bg = bmat[:, g * SSD_STATE:(g + 1) * SSD_STATE]
        cg = cmat[:, g * SSD_STATE:(g + 1) * SSD_STATE].astype(BF16)
        bg_t = bg.T.astype(BF16)
        cb = jnp.dot(cg, bg_t, preferred_element_type=F32)
        st_g = state_ref[:, g * gw:(g + 1) * gw]
        y_off = jnp.dot(cg, st_g.astype(BF16), preferred_element_type=F32)
        new_state.append(jnp.dot(bg_t, xw[:, g * gw:(g + 1) * gw], preferred_element_type=F32))
        heads = []
        for r in range(SSD_HEADS // SSD_GROUPS):
            h = g * (SSD_HEADS // SSD_GROUPS) + r
            seg = acum[:, h:h + 1] - acum_t[h:h + 1, :]
            decay = jnp.where(tri, jnp.exp(jnp.where(tri, seg, 0.0)), 0.0)
            wmat = (cb * decay * dt_t[h:h + 1, :]).astype(BF16)
            heads.append(jnp.dot(wmat, xb[:, h * SSD_HEAD_DIM:(h + 1) * SSD_HEAD_DIM],
                                 preferred_element_type=F32))
        y_parts.append(jnp.concatenate(heads, axis=1) + y_off * e_acum[:, g * gw:(g + 1) * gw])
    y = jnp.concatenate(y_parts, axis=1)
    state_ref[...] = state_ref[...] * e_last + jnp.concatenate(new_state, axis=1)

    y = (y + dsk_ref[...] * xs) * _silu(z_ref[...])
    ms = jnp.mean(y * y, axis=-1, keepdims=True)
    y_ref[...] = (y * lax.rsqrt(ms + EPS) * nw_ref[...]).astype(y_ref.dtype)

    @pl.when(c == nc - 1)
    def _():
        st_ref[...] = state_ref[...]


def ssd(proj, hist8, state_t, conv_w, conv_b, dt_bias, a_log, d_skip, ssd_norm_w, *, nbatch, q, valid):
    t = proj.shape[0]
    nc = t // (nbatch * q)
    hp = SSD_INNER

    def pad_heads(v, fill):
        return jnp.concatenate([v.astype(F32), jnp.full((LANES - SSD_HEADS,), fill, F32)]).reshape(1, LANES)

    expand = (jnp.arange(LANES)[:, None] == (jnp.arange(hp)[None, :] // SSD_HEAD_DIM)).astype(F32)
    dsk = jnp.repeat(d_skip.astype(F32), SSD_HEAD_DIM).reshape(1, hp)
    kern = functools.partial(_ssd_kernel, q=q, valid=valid)
    const2 = lambda shape: pl.BlockSpec(shape, lambda b, c: (0, 0))
    return pl.pallas_call(
        kern,
        grid=(nbatch, nc),
        in_specs=[pl.BlockSpec((q, hp), lambda b, c: (b * nc + c, COL["xs"] // hp)),
                  pl.BlockSpec((q, 256), lambda b, c: (b * nc + c, COL["bm"] // 256)),
                  pl.BlockSpec((q, 256), lambda b, c: (b * nc + c, COL["cm"] // 256)),
                  pl.BlockSpec((q, LANES), lambda b, c: (b * nc + c, COL["dt"] // LANES)),
                  pl.BlockSpec((q, hp), lambda b, c: (b * nc + c, COL["z"] // hp)),
                  pl.BlockSpec((None, 8, CONV_DIM), lambda b, c: (b, 0, 0)),
                  pl.BlockSpec((None, SSD_STATE, hp), lambda b, c: (b, 0, 0)),
                  const2((CONV_WIDTH, CONV_DIM)), const2((1, CONV_DIM)),
                  const2((1, LANES)), const2((1, LANES)), const2((1, hp)), const2((1, hp)),
                  const2((LANES, hp))],
        out_specs=[pl.BlockSpec((q, hp), lambda b, c: (b * nc + c, 0)),
                   pl.BlockSpec((None, SSD_STATE, hp), lambda b, c: (b, 0, 0))],
        out_shape=[jax.ShapeDtypeStruct((t, hp), BF16),
                   jax.ShapeDtypeStruct((nbatch, SSD_STATE, hp), F32)],
        scratch_shapes=[pltpu.VMEM((q + 8, CONV_DIM), F32),
                        pltpu.VMEM((SSD_STATE, hp), F32)],
        compiler_params=_cparams(("arbitrary", "arbitrary")),
        name="ssd",
    )(proj, proj, proj, proj, proj, hist8, state_t, conv_w, conv_b.reshape(1, CONV_DIM),
      pad_heads(dt_bias, 0.0), pad_heads(a_log, 0.0), dsk, ssd_norm_w.reshape(1, hp), expand)


def _pool_kernel(u_ref, pg_ref, hist_ref, w_ref, b_ref, sc_ref, y_ref, ext_ref, *, r, start):
    c = pl.program_id(1)

    @pl.when(c == 0)
    def _():
        ext_ref[0:16, :] = hist_ref[...]

    ext_ref[16:16 + r, :] = u_ref[...]
    pos = start + c * r + lax.broadcasted_iota(I32, (r, 1), 0)
    outs = []
    for gi, w in enumerate(POOL_WINDOWS):
        lo = gi * POOL_GROUP_DIM
        cur = ext_ref[16:16 + r, lo:lo + POOL_GROUP_DIM]
        win = cur
        for s in range(1, w):
            win = win + ext_ref[16 - s:16 - s + r, lo:lo + POOL_GROUP_DIM]
        cnt = jnp.minimum(w, pos + 1).astype(F32)
        pooled = win / cnt - cur
        mixed = jnp.dot(pooled.astype(BF16), w_ref[gi], preferred_element_type=F32)
        outs.append(mixed + b_ref[gi:gi + 1, :])
    ext_ref[0:16, :] = ext_ref[r:r + 16, :]
    mixed = jnp.concatenate(outs, axis=1) * sc_ref[...]
    y_ref[...] = (mixed * _silu(pg_ref[...])).astype(y_ref.dtype)


def pool(proj, hist16, pool_w, pool_b, pool_scale, *, nbatch, r, start):
    t = proj.shape[0]
    nc = t // (nbatch * r)
    d = BRANCH_DIM
    kern = functools.partial(_pool_kernel, r=r, start=start)
    return pl.pallas_call(
        kern,
        grid=(nbatch, nc),
        in_specs=[pl.BlockSpec((r, d), lambda b, c: (b * nc + c, COL["u"] // d)),
                  pl.BlockSpec((r, d), lambda b, c: (b * nc + c, COL["pg"] // d)),
                  pl.BlockSpec((None, 16, d), lambda b, c: (b, 0, 0)),
                  pl.BlockSpec((POOL_GROUPS, POOL_GROUP_DIM, POOL_GROUP_DIM), lambda b, c: (0, 0, 0)),
                  pl.BlockSpec((POOL_GROUPS, POOL_GROUP_DIM), lambda b, c: (0, 0)),
                  pl.BlockSpec((1, d), lambda b, c: (0, 0))],
        out_specs=pl.BlockSpec((r, d), lambda b, c: (b * nc + c, 0)),
        out_shape=jax.ShapeDtypeStruct((t, d), BF16),
        scratch_shapes=[pltpu.VMEM((r + 16, d), F32)],
        compiler_params=_cparams(("arbitrary", "arbitrary")),
        name="pool",
    )(proj, proj, hist16, pool_w.astype(BF16), pool_b, pool_scale.reshape(1, d))


def _merge_kernel(y0_ref, y1_ref, y2_ref, g0_ref, g1_ref, g2_ref, w_ref, o_ref):
    acc = jax.nn.sigmoid(g0_ref[...]) * jnp.dot(y0_ref[...], w_ref[0], preferred_element_type=F32)
    acc = acc + jax.nn.sigmoid(g1_ref[...]) * jnp.dot(y1_ref[...], w_ref[1], preferred_element_type=F32)
    acc = acc + jax.nn.sigmoid(g2_ref[...]) * jnp.dot(y2_ref[...], w_ref[2], preferred_element_type=F32)
    o_ref[...] = acc.astype(o_ref.dtype)


def merge(y_ssd, y_att, y_pool, proj, w_branch, tm):
    t = y_ssd.shape[0]
    tn = 512 if tm > 512 else 1024
    nj = D_MODEL // tn
    ysp = pl.BlockSpec((tm, BRANCH_DIM), lambda i, j: (i, 0))

    def gate_spec(bi):
        return pl.BlockSpec((tm, tn), lambda i, j: (i, (COL["mg"] + bi * D_MODEL) // tn + j))

    return pl.pallas_call(
        _merge_kernel,
        grid=(t // tm, nj),
        in_specs=[ysp, ysp, ysp, gate_spec(0), gate_spec(1), gate_spec(2),
                  pl.BlockSpec((N_BRANCH, BRANCH_DIM, tn), lambda i, j: (0, 0, j))],
        out_specs=pl.BlockSpec((tm, tn), lambda i, j: (i, j)),
        out_shape=jax.ShapeDtypeStruct((t, D_MODEL), BF16),
        compiler_params=_cparams(("arbitrary", "arbitrary")),
        name="merge",
    )(y_ssd, y_att, y_pool, proj, proj, proj, w_branch)


def _outproj_kernel(m_ref, w_ref, x_ref, g_ref, o_ref):
    o_ref[...] = x_ref[...] + g_ref[...] * jnp.dot(m_ref[...], w_ref[...], preferred_element_type=F32)


def outproj(merged, w_out, x, gate, tm, rows_per_mod):
    t = x.shape[0]
    tn = 1024
    return pl.pallas_call(
        _outproj_kernel,
        grid=(t // tm, D_MODEL // tn),
        in_specs=[pl.BlockSpec((tm, D_MODEL), lambda i, j: (i, 0)),
                  pl.BlockSpec((D_MODEL, tn), lambda i, j: (0, j)),
                  pl.BlockSpec((tm, tn), lambda i, j: (i, j)),
                  _mod_spec(gate, tm, tn, rows_per_mod, lambda j: j)],
        out_specs=pl.BlockSpec((tm, tn), lambda i, j: (i, j)),
        out_shape=jax.ShapeDtypeStruct((t, D_MODEL), F32),
        compiler_params=_cparams(("arbitrary", "arbitrary")),
        name="outproj",
    )(merged, w_out, x, gate)


def _pad_to(a, axis, size):
    pad = [(0, 0)] * a.ndim
    pad[axis] = (0, size - a.shape[axis])
    return jnp.pad(a, pad)


def trunk_layer(x, mod, k_past, v_past, ik_past, h0, conv_hist, pool_hist, rel_bias, lw, *, per_row_mod):
    (norm_w, w_in, conv_w, conv_b, dt_bias, a_log, d_skip, ssd_norm_w, q_norm_w, k_norm_w,
     pool_w, pool_b, pool_scale, w_branch, w_out) = lw
    bsz, seq, d = x.shape
    t = bsz * seq
    start = k_past.shape[1]
    n_keys = start + seq
    topk = min(TOPK_MAX, n_keys // 4)
    x2 = x.reshape(t, d)
    shift, scale, gate = mod[:, :d], mod[:, d:2 * d], mod[:, 2 * d:]
    tm = min(1024, t)
    tm_prep = min(512, t)
    if per_row_mod:
        expand = lambda m: jnp.broadcast_to(m[:, None, :], (bsz, seq, d)).reshape(t // tm, tm, d)
    else:
        expand = lambda m: m[:, None, :]
    scale3, shift3, gate3 = expand(scale), expand(shift), expand(gate)

    proj = inproj(x2, scale3, shift3, norm_w, w_in, tm, seq)

    kw = KV_HEADS * HEAD_DIM
    tq = min(256, seq)
    if seq >= 2048:
        tk, ta = 2048, 512
    else:
        tk = ta = None
    s_pad = -(-n_keys // TKS) * TKS
    if tk is None:
        tk, ta = s_pad, s_pad // 3 if (s_pad // 3) % TKS == 0 else s_pad
    s_pad = -(-s_pad // tk) * tk
    bk = near_buckets(tq, start, n_keys)
    dsa_args = dict(nbatch=bsz, tq=tq, ta=ta, tk=tk, start=start, n_keys=n_keys, topk=topk)
    lanes_layout = tq % LANES == 0
    if lanes_layout and start == 0 and s_pad == seq:
        qT, iqT, iwT, k_new, v_new, ik_new, k_aug, v_aug, ik_b = prep_t(proj, q_norm_w, k_norm_w, tm_prep, bsz)
        y_att = dsa_t(qT, iqT, iwT, proj, ik_b, k_aug, v_aug, bk.T, rel_bias, **dsa_args)
    else:
        qh, k_new, iqh = prep(proj, q_norm_w, k_norm_w, tm_prep)
        v_new = proj[:, COL["v"]:COL["v"] + kw]
        ik_new = proj[:, COL["ik"]:COL["ik"] + IDX_DIM]
    k_new4 = k_new.reshape(bsz, seq, KV_HEADS, HEAD_DIM)
    v_new4 = v_new.reshape(bsz, seq, KV_HEADS, HEAD_DIM)
    ik_new3 = ik_new.reshape(bsz, seq, IDX_DIM)
    if not (lanes_layout and start == 0 and s_pad == seq):
        k_all = jnp.concatenate([k_past.astype(F32), k_new4], axis=1).astype(BF16)
        v_all = jnp.concatenate([v_past.astype(F32), v_new4], axis=1).astype(BF16)
        ik_all = jnp.concatenate([ik_past.astype(F32), ik_new3], axis=1).astype(BF16)
        if lanes_layout:
            k4 = _pad_to(jnp.transpose(k_all, (0, 2, 1, 3)), 2, s_pad)
            vT = _pad_to(jnp.transpose(v_all, (0, 2, 3, 1)), 3, s_pad)
            k_aug, v_aug = augment_kv(k4, vT, n_keys)
            ik_p = _pad_to(ik_all, 1, s_pad)
            iwT = proj[:, COL["iw"]:COL["iw"] + IDX_HEADS].T
            y_att = dsa_t(jnp.swapaxes(qh, 1, 2), jnp.swapaxes(iqh, 1, 2), iwT, proj, ik_p, k_aug, v_aug,
                          bk.T, rel_bias, **dsa_args)
        else:
            kT = _pad_to(jnp.transpose(k_all, (0, 2, 3, 1)), 3, s_pad)
            v4 = _pad_to(jnp.transpose(v_all, (0, 2, 1, 3)), 2, s_pad)
            ikT = _pad_to(jnp.transpose(ik_all, (0, 2, 1)), 2, s_pad)
            y_att = dsa(qh, iqh, proj, ikT, kT, v4, bk, rel_bias, **dsa_args)

    q = 128
    hist8 = jnp.pad(conv_hist.astype(F32), ((0, 0), (8 - (CONV_WIDTH - 1), 0), (0, 0)))
    state_t = jnp.transpose(h0.astype(F32), (0, 3, 1, 2)).reshape(bsz, SSD_STATE, SSD_INNER)
    if seq < q:
        proj_ssd = _pad_to(proj.reshape(bsz, seq, PROJ_DIM), 1, q).reshape(bsz * q, PROJ_DIM)
        valid = seq
    else:
        proj_ssd, valid = proj, q
    y_ssd, st = ssd(proj_ssd, hist8, state_t, conv_w, conv_b, dt_bias, a_log, d_skip, ssd_norm_w,
                    nbatch=bsz, q=q, valid=valid)
    if seq < q:
        y_ssd = y_ssd.reshape(bsz, q, SSD_INNER)[:, :seq].reshape(t, SSD_INNER)
    h_last = jnp.transpose(st.reshape(bsz, SSD_STATE, SSD_HEADS, SSD_HEAD_DIM), (0, 2, 3, 1))

    hist16 = jnp.pad(pool_hist.astype(F32), ((0, 0), (16 - POOL_STATE, 0), (0, 0)))
    y_pool = pool(proj, hist16, pool_w, pool_b, pool_scale, nbatch=bsz, r=min(512, seq), start=start)

    merged = merge(y_ssd, y_att, y_pool, proj, w_branch, tm)
    x_new = outproj(merged, w_out, x2, gate3, tm, seq).reshape(bsz, seq, d)

    xbc = proj[:, COL["xbc"]:COL["xbc"] + CONV_DIM].reshape(bsz, seq, CONV_DIM)
    u = proj[:, COL["u"]:COL["u"] + BRANCH_DIM].reshape(bsz, seq, BRANCH_DIM)
    conv_new = jnp.concatenate([conv_hist.astype(F32), xbc], axis=1)[:, -(CONV_WIDTH - 1):]
    pool_new = jnp.concatenate([pool_hist.astype(F32), u[:, -min(seq, POOL_STATE):]], axis=1)[:, -POOL_STATE:]
    return x_new, k_new4, v_new4, ik_new3, h_last, conv_new, pool_new


def _reorder_w_in(w_in):
    parts = []
    for name in _NEW_ORDER:
        off, size = _ORIG[name]
        seg = w_in[..., off:off + size]
        padw = -(-size // LANES) * LANES - size
        if padw:
            seg = jnp.pad(seg, ((0, 0), (0, 0), (0, padw)))
        parts.append(seg)
    out = jnp.concatenate(parts, axis=-1)
    return _pad_to(out, 2, PROJ_DIM).astype(BF16)


def kernel(x_prompt, x_sample, cache_k, cache_v, cache_idx_k, state_ssm, state_conv, state_pool,
           c_prompt, c_sample, rel_bias, w_ada, b_ada, norm_w, w_in, conv_w, conv_b, dt_bias,
           a_log, d_skip, ssd_norm_w, q_norm_w, k_norm_w, pool_w, pool_b, pool_scale,
           w_branch, w_out):
    bp = x_prompt.shape[0]
    f32 = F32
    mods = ada_mod(jnp.concatenate([c_prompt, c_sample], axis=0), w_ada, b_ada)
    w_in_r = _reorder_w_in(w_in)
    w_branch_b = w_branch.astype(BF16)
    w_out_b = w_out.astype(BF16)

    empty_kv = jnp.zeros((bp, 0, KV_HEADS, HEAD_DIM), f32)
    empty_ik = jnp.zeros((bp, 0, IDX_DIM), f32)
    zero_ssm = jnp.zeros((bp, SSD_HEADS, SSD_HEAD_DIM, SSD_STATE), f32)
    zero_conv = jnp.zeros((bp, CONV_WIDTH - 1, CONV_DIM), f32)
    zero_pool = jnp.zeros((bp, POOL_STATE, BRANCH_DIM), f32)

    xp, xs = x_prompt, x_sample
    outs_p = [[] for _ in range(6)]
    outs_s = [[] for _ in range(6)]
    for l in range(DEPTH):
        lw = (norm_w[l], w_in_r[l], conv_w[l], conv_b[l], dt_bias[l], a_log[l], d_skip[l], ssd_norm_w[l],
              q_norm_w[l], k_norm_w[l], pool_w[l], pool_b[l], pool_scale[l], w_branch_b[l], w_out_b[l])
        rp = trunk_layer(xp, mods[l, :bp], empty_kv, empty_kv, empty_ik, zero_ssm, zero_conv, zero_pool,
                         rel_bias, lw, per_row_mod=False)
        rs = trunk_layer(xs, mods[l, bp:], cache_k[l], cache_v[l], cache_idx_k[l], state_ssm[l],
                         state_conv[l], state_pool[l], rel_bias, lw, per_row_mod=True)
        xp, xs = rp[0], rs[0]
        for n in range(6):
            outs_p[n].append(rp[n + 1])
            outs_s[n].append(rs[n + 1])
    return (xp, xs, *[jnp.stack(o) for o in outs_p], *[jnp.stack(o) for o in outs_s])
```

```python
import functools
import math

import numpy as np
import jax
import jax.numpy as jnp
from jax import lax
from jax.experimental import pallas as pl
from jax.experimental.pallas import tpu as pltpu

F32 = jnp.float32
BF16 = jnp.bfloat16
I32 = jnp.int32

D_MODEL = 2048
DEPTH = 4
CHUNK = 64
N_BRANCH = 3
BRANCH_DIM = 1024
SSD_INNER = BRANCH_DIM
SSD_HEAD_DIM = 64
SSD_HEADS = SSD_INNER // SSD_HEAD_DIM
SSD_GROUPS = 2
SSD_STATE = 128
CONV_WIDTH = 4
CONV_DIM = SSD_INNER + 2 * SSD_GROUPS * SSD_STATE
ATTN_HEADS = 16
KV_HEADS = 4
HEAD_DIM = BRANCH_DIM // ATTN_HEADS
Q_PER_KV = ATTN_HEADS // KV_HEADS
IDX_HEADS = 8
IDX_DIM = 64
TOPK_MAX = 256
N_BUCKETS = 32
MAX_DISTANCE = 128
POOL_WINDOWS = (2, 4, 8, 16)
POOL_GROUPS = 4
POOL_GROUP_DIM = BRANCH_DIM // POOL_GROUPS
POOL_STATE = 15
EPS = 1e-6
NEG = -1e30

LANES = 128
VMEM_LIMIT = 56 * 1024 * 1024

_ORIG = {}
_off = 0
for _name, _size in (("z", SSD_INNER), ("xbc", CONV_DIM), ("dt", SSD_HEADS), ("q", BRANCH_DIM),
                     ("k", KV_HEADS * HEAD_DIM), ("v", KV_HEADS * HEAD_DIM), ("ag", BRANCH_DIM),
                     ("iq", IDX_HEADS * IDX_DIM), ("ik", IDX_DIM), ("iw", IDX_HEADS),
                     ("u", BRANCH_DIM), ("pg", BRANCH_DIM), ("mg", N_BRANCH * D_MODEL)):
    _ORIG[_name] = (_off, _size)
    _off += _size
IN_DIM = _off

_NEW_ORDER = ("mg", "z", "q", "ag", "u", "pg", "xbc", "iq", "k", "v", "ik", "dt", "iw")
COL = {}
_off = 0
for _name in _NEW_ORDER:
    COL[_name] = _off
    _off += -(-_ORIG[_name][1] // LANES) * LANES
PROJ_DIM = -(-_off // 2048) * 2048
COL["xs"] = COL["xbc"]
COL["bm"] = COL["xbc"] + SSD_INNER
COL["cm"] = COL["bm"] + SSD_GROUPS * SSD_STATE


def _sortable_const(v):
    i = int(np.float32(v).view(np.int32))
    return i ^ ((i >> 31) & 0x7FFFFFFF)


NEG_KEY = _sortable_const(NEG)
INT_MIN = -2 ** 31


def _cparams(sem):
    return pltpu.CompilerParams(dimension_semantics=sem, vmem_limit_bytes=VMEM_LIMIT)


def _silu(x):
    return x * jax.nn.sigmoid(x)


def _ada_kernel(c_ref, w_ref, b_ref, o_ref):
    c = c_ref[...]
    o_ref[...] = jnp.dot(_silu(c).astype(BF16), w_ref[...].astype(BF16),
                         preferred_element_type=F32) + b_ref[...]


def ada_mod(c_all, w_ada, b_ada):
    nb, d = c_all.shape
    n = w_ada.shape[-1]
    tn = 512
    return pl.pallas_call(
        _ada_kernel,
        grid=(DEPTH, n // tn),
        in_specs=[pl.BlockSpec((nb, d), lambda l, j: (0, 0)),
                  pl.BlockSpec((None, d, tn), lambda l, j: (l, 0, j)),
                  pl.BlockSpec((None, 1, tn), lambda l, j: (l, 0, j))],
        out_specs=pl.BlockSpec((None, nb, tn), lambda l, j: (l, 0, j)),
        out_shape=jax.ShapeDtypeStruct((DEPTH, nb, n), F32),
        compiler_params=_cparams(("arbitrary", "arbitrary")),
        name="ada",
    )(c_all, w_ada, b_ada.reshape(DEPTH, 1, n))


def _inproj_kernel(x_ref, sc_ref, sh_ref, nw_ref, w_ref, o_ref, h_ref):
    @pl.when(pl.program_id(1) == 0)
    def _():
        x = x_ref[...]
        ms = jnp.mean(x * x, axis=-1, keepdims=True)
        y = x * lax.rsqrt(ms + EPS) * nw_ref[...]
        h_ref[...] = (y * (1.0 + sc_ref[...]) + sh_ref[...]).astype(BF16)

    o_ref[...] = jnp.dot(h_ref[...], w_ref[...], preferred_element_type=F32)


def _mod_spec(mod, tm, tn, rows_per_mod, col_of_j):
    r = mod.shape[1]
    if r == 1:
        tiles = rows_per_mod // tm
        return pl.BlockSpec((None, 1, tn), lambda i, j: (i // tiles, 0, col_of_j(j)))
    return pl.BlockSpec((None, r, tn), lambda i, j: (i, 0, col_of_j(j)))


def inproj(x, scale, shift, norm_w, w, tm, rows_per_mod):
    t, d = x.shape
    n = w.shape[1]
    tn = 1024
    zero = lambda j: 0
    return pl.pallas_call(
        _inproj_kernel,
        grid=(t // tm, n // tn),
        in_specs=[pl.BlockSpec((tm, d), lambda i, j: (i, 0)),
                  _mod_spec(scale, tm, d, rows_per_mod, zero),
                  _mod_spec(shift, tm, d, rows_per_mod, zero),
                  pl.BlockSpec((1, d), lambda i, j: (0, 0)),
                  pl.BlockSpec((d, tn), lambda i, j: (0, j))],
        out_specs=pl.BlockSpec((tm, tn), lambda i, j: (i, j)),
        out_shape=jax.ShapeDtypeStruct((t, n), F32),
        scratch_shapes=[pltpu.VMEM((tm, d), BF16)],
        compiler_params=_cparams(("arbitrary", "arbitrary")),
        name="inproj",
    )(x, scale, shift, norm_w.reshape(1, d), w)


def _prep_kernel(q_ref, k_ref, iq_ref, qw_ref, kw_ref, qh_ref, ko_ref, iqh_ref):
    def head_norm(xs, w):
        ms = jnp.mean(xs * xs, axis=-1, keepdims=True)
        return xs * lax.rsqrt(ms + EPS) * w

    q = q_ref[...]
    for h in range(ATTN_HEADS):
        qn = head_norm(q[:, h * HEAD_DIM:(h + 1) * HEAD_DIM], qw_ref[...])
        qh_ref[h] = (qn * HEAD_DIM ** -0.5).astype(BF16)
    k = k_ref[...]
    for g in range(KV_HEADS):
        ko_ref[:, g * HEAD_DIM:(g + 1) * HEAD_DIM] = head_norm(k[:, g * HEAD_DIM:(g + 1) * HEAD_DIM], kw_ref[...])
    iq = iq_ref[...]
    for h in range(IDX_HEADS):
        iqh_ref[h] = (iq[:, h * IDX_DIM:(h + 1) * IDX_DIM] * IDX_DIM ** -0.5).astype(BF16)


def prep(proj, q_norm_w, k_norm_w, tm):
    t = proj.shape[0]
    kw = KV_HEADS * HEAD_DIM
    iqw = IDX_HEADS * IDX_DIM
    return pl.pallas_call(
        _prep_kernel,
        grid=(t // tm,),
        in_specs=[pl.BlockSpec((tm, BRANCH_DIM), lambda i: (i, COL["q"] // BRANCH_DIM)),
                  pl.BlockSpec((tm, kw), lambda i: (i, COL["k"] // kw)),
                  pl.BlockSpec((tm, iqw), lambda i: (i, COL["iq"] // iqw)),
                  pl.BlockSpec((1, HEAD_DIM), lambda i: (0, 0)),
                  pl.BlockSpec((1, HEAD_DIM), lambda i: (0, 0))],
        out_specs=[pl.BlockSpec((ATTN_HEADS, tm, HEAD_DIM), lambda i: (0, i, 0)),
                   pl.BlockSpec((tm, kw), lambda i: (i, 0)),
                   pl.BlockSpec((IDX_HEADS, tm, IDX_DIM), lambda i: (0, i, 0))],
        out_shape=[jax.ShapeDtypeStruct((ATTN_HEADS, t, HEAD_DIM), BF16),
                   jax.ShapeDtypeStruct((t, kw), F32),
                   jax.ShapeDtypeStruct((IDX_HEADS, t, IDX_DIM), BF16)],
        compiler_params=_cparams(("arbitrary",)),
        name="prep",
    )(proj, proj, proj, q_norm_w.reshape(1, HEAD_DIM), k_norm_w.reshape(1, HEAD_DIM))


def _prep_t_kernel(q_ref, k_ref, v_ref, iq_ref, ik_ref, iw_ref, qw_ref, kw_ref,
                   qT_ref, iqT_ref, iwT_ref, ko_ref, vo_ref, iko_ref, ka_ref, va_ref, ikb_ref):
    def head_norm(xs, w):
        ms = jnp.mean(xs * xs, axis=-1, keepdims=True)
        return xs * lax.rsqrt(ms + EPS) * w

    tm = q_ref.shape[0]
    q = q_ref[...]
    qn = jnp.concatenate([head_norm(q[:, h * HEAD_DIM:(h + 1) * HEAD_DIM], qw_ref[...])
                          for h in range(ATTN_HEADS)], axis=1)
    qT_ref[...] = (qn * HEAD_DIM ** -0.5).T.reshape(ATTN_HEADS, HEAD_DIM, tm).astype(BF16)
    iqT_ref[...] = (iq_ref[...] * IDX_DIM ** -0.5).T.reshape(IDX_HEADS, IDX_DIM, tm).astype(BF16)
    iwT_ref[...] = iw_ref[...].T[0:IDX_HEADS, :]

    k = k_ref[...]
    lane = lax.broadcasted_iota(I32, (tm, AUG_K), 1)
    for g in range(KV_HEADS):
        kn = head_norm(k[:, g * HEAD_DIM:(g + 1) * HEAD_DIM], kw_ref[...])
        ko_ref[:, g * HEAD_DIM:(g + 1) * HEAD_DIM] = kn
        kpad = jnp.concatenate([kn, jnp.zeros((tm, AUG_K - HEAD_DIM), F32)], axis=1)
        ka_ref[g] = jnp.where(lane == HEAD_DIM, 1.0, kpad).astype(BF16)

    v = v_ref[...]
    vo_ref[...] = v
    vt = v.T
    row = lax.broadcasted_iota(I32, (V_ROWS - HEAD_DIM, tm), 0)
    tail = jnp.where(row == 0, 1.0, 0.0).astype(BF16)
    for g in range(KV_HEADS):
        va_ref[g, 0:HEAD_DIM, :] = vt[g * HEAD_DIM:(g + 1) * HEAD_DIM, :].astype(BF16)
        va_ref[g, HEAD_DIM:V_ROWS, :] = tail

    ik = ik_ref[...][:, 0:IDX_DIM]
    iko_ref[...] = ik
    ikb_ref[...] = ik.astype(BF16)


def prep_t(proj, q_norm_w, k_norm_w, tm, nbatch):
    t = proj.shape[0]
    s = t // nbatch
    tiles = s // tm
    kw = KV_HEADS * HEAD_DIM
    iqw = IDX_HEADS * IDX_DIM
    row = lambda w, name: pl.BlockSpec((tm, w), lambda i: (i, COL[name] // w))
    return pl.pallas_call(
        _prep_t_kernel,
        grid=(t // tm,),
        in_specs=[row(BRANCH_DIM, "q"), row(kw, "k"), row(kw, "v"), row(iqw, "iq"),
                  row(LANES, "ik"), row(LANES, "iw"),
                  pl.BlockSpec((1, HEAD_DIM), lambda i: (0, 0)),
                  pl.BlockSpec((1, HEAD_DIM), lambda i: (0, 0))],
        out_specs=[pl.BlockSpec((ATTN_HEADS, HEAD_DIM, tm), lambda i: (0, 0, i)),
                   pl.BlockSpec((IDX_HEADS, IDX_DIM, tm), lambda i: (0, 0, i)),
                   pl.BlockSpec((IDX_HEADS, tm), lambda i: (0, i)),
                   pl.BlockSpec((tm, kw), lambda i: (i, 0)),
                   pl.BlockSpec((tm, kw), lambda i: (i, 0)),
                   pl.BlockSpec((tm, IDX_DIM), lambda i: (i, 0)),
                   pl.BlockSpec((None, KV_HEADS, tm, AUG_K), lambda i: (i // tiles, 0, i % tiles, 0)),
                   pl.BlockSpec((None, KV_HEADS, V_ROWS, tm), lambda i: (i // tiles, 0, 0, i % tiles)),
                   pl.BlockSpec((None, tm, IDX_DIM), lambda i: (i // tiles, i % tiles, 0))],
        out_shape=[jax.ShapeDtypeStruct((ATTN_HEADS, HEAD_DIM, t), BF16),
                   jax.ShapeDtypeStruct((IDX_HEADS, IDX_DIM, t), BF16),
                   jax.ShapeDtypeStruct((IDX_HEADS, t), F32),
                   jax.ShapeDtypeStruct((t, kw), F32),
                   jax.ShapeDtypeStruct((t, kw), F32),
                   jax.ShapeDtypeStruct((t, IDX_DIM), F32),
                   jax.ShapeDtypeStruct((nbatch, KV_HEADS, s, AUG_K), BF16),
                   jax.ShapeDtypeStruct((nbatch, KV_HEADS, V_ROWS, s), BF16),
                   jax.ShapeDtypeStruct((nbatch, s, IDX_DIM), BF16)],
        compiler_params=_cparams(("arbitrary",)),
        name="prep_t",
    )(proj, proj, proj, proj, proj, proj, q_norm_w.reshape(1, HEAD_DIM), k_norm_w.reshape(1, HEAD_DIM))


TKS = 256
NEAR_W = 2 * TKS
FAR_BUCKET = N_BUCKETS // 2 - 1
J_ALL = 2 ** 30


def _dsa_kernel(tab_ref, qh_ref, iqh_ref, iw_ref, ikT_ref, kT_ref, v_ref, bk_ref, ag_ref,
                o_ref, keys_ref, nb_ref, acc_ref, m_ref, l_ref, j_ref, *,
                tq, ta, tk, start, n_keys, topk):
    b = pl.program_id(0)
    i = pl.program_id(1)
    kk = pl.program_id(2)
    nk = pl.num_programs(2)
    q0 = start + i * tq
    vis_end = ((q0 + tq - 1) // CHUNK + 1) * CHUNK
    n_cols = jnp.minimum(vis_end, n_keys)
    n_a = (n_cols + ta - 1) // ta
    n_sub = (n_cols + TKS - 1) // TKS
    extra = jnp.maximum(n_keys - n_a * ta, 0).astype(F32)
    sub_per_tile = tk // TKS

    def vis_mask(t, width):
        kpos = t * width + lax.broadcasted_iota(I32, (tq, width), 1)
        qpos = q0 + lax.broadcasted_iota(I32, (tq, width), 0)
        vis = (kpos // CHUNK) <= (qpos // CHUNK)
        return vis, kpos

    @pl.when((b == 0) & (i == 0) & (kk == 0))
    def _():
        bk = bk_ref[...]

        def per_head(h, _):
            def per_bucket(n, val):
                return jnp.where(bk == n, tab_ref[n, h], val)
            val = lax.fori_loop(0, N_BUCKETS, per_bucket, jnp.zeros((tq, NEAR_W), F32))
            nb_ref[h] = val - tab_ref[FAR_BUCKET, h]
            return 0
        lax.fori_loop(0, ATTN_HEADS, per_head, 0)

    @pl.when(kk == 0)
    def _():
        iw = iw_ref[...] * IDX_HEADS ** -0.5

        def score_tile(t, _):
            c0 = pl.multiple_of(t * ta, ta)
            ikt = ikT_ref[:, pl.ds(c0, ta)]
            acc = jnp.zeros((tq, ta), F32)
            for h in range(IDX_HEADS):
                s = jnp.dot(iqh_ref[h], ikt, preferred_element_type=F32)
                acc = acc + iw[:, h:h + 1] * jnp.maximum(s, 0.0)
            vis, kpos = vis_mask(t, ta)
            acc = jnp.where(vis, acc, NEG)
            bits = pltpu.bitcast(acc, I32)
            key = bits ^ ((bits >> 31) & 0x7FFFFFFF)
            key = jnp.where(kpos < n_keys, key, INT_MIN)
            keys_ref[:, pl.ds(c0, ta)] = key
            return 0
        lax.fori_loop(0, n_a, score_tile, 0)

        def count(pred):
            def body(t, cnt):
                c0 = pl.multiple_of(t * ta, ta)
                blk = keys_ref[:, pl.ds(c0, ta)]
                c = jnp.where(pred(blk, t), 1.0, 0.0)
                for jj in range(ta // LANES):
                    cnt = cnt + c[:, jj * LANES:(jj + 1) * LANES]
                return cnt
            cnt = lax.fori_loop(0, n_a, body, jnp.zeros((tq, LANES), F32))
            return jnp.sum(cnt, axis=1, keepdims=True)

        def bit_step(it, prefix_u):
            bit = jnp.left_shift(jnp.int32(1), 31 - it)
            cand_u = prefix_u | bit
            cand_s = cand_u ^ INT_MIN
            cnt = count(lambda blk, t: blk >= cand_s) + jnp.where(NEG_KEY >= cand_s, extra, 0.0)
            return jnp.where(cnt >= topk, cand_u, prefix_u)
        prefix_u = lax.fori_loop(0, 32, bit_step, jnp.zeros((tq, 1), I32))
        thr = prefix_u ^ INT_MIN

        cnt_gt = count(lambda blk, t: blk > thr) + jnp.where(NEG_KEY > thr, extra, 0.0)
        cnt_eq = count(lambda blk, t: blk == thr)
        need = topk - cnt_gt
        j_ref[...] = jnp.full((tq, LANES), J_ALL, I32)

        @pl.when(jnp.max(cnt_eq - need) > 0.0)
        def _():
            def idx_step(it, jmax):
                cand = jmax | jnp.left_shift(jnp.int32(1), 14 - it)

                def pred(blk, t):
                    _, kpos = vis_mask(t, ta)
                    return (blk == thr) & (kpos < cand)
                g = count(pred)
                return jnp.where(g < need, cand, jmax)
            jmax = lax.fori_loop(0, 15, idx_step, jnp.zeros((tq, 1), I32))
            j_ref[...] = jnp.broadcast_to(jmax, (tq, LANES))

        jmax = j_ref[:, 0:1]

        def mask_tile(t, _):
            c0 = pl.multiple_of(t * ta, ta)
            blk = keys_ref[:, pl.ds(c0, ta)]
            vis, kpos = vis_mask(t, ta)
            sel = (blk > thr) | ((blk == thr) & (kpos <= jmax))
            ok = sel & vis & (kpos < n_keys)
            madd = jnp.where(ok, 0.0, NEG).astype(F32)
            keys_ref[:, pl.ds(c0, ta)] = pltpu.bitcast(madd, I32)
            return 0
        lax.fori_loop(0, n_a, mask_tile, 0)

        m_ref[...] = jnp.full(m_ref.shape, -jnp.inf, F32)
        l_ref[...] = jnp.zeros(l_ref.shape, F32)
        acc_ref[...] = jnp.zeros(acc_ref.shape, F32)

    def attend(jl, boff):
        lo = pl.multiple_of(jl * TKS, TKS)
        c0 = pl.multiple_of(kk * tk + lo, TKS)
        madd = pltpu.bitcast(keys_ref[:, pl.ds(c0, TKS)], F32)

        logits = [madd + jnp.dot(qh_ref[h], kT_ref[h // Q_PER_KV, :, pl.ds(lo, TKS)],
                                 preferred_element_type=F32) for h in range(ATTN_HEADS)]
        for h in range(ATTN_HEADS):
            s = logits[h]
            if boff is not None:
                s = s + nb_ref[h, :, boff:boff + TKS]
            m_prev = m_ref[h]
            m_new = jnp.maximum(m_prev, jnp.max(s, axis=1, keepdims=True))
            alpha = jnp.exp(m_prev - m_new)
            p = jnp.exp(s - m_new[:, 0:1])
            l_ref[h] = alpha * l_ref[h] + jnp.sum(p, axis=1, keepdims=True)
            acc_ref[h] = alpha[:, 0:HEAD_DIM] * acc_ref[h] + jnp.dot(
                p.astype(BF16), v_ref[h // Q_PER_KV, pl.ds(lo, TKS), :], preferred_element_type=F32)
            m_ref[h] = m_new

    n_here = jnp.clip(n_sub - kk * sub_per_tile, 0, sub_per_tile)

    def sub_body(jl, _):
        u = kk * sub_per_tile + jl

        @pl.when(u < n_sub - 2)
        def _():
            attend(jl, None)

        @pl.when(u == n_sub - 2)
        def _():
            attend(jl, 0)

        @pl.when(u == n_sub - 1)
        def _():
            attend(jl, TKS)
        return 0
    lax.fori_loop(0, n_here, sub_body, 0)

    @pl.when(kk == nk - 1)
    def _():
        ag = ag_ref[...]
        for h in range(ATTN_HEADS):
            o = acc_ref[h] / l_ref[h][:, 0:HEAD_DIM]
            sl = slice(h * HEAD_DIM, (h + 1) * HEAD_DIM)
            o_ref[:, sl] = (o * _silu(ag[:, sl])).astype(o_ref.dtype)


def dsa(qh, iqh, proj, ikT, kT, v4, bk, rel_bias, *, nbatch, tq, ta, tk, start, n_keys, topk):
    t = qh.shape[1]
    s_pad = ikT.shape[-1]
    nq = t // (nbatch * tq)
    nk = s_pad // tk

    def kt_idx(b, i, kk):
        q0 = start + i * tq
        n_cols = jnp.minimum(((q0 + tq - 1) // CHUNK + 1) * CHUNK, n_keys)
        return jnp.minimum(kk, (n_cols + tk - 1) // tk - 1)

    kern = functools.partial(_dsa_kernel, tq=tq, ta=ta, tk=tk, start=start, n_keys=n_keys, topk=topk)
    return pl.pallas_call(
        kern,
        grid=(nbatch, nq, nk),
        in_specs=[pl.BlockSpec(memory_space=pltpu.SMEM),
                  pl.BlockSpec((ATTN_HEADS, tq, HEAD_DIM), lambda b, i, kk: (0, b * nq + i, 0)),
                  pl.BlockSpec((IDX_HEADS, tq, IDX_DIM), lambda b, i, kk: (0, b * nq + i, 0)),
                  pl.BlockSpec((tq, LANES), lambda b, i, kk: (b * nq + i, COL["iw"] // LANES)),
                  pl.BlockSpec((None, IDX_DIM, s_pad), lambda b, i, kk: (b, 0, 0)),
                  pl.BlockSpec((None, KV_HEADS, HEAD_DIM, tk), lambda b, i, kk: (b, 0, 0, kt_idx(b, i, kk))),
                  pl.BlockSpec((None, KV_HEADS, tk, HEAD_DIM), lambda b, i, kk: (b, 0, kt_idx(b, i, kk), 0)),
                  pl.BlockSpec((tq, NEAR_W), lambda b, i, kk: (0, 0)),
                  pl.BlockSpec((tq, BRANCH_DIM), lambda b, i, kk: (b * nq + i, COL["ag"] // BRANCH_DIM))],
        out_specs=pl.BlockSpec((tq, BRANCH_DIM), lambda b, i, kk: (b * nq + i, 0)),
        out_shape=jax.ShapeDtypeStruct((t, BRANCH_DIM), BF16),
        scratch_shapes=[pltpu.VMEM((tq, s_pad), I32),
                        pltpu.VMEM((ATTN_HEADS, tq, NEAR_W), F32),
                        pltpu.VMEM((ATTN_HEADS, tq, HEAD_DIM), F32),
                        pltpu.VMEM((ATTN_HEADS, tq, LANES), F32),
                        pltpu.VMEM((ATTN_HEADS, tq, LANES), F32),
                        pltpu.VMEM((tq, LANES), I32)],
        compiler_params=_cparams(("arbitrary", "arbitrary", "arbitrary")),
        name="dsa",
    )(rel_bias, qh, iqh, proj, ikT, kT, v4, bk, proj)


AUG_K = 128
V_ROWS = 80
CNT_ROWS = 64
FAST_BOUND = 30.0
GM_ROWS = 256
QK_LOOKAHEAD = 8


def _dsat_kernel(tab_ref, kmax_ref, qT_ref, iqT_ref, iw_ref, ik_ref, k_ref, vT_ref, bk_ref, ag_ref,
                 o_ref, keys_ref, nb_ref, acc_ref, m_ref, qa_ref, j_ref, gm_ref, bmax_ref, fast_ref, *,
                 tq, ta, tk, start, n_keys, topk):
    b = pl.program_id(0)
    i = pl.program_id(1)
    kk = pl.program_id(2)
    nk = pl.num_programs(2)
    q0 = start + i * tq
    vis_end = ((q0 + tq - 1) // CHUNK + 1) * CHUNK
    n_cols = jnp.minimum(vis_end, n_keys)
    n_a = (n_cols + ta - 1) // ta
    n_sub = (n_cols + TKS - 1) // TKS
    extra = jnp.maximum(n_keys - n_a * ta, 0).astype(F32)
    sub_per_tile = tk // TKS

    def key_pos(t):
        return t * ta + lax.broadcasted_iota(I32, (ta, tq), 0)

    def vis_of(kpos):
        qpos = q0 + lax.broadcasted_iota(I32, (ta, tq), 1)
        return (kpos >> 6) <= (qpos >> 6)

    def colsum(x):
        return jnp.sum(x.reshape(ta // CNT_ROWS, CNT_ROWS, tq), axis=0)

    @pl.when((b == 0) & (i == 0) & (kk == 0))
    def _():
        bk = bk_ref[...]

        def per_head(h, bmax):
            def per_bucket(n, val):
                return jnp.where(bk == n, tab_ref[n, h], val)
            val = lax.fori_loop(0, N_BUCKETS, per_bucket, jnp.zeros((NEAR_W, tq), F32))
            val = val - tab_ref[FAR_BUCKET, h]
            nb_ref[h] = val
            return jnp.maximum(bmax, jnp.max(val))
        bmax_ref[0] = lax.fori_loop(0, ATTN_HEADS, per_head, jnp.float32(0.0))

    @pl.when(kk == 0)
    def _():
        rows = lax.broadcasted_iota(I32, (AUG_K, tq), 0)
        worst = jnp.zeros((1, tq), F32)
        for h in range(ATTN_HEADS):
            q = qT_ref[h].astype(F32)
            nrm = jnp.sqrt(jnp.sum(q * q, axis=0, keepdims=True))
            bound = nrm * kmax_ref[b, h // Q_PER_KV] + bmax_ref[0]
            worst = jnp.maximum(worst, bound)
            qpad = jnp.concatenate([q, jnp.zeros((AUG_K - HEAD_DIM, tq), F32)], axis=0)
            qa_ref[h] = jnp.where(rows == HEAD_DIM, -bound, qpad).astype(BF16)
        fast_ref[0] = (jnp.max(worst) <= FAST_BOUND).astype(I32)

        iw = iw_ref[...] * IDX_HEADS ** -0.5

        def score_tile(t, edge):
            c0 = pl.multiple_of(t * ta, ta)
            ikt = ik_ref[pl.ds(c0, ta), :]
            acc = jnp.zeros((ta, tq), F32)
            for h in range(IDX_HEADS):
                s = jnp.dot(ikt, iqT_ref[h], preferred_element_type=F32)
                acc = acc + iw[h:h + 1, :] * jnp.maximum(s, 0.0)
            if edge:
                kpos = key_pos(t)
                acc = jnp.where(vis_of(kpos), acc, NEG)
            bits = pltpu.bitcast(acc, I32)
            key = bits ^ ((bits >> 31) & 0x7FFFFFFF)
            if edge:
                key = jnp.where(kpos < n_keys, key, INT_MIN)
                acc = jnp.where(kpos < n_keys, acc, -jnp.inf)
            keys_ref[pl.ds(c0, ta), :] = key
            gm_ref[...] = jnp.maximum(gm_ref[...], jnp.max(acc.reshape(ta // GM_ROWS, GM_ROWS, tq), axis=0))
        gm_ref[...] = jnp.full((GM_ROWS, tq), -jnp.inf, F32)
        lax.fori_loop(0, n_a - 1, lambda t, _: (score_tile(t, False), 0)[1], 0)
        score_tile(n_a - 1, True)

        def count(pred):
            def body(t, cnt):
                c0 = pl.multiple_of(t * ta, ta)
                blk = keys_ref[pl.ds(c0, ta), :]
                return cnt + colsum(jnp.where(pred(blk, t), 1.0, 0.0))
            cnt = lax.fori_loop(0, n_a, body, jnp.zeros((CNT_ROWS, tq), F32))
            return jnp.sum(cnt, axis=0, keepdims=True)

        def count_ge(p):
            def body(t, cnt):
                c0 = pl.multiple_of(t * ta, ta)
                for jj in range(ta // CNT_ROWS):
                    blk = keys_ref[pl.ds(c0 + jj * CNT_ROWS, CNT_ROWS), :]
                    cnt = cnt + jnp.where(blk >= p, 1.0, 0.0)
                return cnt
            cnt = lax.fori_loop(0, n_a, body, jnp.zeros((CNT_ROWS, tq), F32))
            return jnp.sum(cnt, axis=0, keepdims=True) + jnp.where(NEG_KEY >= p, extra, 0.0)

        def key_of(f):
            bits = pltpu.bitcast(f, I32)
            return bits ^ ((bits >> 31) & 0x7FFFFFFF)

        gm = gm_ref[...]
        lo0 = key_of(jnp.min(gm, axis=0, keepdims=True))
        hi0 = key_of(jnp.max(gm, axis=0, keepdims=True)) + 1
        c_lo0 = jnp.full((1, tq), float(2 ** 24), F32)

        def active_of(lo, hi, c_lo):
            return (c_lo != topk) & (lo + 1 < hi)

        def search_cond(st):
            return st[5] > 0.0

        def search_body(st):
            lo, hi, c_lo, c_hi, it, _ = st
            mid = (lo >> 1) + (hi >> 1) + (lo & hi & 1)
            zero_pivot = (it < 2) & (lo < it) & (it < hi)
            p = jnp.where(zero_pivot, it, mid)
            p = jnp.minimum(jnp.maximum(p, lo + 1), hi - 1)
            c = count_ge(p)
            act = active_of(lo, hi, c_lo)
            up = act & (c >= topk)
            dn = act & (c < topk)
            lo, c_lo = jnp.where(up, p, lo), jnp.where(up, c, c_lo)
            hi, c_hi = jnp.where(dn, p, hi), jnp.where(dn, c, c_hi)
            left = jnp.max(jnp.where(active_of(lo, hi, c_lo), 1.0, 0.0))
            return lo, hi, c_lo, c_hi, it + 1, left

        st0 = (lo0, hi0, c_lo0, jnp.zeros((1, tq), F32), jnp.int32(0),
               jnp.max(jnp.where(active_of(lo0, hi0, c_lo0), 1.0, 0.0)))
        thr, _, c_thr, _, _, _ = lax.while_loop(search_cond, search_body, st0)
        j_ref[...] = jnp.full((8, tq), J_ALL, I32)

        @pl.when(jnp.max(c_thr) > topk)
        def _():
            cnt_gt = count(lambda blk, t: blk > thr) + jnp.where(NEG_KEY > thr, extra, 0.0)
            need = topk - cnt_gt

            def idx_step(it, jmax):
                cand = jmax | jnp.left_shift(jnp.int32(1), 14 - it)
                g = count(lambda blk, t: (blk == thr) & (key_pos(t) < cand))
                return jnp.where(g < need, cand, jmax)
            jmax = lax.fori_loop(0, 15, idx_step, jnp.zeros((1, tq), I32))
            j_ref[...] = jnp.broadcast_to(jmax, (8, tq))

        jmax = j_ref[0:1, :]

        def mask_tile(t, edge):
            c0 = pl.multiple_of(t * ta, ta)
            blk = keys_ref[pl.ds(c0, ta), :]
            kpos = key_pos(t)
            ok = (blk > thr) | ((blk == thr) & (kpos <= jmax))
            if edge:
                ok = ok & vis_of(kpos) & (kpos < n_keys)
            keys_ref[pl.ds(c0, ta), :] = pltpu.bitcast(jnp.where(ok, 1.0, 0.0).astype(F32), I32)
        lax.fori_loop(0, n_a - 1, lambda t, _: (mask_tile(t, False), 0)[1], 0)
        mask_tile(n_a - 1, True)

        m_ref[...] = jnp.full(m_ref.shape, -jnp.inf, F32)
        acc_ref[...] = jnp.zeros(acc_ref.shape, F32)

    def attend_fast(jl, boff):
        lo = pl.multiple_of(jl * TKS, TKS)
        c0 = pl.multiple_of(kk * tk + lo, TKS)
        mask = pltpu.bitcast(keys_ref[pl.ds(c0, TKS), :], F32)

        def logits(h):
            kt = k_ref[h // Q_PER_KV, pl.ds(lo, TKS), :]
            return jnp.dot(kt, qa_ref[h], preferred_element_type=F32)

        pending = [logits(h) for h in range(QK_LOOKAHEAD)]
        for h in range(ATTN_HEADS):
            if h + QK_LOOKAHEAD < ATTN_HEADS:
                pending.append(logits(h + QK_LOOKAHEAD))
            s = pending.pop(0)
            if boff is not None:
                s = nb_ref[h, boff:boff + TKS, :] + s
            p = (jnp.exp(s) * mask).astype(BF16)
            vt = vT_ref[h // Q_PER_KV, :, pl.ds(lo, TKS)]
            acc_ref[h] = acc_ref[h] + jnp.dot(vt, p, preferred_element_type=F32)

    def attend_slow(jl, boff):
        lo = pl.multiple_of(jl * TKS, TKS)
        c0 = pl.multiple_of(kk * tk + lo, TKS)
        madd = (pltpu.bitcast(keys_ref[pl.ds(c0, TKS), :], F32) - 1.0) * (-NEG)

        def group(g, _):
            kt = k_ref[g, pl.ds(lo, TKS), :]
            vt = vT_ref[g, :, pl.ds(lo, TKS)]
            for r in range(Q_PER_KV):
                h = g * Q_PER_KV + r
                s = jnp.dot(kt, qa_ref[h], preferred_element_type=F32) + madd
                if boff is not None:
                    s = s + nb_ref[h, boff:boff + TKS, :]
                m_prev = m_ref[h]
                m_new = jnp.maximum(m_prev, jnp.max(s, axis=0, keepdims=True))
                alpha = jnp.exp(m_prev - m_new)
                p = jnp.exp(s - m_new[0:1, :])
                acc_ref[h] = alpha[0:1, :] * acc_ref[h] + jnp.dot(
                    vt, p.astype(BF16), preferred_element_type=F32)
                m_ref[h] = m_new
            return 0
        lax.fori_loop(0, KV_HEADS, group, 0)

    n_here = jnp.clip(n_sub - kk * sub_per_tile, 0, sub_per_tile)
    fast = fast_ref[0] == 1

    def sub_loop(attend):
        def sub_body(jl, _):
            u = kk * sub_per_tile + jl

            @pl.when(u < n_sub - 2)
            def _():
                attend(jl, None)

            @pl.when(u == n_sub - 2)
            def _():
                attend(jl, 0)

            @pl.when(u == n_sub - 1)
            def _():
                attend(jl, TKS)
            return 0
        lax.fori_loop(0, n_here, sub_body, 0)

    @pl.when(fast)
    def _():
        sub_loop(attend_fast)

    @pl.when(jnp.logical_not(fast))
    def _():
        sub_loop(attend_slow)

    @pl.when(kk == nk - 1)
    def _():
        outs = []
        for h in range(ATTN_HEADS):
            a = acc_ref[h]
            outs.append(a[0:HEAD_DIM, :] / a[HEAD_DIM:HEAD_DIM + 1, :])
        o = jnp.concatenate(outs, axis=0).T
        o_ref[...] = (o * _silu(ag_ref[...])).astype(o_ref.dtype)


def augment_kv(k4, vT, n_keys):
    s_pad = k4.shape[2]
    real = (jnp.arange(s_pad) < n_keys).astype(BF16)
    k_aug = jnp.concatenate([k4, jnp.broadcast_to(real[None, None, :, None], k4.shape[:3] + (1,)),
                             jnp.zeros(k4.shape[:3] + (AUG_K - HEAD_DIM - 1,), BF16)], axis=-1)
    v_aug = jnp.concatenate([vT, jnp.broadcast_to(real[None, None, None, :], vT.shape[:2] + (1, s_pad)),
                             jnp.zeros(vT.shape[:2] + (V_ROWS - HEAD_DIM - 1, s_pad), BF16)], axis=2)
    return k_aug, v_aug


def dsa_t(qT, iqT, iwT, proj, ik, k_aug, v_aug, bkT, rel_bias, *, nbatch, tq, ta, tk, start, n_keys, topk):
    t = qT.shape[2]
    s_pad = ik.shape[1]
    nq = t // (nbatch * tq)
    nk = s_pad // tk
    assert tq % LANES == 0 and ta % tq == 0 and tk % TKS == 0 and s_pad % ta == 0 and ta % CNT_ROWS == 0
    assert ta % GM_ROWS == 0 and topk <= GM_ROWS <= start + tq

    kf = k_aug[..., :HEAD_DIM].astype(F32)
    kmax = jnp.sqrt(jnp.max(jnp.sum(kf * kf, axis=-1), axis=-1))

    def kt_idx(b, i, kk):
        q0 = start + i * tq
        n_cols = jnp.minimum(((q0 + tq - 1) // CHUNK + 1) * CHUNK, n_keys)
        return jnp.minimum(kk, (n_cols + tk - 1) // tk - 1)

    kern = functools.partial(_dsat_kernel, tq=tq, ta=ta, tk=tk, start=start, n_keys=n_keys, topk=topk)
    smem = pl.BlockSpec(memory_space=pltpu.SMEM)
    return pl.pallas_call(
        kern,
        grid=(nbatch, nq, nk),
        in_specs=[smem, smem,
                  pl.BlockSpec((ATTN_HEADS, HEAD_DIM, tq), lambda b, i, kk: (0, 0, b * nq + i)),
                  pl.BlockSpec((IDX_HEADS, IDX_DIM, tq), lambda b, i, kk: (0, 0, b * nq + i)),
                  pl.BlockSpec((IDX_HEADS, tq), lambda b, i, kk: (0, b * nq + i)),
                  pl.BlockSpec((None, s_pad, IDX_DIM), lambda b, i, kk: (b, 0, 0)),
                  pl.BlockSpec((None, KV_HEADS, tk, AUG_K), lambda b, i, kk: (b, 0, kt_idx(b, i, kk), 0)),
                  pl.BlockSpec((None, KV_HEADS, V_ROWS, tk), lambda b, i, kk: (b, 0, 0, kt_idx(b, i, kk))),
                  pl.BlockSpec((NEAR_W, tq), lambda b, i, kk: (0, 0)),
                  pl.BlockSpec((tq, BRANCH_DIM), lambda b, i, kk: (b * nq + i, COL["ag"] // BRANCH_DIM))],
        out_specs=pl.BlockSpec((tq, BRANCH_DIM), lambda b, i, kk: (b * nq + i, 0)),
        out_shape=jax.ShapeDtypeStruct((t, BRANCH_DIM), BF16),
        scratch_shapes=[pltpu.VMEM((s_pad, tq), I32),
                        pltpu.VMEM((ATTN_HEADS, NEAR_W, tq), F32),
                        pltpu.VMEM((ATTN_HEADS, V_ROWS, tq), F32),
                        pltpu.VMEM((ATTN_HEADS, 8, tq), F32),
                        pltpu.VMEM((ATTN_HEADS, AUG_K, tq), BF16),
                        pltpu.VMEM((8, tq), I32),
                        pltpu.VMEM((GM_ROWS, tq), F32),
                        pltpu.SMEM((1,), F32),
                        pltpu.SMEM((1,), I32)],
        compiler_params=_cparams(("arbitrary", "arbitrary", "arbitrary")),
        name="dsa_t",
    )(rel_bias, kmax, qT, iqT, iwT, ik, k_aug, v_aug, bkT, proj)


def t5_bucket(rel):
    half = N_BUCKETS // 2
    max_exact = half // 2
    ret = jnp.where(rel > 0, half, 0)
    n = jnp.abs(rel)
    nf = jnp.maximum(n, max_exact).astype(jnp.float32)
    large = max_exact + (jnp.log(nf / max_exact) / math.log(MAX_DISTANCE / max_exact)
                         * (half - max_exact)).astype(jnp.int32)
    large = jnp.minimum(large, half - 1)
    return ret + jnp.where(n < max_exact, n, large)


def near_buckets(tq, q0, n_keys):
    vis_end = ((q0 + tq - 1) // CHUNK + 1) * CHUNK
    end = -(-min(vis_end, n_keys) // TKS) * TKS
    assert end - NEAR_W <= q0 - MAX_DISTANCE + 1, "near window must cover every non-saturated offset"
    kpos = end - NEAR_W + jnp.arange(NEAR_W, dtype=jnp.int32)[None, :]
    qpos = q0 + jnp.arange(tq, dtype=jnp.int32)[:, None]
    return t5_bucket(kpos - qpos)


def _ssd_kernel(xs_ref, bm_ref, cm_ref, dt_ref, z_ref, hist_ref, st0_ref, cw_ref, cb_ref,
                dtb_ref, alog_ref, dsk_ref, nw_ref, exp_ref, y_ref, st_ref,
                ext_ref, state_ref, *, q, valid):
    c = pl.program_id(1)
    nc = pl.num_programs(1)
    hp = SSD_INNER
    gw = hp // SSD_GROUPS
    hi = lax.Precision.HIGHEST

    @pl.when(c == 0)
    def _():
        ext_ref[0:8, :] = hist_ref[...]
        state_ref[...] = st0_ref[...]

    ext_ref[8:8 + q, 0:hp] = xs_ref[...]
    ext_ref[8:8 + q, hp:hp + 256] = bm_ref[...]
    ext_ref[8:8 + q, hp + 256:hp + 512] = cm_ref[...]
    conv = jnp.zeros((q, CONV_DIM), F32) + cb_ref[...]
    for j in range(CONV_WIDTH):
        conv = conv + ext_ref[5 + j:5 + j + q, :] * cw_ref[j:j + 1, :]
    ext_ref[0:8, :] = ext_ref[q:q + 8, :]
    conv = _silu(conv)
    xs = conv[:, 0:hp]
    bmat = conv[:, hp:hp + 256]
    cmat = conv[:, hp + 256:hp + 512]

    xdt = dt_ref[...] + dtb_ref[...]
    dt = jnp.maximum(xdt, 0.0) + jnp.log1p(jnp.exp(-jnp.abs(xdt)))
    if valid < q:
        rows = lax.broadcasted_iota(I32, (q, LANES), 0)
        dt = jnp.where(rows < valid, dt, 0.0)
    adt = dt * (-jnp.exp(alog_ref[...]))
    rr = lax.broadcasted_iota(I32, (q, q), 0)
    cc = lax.broadcasted_iota(I32, (q, q), 1)
    tri = rr >= cc
    acum = jnp.dot(tri.astype(F32), adt, precision=hi, preferred_element_type=F32)
    acum_t = acum.T
    dt_t = dt.T
    alast = acum[q - 1:q, :]

    expand = exp_ref[...]
    e_acum = jnp.dot(jnp.exp(acum), expand, precision=hi, preferred_element_type=F32)
    e_tail = jnp.dot(jnp.exp(alast - acum) * dt, expand, precision=hi, preferred_element_type=F32)
    e_last = e_acum[q - 1:q, :]

    xw = (xs * e_tail).astype(BF16)
    xb = xs.astype(BF16)
    y_parts = []
    new_state = []
    for g in range(SSD_GROUPS):
        bg = bmat[:, g * SSD_STATE:(g + 1) * SSD_STATE]
        cg = cmat[:, g * SSD_STATE:(g + 1) * SSD_STATE].astype(BF16)
        bg_t = bg.T.astype(BF16)
        cb = jnp.dot(cg, bg_t, preferred_element_type=F32)
        st_g = state_ref[:, g * gw:(g + 1) * gw]
        y_off = jnp.dot(cg, st_g.astype(BF16), preferred_element_type=F32)
        new_state.append(jnp.dot(bg_t, xw[:, g * gw:(g + 1) * gw], preferred_element_type=F32))
        heads = []
        for r in range(SSD_HEADS // SSD_GROUPS):
            h = g * (SSD_HEADS // SSD_GROUPS) + r
            seg = acum[:, h:h + 1] - acum_t[h:h + 1, :]
            decay = jnp.where(tri, jnp.exp(jnp.where(tri, seg, 0.0)), 0.0)
            wmat = (cb * decay * dt_t[h:h + 1, :]).astype(BF16)
            heads.append(jnp.dot(wmat, xb[:, h * SSD_HEAD_DIM:(h + 1) * SSD_HEAD_DIM],
                                 preferred_element_type=F32))
        y_parts.append(jnp.concatenate(heads, axis=1) + y_off * e_acum[:, g * gw:(g + 1) * gw])
    y = jnp.concatenate(y_parts, axis=1)
    state_ref[...] = state_ref[...] * e_last + jnp.concatenate(new_state, axis=1)

    y = (y + dsk_ref[...] * xs) * _silu(z_ref[...])
    ms = jnp.mean(y * y, axis=-1, keepdims=True)
    y_ref[...] = (y * lax.rsqrt(ms + EPS) * nw_ref[...]).astype(y_ref.dtype)

    @pl.when(c == nc - 1)
    def _():
        st_ref[...] = state_ref[...]


def ssd(proj, hist8, state_t, conv_w, conv_b, dt_bias, a_log, d_skip, ssd_norm_w, *, nbatch, q, valid):
    t = proj.shape[0]
    nc = t // (nbatch * q)
    hp = SSD_INNER

    def pad_heads(v, fill):
        return jnp.concatenate([v.astype(F32), jnp.full((LANES - SSD_HEADS,), fill, F32)]).reshape(1, LANES)

    expand = (jnp.arange(LANES)[:, None] == (jnp.arange(hp)[None, :] // SSD_HEAD_DIM)).astype(F32)
    dsk = jnp.repeat(d_skip.astype(F32), SSD_HEAD_DIM).reshape(1, hp)
    kern = functools.partial(_ssd_kernel, q=q, valid=valid)
    const2 = lambda shape: pl.BlockSpec(shape, lambda b, c: (0, 0))
    return pl.pallas_call(
        kern,
        grid=(nbatch, nc),
        in_specs=[pl.BlockSpec((q, hp), lambda b, c: (b * nc + c, COL["xs"] // hp)),
                  pl.BlockSpec((q, 256), lambda b, c: (b * nc + c, COL["bm"] // 256)),
                  pl.BlockSpec((q, 256), lambda b, c: (b * nc + c, COL["cm"] // 256)),
                  pl.BlockSpec((q, LANES), lambda b, c: (b * nc + c, COL["dt"] // LANES)),
                  pl.BlockSpec((q, hp), lambda b, c: (b * nc + c, COL["z"] // hp)),
                  pl.BlockSpec((None, 8, CONV_DIM), lambda b, c: (b, 0, 0)),
                  pl.BlockSpec((None, SSD_STATE, hp), lambda b, c: (b, 0, 0)),
                  const2((CONV_WIDTH, CONV_DIM)), const2((1, CONV_DIM)),
                  const2((1, LANES)), const2((1, LANES)), const2((1, hp)), const2((1, hp)),
                  const2((LANES, hp))],
        out_specs=[pl.BlockSpec((q, hp), lambda b, c: (b * nc + c, 0)),
                   pl.BlockSpec((None, SSD_STATE, hp), lambda b, c: (b, 0, 0))],
        out_shape=[jax.ShapeDtypeStruct((t, hp), BF16),
                   jax.ShapeDtypeStruct((nbatch, SSD_STATE, hp), F32)],
        scratch_shapes=[pltpu.VMEM((q + 8, CONV_DIM), F32),
                        pltpu.VMEM((SSD_STATE, hp), F32)],
        compiler_params=_cparams(("arbitrary", "arbitrary")),
        name="ssd",
    )(proj, proj, proj, proj, proj, hist8, state_t, conv_w, conv_b.reshape(1, CONV_DIM),
      pad_heads(dt_bias, 0.0), pad_heads(a_log, 0.0), dsk, ssd_norm_w.reshape(1, hp), expand)


def _pool_kernel(u_ref, pg_ref, hist_ref, w_ref, b_ref, sc_ref, y_ref, ext_ref, *, r, start):
    c = pl.program_id(1)

    @pl.when(c == 0)
    def _():
        ext_ref[0:16, :] = hist_ref[...]

    ext_ref[16:16 + r, :] = u_ref[...]
    pos = start + c * r + lax.broadcasted_iota(I32, (r, 1), 0)
    outs = []
    for gi, w in enumerate(POOL_WINDOWS):
        lo = gi * POOL_GROUP_DIM
        cur = ext_ref[16:16 + r, lo:lo + POOL_GROUP_DIM]
        win = cur
        for s in range(1, w):
            win = win + ext_ref[16 - s:16 - s + r, lo:lo + POOL_GROUP_DIM]
        cnt = jnp.minimum(w, pos + 1).astype(F32)
        pooled = win / cnt - cur
        mixed = jnp.dot(pooled.astype(BF16), w_ref[gi], preferred_element_type=F32)
        outs.append(mixed + b_ref[gi:gi + 1, :])
    ext_ref[0:16, :] = ext_ref[r:r + 16, :]
    mixed = jnp.concatenate(outs, axis=1) * sc_ref[...]
    y_ref[...] = (mixed * _silu(pg_ref[...])).astype(y_ref.dtype)


def pool(proj, hist16, pool_w, pool_b, pool_scale, *, nbatch, r, start):
    t = proj.shape[0]
    nc = t // (nbatch * r)
    d = BRANCH_DIM
    kern = functools.partial(_pool_kernel, r=r, start=start)
    return pl.pallas_call(
        kern,
        grid=(nbatch, nc),
        in_specs=[pl.BlockSpec((r, d), lambda b, c: (b * nc + c, COL["u"] // d)),
                  pl.BlockSpec((r, d), lambda b, c: (b * nc + c, COL["pg"] // d)),
                  pl.BlockSpec((None, 16, d), lambda b, c: (b, 0, 0)),
                  pl.BlockSpec((POOL_GROUPS, POOL_GROUP_DIM, POOL_GROUP_DIM), lambda b, c: (0, 0, 0)),
                  pl.BlockSpec((POOL_GROUPS, POOL_GROUP_DIM), lambda b, c: (0, 0)),
                  pl.BlockSpec((1, d), lambda b, c: (0, 0))],
        out_specs=pl.BlockSpec((r, d), lambda b, c: (b * nc + c, 0)),
        out_shape=jax.ShapeDtypeStruct((t, d), BF16),
        scratch_shapes=[pltpu.VMEM((r + 16, d), F32)],
        compiler_params=_cparams(("arbitrary", "arbitrary")),
        name="pool",
    )(proj, proj, hist16, pool_w.astype(BF16), pool_b, pool_scale.reshape(1, d))


def _merge_kernel(y0_ref, y1_ref, y2_ref, g0_ref, g1_ref, g2_ref, w_ref, o_ref):
    acc = jax.nn.sigmoid(g0_ref[...]) * jnp.dot(y0_ref[...], w_ref[0], preferred_element_type=F32)
    acc = acc + jax.nn.sigmoid(g1_ref[...]) * jnp.dot(y1_ref[...], w_ref[1], preferred_element_type=F32)
    acc = acc + jax.nn.sigmoid(g2_ref[...]) * jnp.dot(y2_ref[...], w_ref[2], preferred_element_type=F32)
    o_ref[...] = acc.astype(o_ref.dtype)


def merge(y_ssd, y_att, y_pool, proj, w_branch, tm):
    t = y_ssd.shape[0]
    tn = 512 if tm > 512 else 1024
    nj = D_MODEL // tn
    ysp = pl.BlockSpec((tm, BRANCH_DIM), lambda i, j: (i, 0))

    def gate_spec(bi):
        return pl.BlockSpec((tm, tn), lambda i, j: (i, (COL["mg"] + bi * D_MODEL) // tn + j))

    return pl.pallas_call(
        _merge_kernel,
        grid=(t // tm, nj),
        in_specs=[ysp, ysp, ysp, gate_spec(0), gate_spec(1), gate_spec(2),
                  pl.BlockSpec((N_BRANCH, BRANCH_DIM, tn), lambda i, j: (0, 0, j))],
        out_specs=pl.BlockSpec((tm, tn), lambda i, j: (i, j)),
        out_shape=jax.ShapeDtypeStruct((t, D_MODEL), BF16),
        compiler_params=_cparams(("arbitrary", "arbitrary")),
        name="merge",
    )(y_ssd, y_att, y_pool, proj, proj, proj, w_branch)


def _outproj_kernel(m_ref, w_ref, x_ref, g_ref, o_ref):
    o_ref[...] = x_ref[...] + g_ref[...] * jnp.dot(m_ref[...], w_ref[...], preferred_element_type=F32)


def outproj(merged, w_out, x, gate, tm, rows_per_mod):
    t = x.shape[0]
    tn = 1024
    return pl.pallas_call(
        _outproj_kernel,
        grid=(t // tm, D_MODEL // tn),
        in_specs=[pl.BlockSpec((tm, D_MODEL), lambda i, j: (i, 0)),
                  pl.BlockSpec((D_MODEL, tn), lambda i, j: (0, j)),
                  pl.BlockSpec((tm, tn), lambda i, j: (i, j)),
                  _mod_spec(gate, tm, tn, rows_per_mod, lambda j: j)],
        out_specs=pl.BlockSpec((tm, tn), lambda i, j: (i, j)),
        out_shape=jax.ShapeDtypeStruct((t, D_MODEL), F32),
        compiler_params=_cparams(("arbitrary", "arbitrary")),
        name="outproj",
    )(merged, w_out, x, gate)


def _pad_to(a, axis, size):
    pad = [(0, 0)] * a.ndim
    pad[axis] = (0, size - a.shape[axis])
    return jnp.pad(a, pad)


def trunk_layer(x, mod, k_past, v_past, ik_past, h0, conv_hist, pool_hist, rel_bias, lw, *, per_row_mod):
    (norm_w, w_in, conv_w, conv_b, dt_bias, a_log, d_skip, ssd_norm_w, q_norm_w, k_norm_w,
     pool_w, pool_b, pool_scale, w_branch, w_out) = lw
    bsz, seq, d = x.shape
    t = bsz * seq
    start = k_past.shape[1]
    n_keys = start + seq
    topk = min(TOPK_MAX, n_keys // 4)
    x2 = x.reshape(t, d)
    shift, scale, gate = mod[:, :d], mod[:, d:2 * d], mod[:, 2 * d:]
    tm = min(1024, t)
    tm_prep = min(512, t)
    if per_row_mod:
        expand = lambda m: jnp.broadcast_to(m[:, None, :], (bsz, seq, d)).reshape(t // tm, tm, d)
    else:
        expand = lambda m: m[:, None, :]
    scale3, shift3, gate3 = expand(scale), expand(shift), expand(gate)

    proj = inproj(x2, scale3, shift3, norm_w, w_in, tm, seq)

    kw = KV_HEADS * HEAD_DIM
    tq = min(256, seq)
    if seq >= 2048:
        tk, ta = 2048, 512
    else:
        tk = ta = None
    s_pad = -(-n_keys // TKS) * TKS
    if tk is None:
        tk, ta = s_pad, s_pad // 3 if (s_pad // 3) % TKS == 0 else s_pad
    s_pad = -(-s_pad // tk) * tk
    bk = near_buckets(tq, start, n_keys)
    dsa_args = dict(nbatch=bsz, tq=tq, ta=ta, tk=tk, start=start, n_keys=n_keys, topk=topk)
    lanes_layout = tq % LANES == 0
    if lanes_layout and start == 0 and s_pad == seq:
        qT, iqT, iwT, k_new, v_new, ik_new, k_aug, v_aug, ik_b = prep_t(proj, q_norm_w, k_norm_w, tm_prep, bsz)
        y_att = dsa_t(qT, iqT, iwT, proj, ik_b, k_aug, v_aug, bk.T, rel_bias, **dsa_args)
    else:
        qh, k_new, iqh = prep(proj, q_norm_w, k_norm_w, tm_prep)
        v_new = proj[:, COL["v"]:COL["v"] + kw]
        ik_new = proj[:, COL["ik"]:COL["ik"] + IDX_DIM]
    k_new4 = k_new.reshape(bsz, seq, KV_HEADS, HEAD_DIM)
    v_new4 = v_new.reshape(bsz, seq, KV_HEADS, HEAD_DIM)
    ik_new3 = ik_new.reshape(bsz, seq, IDX_DIM)
    if not (lanes_layout and start == 0 and s_pad == seq):
        k_all = jnp.concatenate([k_past.astype(F32), k_new4], axis=1).astype(BF16)
        v_all = jnp.concatenate([v_past.astype(F32), v_new4], axis=1).astype(BF16)
        ik_all = jnp.concatenate([ik_past.astype(F32), ik_new3], axis=1).astype(BF16)
        if lanes_layout:
            k4 = _pad_to(jnp.transpose(k_all, (0, 2, 1, 3)), 2, s_pad)
            vT = _pad_to(jnp.transpose(v_all, (0, 2, 3, 1)), 3, s_pad)
            k_aug, v_aug = augment_kv(k4, vT, n_keys)
            ik_p = _pad_to(ik_all, 1, s_pad)
            iwT = proj[:, COL["iw"]:COL["iw"] + IDX_HEADS].T
            y_att = dsa_t(jnp.swapaxes(qh, 1, 2), jnp.swapaxes(iqh, 1, 2), iwT, proj, ik_p, k_aug, v_aug,
                          bk.T, rel_bias, **dsa_args)
        else:
            kT = _pad_to(jnp.transpose(k_all, (0, 2, 3, 1)), 3, s_pad)
            v4 = _pad_to(jnp.transpose(v_all, (0, 2, 1, 3)), 2, s_pad)
            ikT = _pad_to(jnp.transpose(ik_all, (0, 2, 1)), 2, s_pad)
            y_att = dsa(qh, iqh, proj, ikT, kT, v4, bk, rel_bias, **dsa_args)

    q = 128
    hist8 = jnp.pad(conv_hist.astype(F32), ((0, 0), (8 - (CONV_WIDTH - 1), 0), (0, 0)))
    state_t = jnp.transpose(h0.astype(F32), (0, 3, 1, 2)).reshape(bsz, SSD_STATE, SSD_INNER)
    if seq < q:
        proj_ssd = _pad_to(proj.reshape(bsz, seq, PROJ_DIM), 1, q).reshape(bsz * q, PROJ_DIM)
        valid = seq
    else:
        proj_ssd, valid = proj, q
    y_ssd, st = ssd(proj_ssd, hist8, state_t, conv_w, conv_b, dt_bias, a_log, d_skip, ssd_norm_w,
                    nbatch=bsz, q=q, valid=valid)
    if seq < q:
        y_ssd = y_ssd.reshape(bsz, q, SSD_INNER)[:, :seq].reshape(t, SSD_INNER)
    h_last = jnp.transpose(st.reshape(bsz, SSD_STATE, SSD_HEADS, SSD_HEAD_DIM), (0, 2, 3, 1))

    hist16 = jnp.pad(pool_hist.astype(F32), ((0, 0), (16 - POOL_STATE, 0), (0, 0)))
    y_pool = pool(proj, hist16, pool_w, pool_b, pool_scale, nbatch=bsz, r=min(512, seq), start=start)

    merged = merge(y_ssd, y_att, y_pool, proj, w_branch, tm)
    x_new = outproj(merged, w_out, x2, gate3, tm, seq).reshape(bsz, seq, d)

    xbc = proj[:, COL["xbc"]:COL["xbc"] + CONV_DIM].reshape(bsz, seq, CONV_DIM)
    u = proj[:, COL["u"]:COL["u"] + BRANCH_DIM].reshape(bsz, seq, BRANCH_DIM)
    conv_new = jnp.concatenate([conv_hist.astype(F32), xbc], axis=1)[:, -(CONV_WIDTH - 1):]
    pool_new = jnp.concatenate([pool_hist.astype(F32), u[:, -min(seq, POOL_STATE):]], axis=1)[:, -POOL_STATE:]
    return x_new, k_new4, v_new4, ik_new3, h_last, conv_new, pool_new


def _reorder_w_in(w_in):
    parts = []
    for name in _NEW_ORDER:
        off, size = _ORIG[name]
        seg = w_in[..., off:off + size]
        padw = -(-size // LANES) * LANES - size
        if padw:
            seg = jnp.pad(seg, ((0, 0), (0, 0), (0, padw)))
        parts.append(seg)
    out = jnp.concatenate(parts, axis=-1)
    return _pad_to(out, 2, PROJ_DIM).astype(BF16)


def kernel(x_prompt, x_sample, cache_k, cache_v, cache_idx_k, state_ssm, state_conv, state_pool,
           c_prompt, c_sample, rel_bias, w_ada, b_ada, norm_w, w_in, conv_w, conv_b, dt_bias,
           a_log, d_skip, ssd_norm_w, q_norm_w, k_norm_w, pool_w, pool_b, pool_scale,
           w_branch, w_out):
    bp = x_prompt.shape[0]
    f32 = F32
    mods = ada_mod(jnp.concatenate([c_prompt, c_sample], axis=0), w_ada, b_ada)
    w_in_r = _reorder_w_in(w_in)
    w_branch_b = w_branch.astype(BF16)
    w_out_b = w_out.astype(BF16)

    empty_kv = jnp.zeros((bp, 0, KV_HEADS, HEAD_DIM), f32)
    empty_ik = jnp.zeros((bp, 0, IDX_DIM), f32)
    zero_ssm = jnp.zeros((bp, SSD_HEADS, SSD_HEAD_DIM, SSD_STATE), f32)
    zero_conv = jnp.zeros((bp, CONV_WIDTH - 1, CONV_DIM), f32)
    zero_pool = jnp.zeros((bp, POOL_STATE, BRANCH_DIM), f32)

    xp, xs = x_prompt, x_sample
    outs_p = [[] for _ in range(6)]
    outs_s = [[] for _ in range(6)]
    for l in range(DEPTH):
        lw = (norm_w[l], w_in_r[l], conv_w[l], conv_b[l], dt_bias[l], a_log[l], d_skip[l], ssd_norm_w[l],
              q_norm_w[l], k_norm_w[l], pool_w[l], pool_b[l], pool_scale[l], w_branch_b[l], w_out_b[l])
        rp = trunk_layer(xp, mods[l, :bp], empty_kv, empty_kv, empty_ik, zero_ssm, zero_conv, zero_pool,
                         rel_bias, lw, per_row_mod=False)
        rs = trunk_layer(xs, mods[l, bp:], cache_k[l], cache_v[l], cache_idx_k[l], state_ssm[l],
                         state_conv[l], state_pool[l], rel_bias, lw, per_row_mod=True)
        xp, xs = rp[0], rs[0]
        for n in range(6):
            outs_p[n].append(rp[n + 1])
            outs_s[n].append(rs[n + 1])
    return (xp, xs, *[jnp.stack(o) for o in outs_p], *[jnp.stack(o) for o in outs_s])
```
